```python
import jax
import jax.numpy as jnp
from jax import lax
import numpy as np

D_MODEL = 2048
BATCH = 8
SEQ = 2048
DEPTH = 1

D_RNN = D_MODEL
LRU_BLOCKS = 16
LRU_BLOCK_DIM = D_RNN // LRU_BLOCKS
CONV_WIDTH = 4
LRU_C = 8.0
N_HEADS = 16
HEAD_DIM = 128
N_KV_GROUPS = 4
HEADS_PER_GROUP = N_HEADS // N_KV_GROUPS
Q_WIDTH = N_HEADS * HEAD_DIM
KV_WIDTH = N_KV_GROUPS * HEAD_DIM
CMP_STRIDE = 16
CMP_BLOCK = 2 * CMP_STRIDE
SLC_BLOCK = 64
N_SELECT = 16
WINDOW = 512
WIN_Q_BLOCK = 128
SLC_Q_BLOCK = 32
ROPE_THETA = 500000.0
ROPE_DIM = HEAD_DIM // 4
D_FF = 5632
NORM_EPS = 1e-6
N_ADA = 9
SPLIT_SIZES = (D_RNN, D_RNN, Q_WIDTH, 6 * KV_WIDTH, 3 * N_HEADS, 2 * D_MODEL)
IN_WIDTH = 2 * D_RNN + Q_WIDTH + 6 * KV_WIDTH + 3 * N_HEADS + 2 * D_MODEL
NEG = -1e30

kernel_name = "hybrid_rglru_nsa_macaron_block"


def rms_norm(x, g):
    x32 = x.astype(jnp.float32)
    y = x32 * lax.rsqrt(jnp.mean(x32 * x32, axis=-1, keepdims=True) + NORM_EPS)
    return (y * g.astype(jnp.float32)).astype(x.dtype)


def modulate(h, shift, scale):
    return h * (1 + scale[:, None, :]) + shift[:, None, :]


def swiglu(u, w_gate, w_up, w_down):
    return (jax.nn.silu(u @ w_gate) * (u @ w_up)) @ w_down


def masked_softmax(s, mask):
    p = jax.nn.softmax(jnp.where(mask, s, NEG), axis=-1)
    return jnp.where(mask, p, 0.0)


def partial_rope(t, cos, sin):
    half = ROPE_DIM // 2
    c = cos[None, :, None, :].astype(t.dtype)
    s = sin[None, :, None, :].astype(t.dtype)
    t1 = t[..., :half]
    t2 = t[..., half:ROPE_DIM]
    return jnp.concatenate([t1 * c - t2 * s, t2 * c + t1 * s, t[..., ROPE_DIM:]], axis=-1)


def rglru_branch(xa, ya, conv_w, conv_b, wr, br, wi, bi, lam):
    B, S, _ = xa.shape
    xc = lax.conv_general_dilated(
        xa, conv_w[:, None, :], window_strides=(1,), padding=[(CONV_WIDTH - 1, 0)],
        dimension_numbers=('NWC', 'WIO', 'NWC'), feature_group_count=D_RNN) + conv_b
    xb = xc.reshape(B, S, LRU_BLOCKS, LRU_BLOCK_DIM)
    r = jax.nn.sigmoid(jnp.einsum('bshi,hij->bshj', xb, wr).reshape(B, S, D_RNN) + br)
    i = jax.nn.sigmoid(jnp.einsum('bshi,hij->bshj', xb, wi).reshape(B, S, D_RNN) + bi)
    log_a = (-LRU_C * r.astype(jnp.float32)) * jax.nn.softplus(-lam.astype(jnp.float32))
    a = jnp.exp(log_a)
    b = jnp.sqrt(-jnp.expm1(2.0 * log_a)) * (i * xc).astype(jnp.float32)

    def combine(left, right):
        a1, b1 = left
        a2, b2 = right
        return a1 * a2, a2 * b1 + b2

    _, h = lax.associative_scan(combine, (a, b), axis=1)
    return h.astype(xa.dtype) * jax.nn.gelu(ya)


def compress(t, pe, w1, w2):
    B, G, S, dh = t.shape
    ch = t.reshape(B, G, S // CMP_STRIDE, CMP_STRIDE, dh)
    blocks = jnp.concatenate([ch[:, :, :-1], ch[:, :, 1:]], axis=3) + pe
    flat = blocks.reshape(B, G, blocks.shape[2], CMP_BLOCK * dh)
    return jax.nn.gelu(flat @ w1) @ w2


def nsa_branch(q, kv, gate_logits, cos, sin, pe_k, w1_k, w2_k, pe_v, w1_v, w2_v):
    B, S, _ = q.shape
    G, HPG, DH = N_KV_GROUPS, HEADS_PER_GROUP, HEAD_DIM
    scale = HEAD_DIM ** -0.5
    pos = jnp.arange(S)

    q = partial_rope(q.reshape(B, S, N_HEADS, DH), cos, sin)
    qg = q.reshape(B, S, G, HPG, DH).transpose(0, 2, 3, 1, 4)
    k_cmp, v_cmp, k_slc, v_slc, k_win, v_win = jnp.split(kv, 6, axis=-1)

    def kv_heads(t, rotate):
        t = t.reshape(B, S, G, DH)
        if rotate:
            t = partial_rope(t, cos, sin)
        return t.transpose(0, 2, 1, 3)

    k_cmp, k_slc, k_win = kv_heads(k_cmp, True), kv_heads(k_slc, True), kv_heads(k_win, True)
    v_cmp, v_slc, v_win = kv_heads(v_cmp, False), kv_heads(v_slc, False), kv_heads(v_win, False)

    kc = compress(k_cmp, pe_k, w1_k, w2_k)
    vc = compress(v_cmp, pe_v, w1_v, w2_v)
    n_cmp = kc.shape[2]
    cmp_start = jnp.arange(n_cmp) * CMP_STRIDE
    cmp_mask = (cmp_start + CMP_BLOCK - 1)[None, :] <= pos[:, None]
    s_cmp = jnp.einsum('bghsd,bgnd->bghsn', qg, kc, preferred_element_type=jnp.float32) * scale
    p_cmp = masked_softmax(s_cmp, cmp_mask)
    o_cmp = jnp.einsum('bghsn,bgnd->bghsd', p_cmp.astype(vc.dtype), vc)

    n_slc = S // SLC_BLOCK
    slc_start = jnp.arange(n_slc) * SLC_BLOCK
    overlap = ((cmp_start[:, None] < slc_start[None, :] + SLC_BLOCK)
               & (cmp_start[:, None] + CMP_BLOCK > slc_start[None, :])).astype(jnp.float32)
    imp = jnp.einsum('bghsn,nj->bgsj', p_cmp, overlap)
    cur = pos // SLC_BLOCK
    blk = jnp.arange(n_slc)
    forced = (blk[None, :] == 0) | (blk[None, :] == cur[:, None]) | (blk[None, :] == cur[:, None] - 1)
    causal_blk = slc_start[None, :] <= pos[:, None]
    imp = jnp.where(forced, jnp.inf, jnp.where(causal_blk, imp, -jnp.inf))
    n_sel = min(N_SELECT, n_slc)
    _, sel_idx = lax.top_k(imp, n_sel)

    kb = k_slc.reshape(B, G, n_slc, SLC_BLOCK, DH)
    vb = v_slc.reshape(B, G, n_slc, SLC_BLOCK, DH)
    nqc = S // SLC_Q_BLOCK
    q_chunks = qg.reshape(B, G, HPG, nqc, SLC_Q_BLOCK, DH).transpose(3, 0, 1, 2, 4, 5)
    idx_chunks = sel_idx.reshape(B, G, nqc, SLC_Q_BLOCK, n_sel).transpose(2, 0, 1, 3, 4)
    pos_chunks = pos.reshape(nqc, SLC_Q_BLOCK)
    gather = jax.vmap(jax.vmap(lambda blocks, ids: blocks[ids]))

    def slc_chunk(args):
        qc, ic, pc = args
        kg = gather(kb, ic)
        vg = gather(vb, ic)
        s = jnp.einsum('bghqd,bgqnld->bghqnl', qc, kg, preferred_element_type=jnp.float32) * scale
        kpos = ic[..., None] * SLC_BLOCK + jnp.arange(SLC_BLOCK)
        mask = (kpos <= pc[None, None, :, None, None]).reshape(B, G, 1, SLC_Q_BLOCK, n_sel * SLC_BLOCK)
        p = masked_softmax(s.reshape(B, G, HPG, SLC_Q_BLOCK, n_sel * SLC_BLOCK), mask)
        return jnp.einsum('bghqnl,bgqnld->bghqd', p.reshape(s.shape).astype(vg.dtype), vg)

    o_slc = lax.map(slc_chunk, (q_chunks, idx_chunks, pos_chunks))
    o_slc = o_slc.transpose(1, 2, 3, 0, 4, 5).reshape(B, G, HPG, S, DH)

    nqb = S // WIN_Q_BLOCK
    span = WINDOW + WIN_Q_BLOCK
    kp = jnp.pad(k_win, ((0, 0), (0, 0), (WINDOW, 0), (0, 0)))
    vp = jnp.pad(v_win, ((0, 0), (0, 0), (WINDOW, 0), (0, 0)))
    qb = qg.reshape(B, G, HPG, nqb, WIN_Q_BLOCK, DH).transpose(3, 0, 1, 2, 4, 5)

    def win_chunk(args):
        qc, b = args
        start = b * WIN_Q_BLOCK
        kw = lax.dynamic_slice_in_dim(kp, start, span, axis=2)
        vw = lax.dynamic_slice_in_dim(vp, start, span, axis=2)
        qpos = start + jnp.arange(WIN_Q_BLOCK)
        kpos = start - WINDOW + jnp.arange(span)
        mask = ((kpos[None, :] <= qpos[:, None]) & (kpos[None, :] > qpos[:, None] - WINDOW)
                & (kpos[None, :] >= 0))
        s = jnp.einsum('bghqd,bgkd->bghqk', qc, kw, preferred_element_type=jnp.float32) * scale
        p = masked_softmax(s, mask)
        return jnp.einsum('bghqk,bgkd->bghqd', p.astype(vw.dtype), vw)

    o_win = lax.map(win_chunk, (qb, jnp.arange(nqb)))
    o_win = o_win.transpose(1, 2, 3, 0, 4, 5).reshape(B, G, HPG, S, DH)

    g = jax.nn.sigmoid(gate_logits).reshape(B, S, G, HPG, 3).transpose(0, 2, 3, 1, 4)
    o = g[..., 0:1] * o_cmp + g[..., 1:2] * o_slc + g[..., 2:3] * o_win
    return o.transpose(0, 3, 1, 2, 4).reshape(B, S, Q_WIDTH)


def setup_inputs(seed: int = 0) -> dict:
    key = jax.random.key(seed)
    ks = jax.random.split(key, 33)
    f32 = jnp.float32

    def nrm(k, shape, scale):
        return jax.random.normal(k, shape, f32) * scale

    def gain(k):
        return 1.0 + 0.05 * jax.random.normal(k, (DEPTH, D_MODEL), f32)

    u = jax.random.uniform(ks[18], (DEPTH, D_RNN), f32, minval=0.9, maxval=0.999)
    p = u ** (1.0 / LRU_C)
    lam = jnp.log(p) - jnp.log1p(-p)
    return {
        "x": nrm(ks[0], (BATCH, SEQ, D_MODEL), 1.0),
        "c": nrm(ks[1], (BATCH, D_MODEL), 1.0),
        "w_ada": nrm(ks[2], (DEPTH, D_MODEL, N_ADA * D_MODEL), D_MODEL ** -0.5),
        "b_ada": nrm(ks[3], (DEPTH, N_ADA * D_MODEL), 0.02),
        "ffn1_pre_g": gain(ks[4]),
        "ffn1_post_g": gain(ks[5]),
        "ffn1_w_gate": nrm(ks[6], (DEPTH, D_MODEL, D_FF), D_MODEL ** -0.5),
        "ffn1_w_up": nrm(ks[7], (DEPTH, D_MODEL, D_FF), D_MODEL ** -0.5),
        "ffn1_w_down": nrm(ks[8], (DEPTH, D_FF, D_MODEL), D_FF ** -0.5),
        "mix_pre_g": gain(ks[9]),
        "mix_post_g": gain(ks[10]),
        "w_in": nrm(ks[11], (DEPTH, D_MODEL, IN_WIDTH), D_MODEL ** -0.5),
        "conv_w": nrm(ks[12], (DEPTH, CONV_WIDTH, D_RNN), CONV_WIDTH ** -0.5),
        "conv_b": nrm(ks[13], (DEPTH, D_RNN), 0.02),
        "lru_wr": nrm(ks[14], (DEPTH, LRU_BLOCKS, LRU_BLOCK_DIM, LRU_BLOCK_DIM), LRU_BLOCK_DIM ** -0.5),
        "lru_br": nrm(ks[15], (DEPTH, D_RNN), 0.02),
        "lru_wi": nrm(ks[16], (DEPTH, LRU_BLOCKS, LRU_BLOCK_DIM, LRU_BLOCK_DIM), LRU_BLOCK_DIM ** -0.5),
        "lru_bi": nrm(ks[17], (DEPTH, D_RNN), 0.02),
        "lru_lambda": lam,
        "cmp_pe_k": nrm(ks[19], (DEPTH, CMP_BLOCK, HEAD_DIM), 0.02),
        "cmp_w1_k": nrm(ks[20], (DEPTH, CMP_BLOCK * HEAD_DIM, HEAD_DIM), (CMP_BLOCK * HEAD_DIM) ** -0.5),
        "cmp_w2_k": nrm(ks[21], (DEPTH, HEAD_DIM, HEAD_DIM), HEAD_DIM ** -0.5),
        "cmp_pe_v": nrm(ks[22], (DEPTH, CMP_BLOCK, HEAD_DIM), 0.02),
        "cmp_w1_v": nrm(ks[23], (DEPTH, CMP_BLOCK * HEAD_DIM, HEAD_DIM), (CMP_BLOCK * HEAD_DIM) ** -0.5),
        "cmp_w2_v": nrm(ks[24], (DEPTH, HEAD_DIM, HEAD_DIM), HEAD_DIM ** -0.5),
        "w_a_out": nrm(ks[25], (DEPTH, D_RNN, D_MODEL), D_RNN ** -0.5),
        "w_b_out": nrm(ks[26], (DEPTH, Q_WIDTH, D_MODEL), Q_WIDTH ** -0.5),
        "w_out": nrm(ks[27], (DEPTH, D_MODEL, D_MODEL), D_MODEL ** -0.5),
        "ffn2_pre_g": gain(ks[28]),
        "ffn2_post_g": gain(ks[29]),
        "ffn2_w_gate": nrm(ks[30], (DEPTH, D_MODEL, D_FF), D_MODEL ** -0.5),
        "ffn2_w_up": nrm(ks[31], (DEPTH, D_MODEL, D_FF), D_MODEL ** -0.5),
        "ffn2_w_down": nrm(ks[32], (DEPTH, D_FF, D_MODEL), D_FF ** -0.5),
    }


def reference(x, c, w_ada, b_ada,
              ffn1_pre_g, ffn1_post_g, ffn1_w_gate, ffn1_w_up, ffn1_w_down,
              mix_pre_g, mix_post_g, w_in, conv_w, conv_b,
              lru_wr, lru_br, lru_wi, lru_bi, lru_lambda,
              cmp_pe_k, cmp_w1_k, cmp_w2_k, cmp_pe_v, cmp_w1_v, cmp_w2_v,
              w_a_out, w_b_out, w_out,
              ffn2_pre_g, ffn2_post_g, ffn2_w_gate, ffn2_w_up, ffn2_w_down):
    B, S, D = x.shape
    pos = jnp.arange(S).astype(jnp.float32)
    inv_freq = ROPE_THETA ** (-jnp.arange(0, ROPE_DIM, 2, dtype=jnp.float32) / ROPE_DIM)
    ang = pos[:, None] * inv_freq[None, :]
    cos, sin = jnp.cos(ang), jnp.sin(ang)
    offsets = np.cumsum(SPLIT_SIZES)[:-1].tolist()
    c_act = jax.nn.silu(c)

    for l in range(DEPTH):
        mod = (c_act @ w_ada[l] + b_ada[l]).reshape(B, N_ADA, D)
        sh1, sc1, g1 = mod[:, 0], mod[:, 1], mod[:, 2]
        sh2, sc2, g2 = mod[:, 3], mod[:, 4], mod[:, 5]
        sh3, sc3, g3 = mod[:, 6], mod[:, 7], mod[:, 8]

        u = modulate(rms_norm(x, ffn1_pre_g[l]), sh1, sc1)
        f = swiglu(u, ffn1_w_gate[l], ffn1_w_up[l], ffn1_w_down[l])
        x = x + 0.5 * g1[:, None, :] * rms_norm(f, ffn1_post_g[l])

        u = modulate(rms_norm(x, mix_pre_g[l]), sh2, sc2)
        xa, ya, q, kv, nsa_g, merge_g = jnp.split(u @ w_in[l], offsets, axis=-1)
        y_a = rglru_branch(xa, ya, conv_w[l], conv_b[l], lru_wr[l], lru_br[l],
                           lru_wi[l], lru_bi[l], lru_lambda[l]) @ w_a_out[l]
        y_b = nsa_branch(q, kv, nsa_g, cos, sin, cmp_pe_k[l], cmp_w1_k[l], cmp_w2_k[l],
                         cmp_pe_v[l], cmp_w1_v[l], cmp_w2_v[l]) @ w_b_out[l]
        gate_a, gate_b = jnp.split(jax.nn.sigmoid(merge_g), 2, axis=-1)
        mixed = (gate_a * y_a + gate_b * y_b) @ w_out[l]
        x = x + g2[:, None, :] * rms_norm(mixed, mix_post_g[l])

        u = modulate(rms_norm(x, ffn2_pre_g[l]), sh3, sc3)
        f = swiglu(u, ffn2_w_gate[l], ffn2_w_up[l], ffn2_w_down[l])
        x = x + 0.5 * g3[:, None, :] * rms_norm(f, ffn2_post_g[l])
    return x
```

```python
import functools
import math

import jax
import jax.numpy as jnp
from jax import lax
from jax.experimental import pallas as pl
from jax.experimental.pallas import tpu as pltpu

F32 = jnp.float32
BF16 = jnp.bfloat16

D_MODEL = 2048
BATCH = 8
SEQ = 2048
D_RNN = D_MODEL
LRU_BLOCKS = 16
LRU_BLOCK_DIM = D_RNN // LRU_BLOCKS
CONV_WIDTH = 4
LRU_C = 8.0
N_HEADS = 16
HEAD_DIM = 128
N_KV_GROUPS = 4
HEADS_PER_GROUP = N_HEADS // N_KV_GROUPS
Q_WIDTH = N_HEADS * HEAD_DIM
KV_WIDTH = N_KV_GROUPS * HEAD_DIM
CMP_STRIDE = 16
CMP_BLOCK = 2 * CMP_STRIDE
SLC_BLOCK = 64
N_SELECT = 16
WINDOW = 512
ROPE_THETA = 500000.0
ROPE_DIM = HEAD_DIM // 4
D_FF = 5632
NORM_EPS = 1e-6
N_ADA = 9
N_SLC = SEQ // SLC_BLOCK
N_CMP_PAD = SEQ // CMP_STRIDE

LANES = 128
SUBLANES = 8

COL_XA = 0
COL_YA = COL_XA + D_RNN
COL_Q = COL_YA + D_RNN
COL_KV = COL_Q + Q_WIDTH
COL_MG = COL_KV + 6 * KV_WIDTH
COL_NG = COL_MG + 2 * D_MODEL
PROJ_TN = 512
NG_PAD = PROJ_TN
PROJ_W = COL_NG + NG_PAD

MASK_NEG = -1e30
M_INIT = -1e29

FFN_TS = 512
FFN_TF = 512
PROJ_TS = 1024
OUT_TS = 512
ATT_T = 256
LRU_TT = 256
LRU_CW = 512
MOD_TN = 1024


def _vmem(mb):
    return mb * 1024 * 1024


def _rms(x, g):
    return x * lax.rsqrt(jnp.mean(x * x, axis=-1, keepdims=True) + NORM_EPS) * g


def _gelu_tanh(x):
    c = math.sqrt(2.0 / math.pi)
    return x * (0.5 * (1.0 + jnp.tanh(c * (x + 0.044715 * (x * x * x)))))


def _bdot(a, b):
    return jnp.dot(a, b, preferred_element_type=F32)


def _bdot_nt(a, b):
    return lax.dot_general(a, b, (((1,), (1,)), ((), ())), preferred_element_type=F32)


def _mod_kernel(c_ref, w_ref, b_ref, o_ref):
    c = c_ref[...]
    ca = c * jax.nn.sigmoid(c)
    o_ref[...] = _bdot(ca.astype(BF16), w_ref[...].astype(BF16)) + b_ref[...]


def _modulation(c, w_ada, b_ada):
    n = N_ADA * D_MODEL
    return pl.pallas_call(
        _mod_kernel,
        grid=(n // MOD_TN,),
        in_specs=[
            pl.BlockSpec((BATCH, D_MODEL), lambda j: (0, 0)),
            pl.BlockSpec((D_MODEL, MOD_TN), lambda j: (0, j)),
            pl.BlockSpec((1, MOD_TN), lambda j: (0, j)),
        ],
        out_specs=pl.BlockSpec((BATCH, MOD_TN), lambda j: (0, j)),
        out_shape=jax.ShapeDtypeStruct((BATCH, n), F32),
        compiler_params=pltpu.CompilerParams(
            dimension_semantics=("arbitrary",), vmem_limit_bytes=_vmem(40)),
        name="adaln_mod",
    )(c, w_ada, b_ada)


def _ffn_kernel(x_ref, sh_ref, sc_ref, gt_ref, pre_ref, post_ref, wg_ref, wu_ref, wd_ref,
                o_ref, u_scr, acc_scr):
    f = pl.program_id(2)

    @pl.when(f == 0)
    def _():
        u = _rms(x_ref[...], pre_ref[...]) * (1.0 + sc_ref[...]) + sh_ref[...]
        u_scr[...] = u.astype(BF16)
        acc_scr[...] = jnp.zeros_like(acc_scr)

    u = u_scr[...]
    gate = _bdot(u, wg_ref[...])
    up = _bdot(u, wu_ref[...])
    h = (gate * jax.nn.sigmoid(gate)) * up
    acc_scr[...] += _bdot(h.astype(BF16), wd_ref[...])

    @pl.when(f == pl.num_programs(2) - 1)
    def _():
        y = _rms(acc_scr[...], post_ref[...])
        o_ref[...] = x_ref[...] + 0.5 * gt_ref[...] * y


def _ffn(x, mod3, k_mod, pre_g, post_g, wg, wu, wd, *, x_time_major, out_time_major):
    ts, tf = FFN_TS, FFN_TF
    if x_time_major:
        x_spec = pl.BlockSpec((ts, D_MODEL), lambda i, b, f: (i, b))
    else:
        x_spec = pl.BlockSpec((None, ts, D_MODEL), lambda i, b, f: (b, i, 0))
    if out_time_major:
        o_spec = pl.BlockSpec((ts, D_MODEL), lambda i, b, f: (i, b))
        o_shape = jax.ShapeDtypeStruct((SEQ, BATCH * D_MODEL), F32)
    else:
        o_spec = pl.BlockSpec((None, ts, D_MODEL), lambda i, b, f: (b, i, 0))
        o_shape = jax.ShapeDtypeStruct((BATCH, SEQ, D_MODEL), F32)

    def mod_spec(k):
        return pl.BlockSpec((None, 1, D_MODEL), lambda i, b, f: (b * N_ADA + k, 0, 0))

    vec_spec = pl.BlockSpec((1, D_MODEL), lambda i, b, f: (0, 0))
    return pl.pallas_call(
        _ffn_kernel,
        grid=(SEQ // ts, BATCH, D_FF // tf),
        in_specs=[
            x_spec, mod_spec(k_mod), mod_spec(k_mod + 1), mod_spec(k_mod + 2), vec_spec, vec_spec,
            pl.BlockSpec((D_MODEL, tf), lambda i, b, f: (0, f)),
            pl.BlockSpec((D_MODEL, tf), lambda i, b, f: (0, f)),
            pl.BlockSpec((tf, D_MODEL), lambda i, b, f: (f, 0)),
        ],
        out_specs=o_spec,
        out_shape=o_shape,
        scratch_shapes=[pltpu.VMEM((ts, D_MODEL), BF16), pltpu.VMEM((ts, D_MODEL), F32)],
        compiler_params=pltpu.CompilerParams(
            dimension_semantics=("parallel", "parallel", "arbitrary"), vmem_limit_bytes=_vmem(56)),
        name="macaron_ffn",
    )(x, mod3, mod3, mod3, pre_g, post_g, wg, wu, wd)


def _rope_slice(x, c, s):
    lane = lax.broadcasted_iota(jnp.int32, x.shape, 1)
    half = ROPE_DIM // 2
    partner = jnp.where(lane < half, pltpu.roll(x, LANES - half, axis=1), pltpu.roll(x, half, axis=1))
    return x * c + partner * s


def _proj_kernel(x_ref, sh_ref, sc_ref, pre_ref, w_ref, cos_ref, sin_ref, o_ref, u_scr):
    n = pl.program_id(2)

    @pl.when(n == 0)
    def _():
        u = _rms(x_ref[...], pre_ref[...]) * (1.0 + sc_ref[...]) + sh_ref[...]
        u_scr[...] = u.astype(BF16)

    r = _bdot(u_scr[...], w_ref[...])
    q_lo, q_hi = COL_Q // PROJ_TN, COL_KV // PROJ_TN
    k_tiles = [(COL_KV + 2 * j * KV_WIDTH) // PROJ_TN for j in range(3)]
    is_q = jnp.logical_and(n >= q_lo, n < q_hi)
    is_rope = is_q
    for kt in k_tiles:
        is_rope = jnp.logical_or(is_rope, n == kt)

    @pl.when(is_rope)
    def _():
        scale = jnp.where(is_q, HEAD_DIM ** -0.5, 1.0).astype(F32)
        c = cos_ref[...]
        s = sin_ref[...]
        for hh in range(PROJ_TN // HEAD_DIM):
            sl = slice(hh * HEAD_DIM, (hh + 1) * HEAD_DIM)
            o_ref[:, sl] = (_rope_slice(r[:, sl], c, s) * scale).astype(BF16)

    @pl.when(jnp.logical_not(is_rope))
    def _():
        o_ref[...] = r.astype(BF16)


def _projection(x1, mod3, pre_g, w_all, cos_t, sin_t):
    ts, tn = PROJ_TS, PROJ_TN
    nn = PROJ_W // tn

    def mod_spec(k):
        return pl.BlockSpec((None, 1, D_MODEL), lambda i, b, n: (b * N_ADA + k, 0, 0))

    return pl.pallas_call(
        _proj_kernel,
        grid=(SEQ // ts, BATCH, nn),
        in_specs=[
            pl.BlockSpec((ts, D_MODEL), lambda i, b, n: (i, b)),
            mod_spec(3), mod_spec(4),
            pl.BlockSpec((1, D_MODEL), lambda i, b, n: (0, 0)),
            pl.BlockSpec((D_MODEL, tn), lambda i, b, n: (0, n)),
            pl.BlockSpec((ts, LANES), lambda i, b, n: (i, 0)),
            pl.BlockSpec((ts, LANES), lambda i, b, n: (i, 0)),
        ],
        out_specs=pl.BlockSpec((ts, tn), lambda i, b, n: (i, b * nn + n)),
        out_shape=jax.ShapeDtypeStruct((SEQ, BATCH * PROJ_W), BF16),
        scratch_shapes=[pltpu.VMEM((ts, D_MODEL), BF16)],
        compiler_params=pltpu.CompilerParams(
            dimension_semantics=("parallel", "parallel", "arbitrary"), vmem_limit_bytes=_vmem(48)),
        name="mix_in_proj",
    )(x1, mod3, mod3, pre_g, w_all, cos_t, sin_t)


def _lru_kernel(xa_ref, ya_ref, cw_ref, cb_ref, wr_ref, br_ref, wi_ref, bi_ref, lam_ref,
                o_ref, xe_scr, a_scr, b_scr, h_scr):
    tc = pl.program_id(1)
    rows = LRU_TT * SUBLANES
    pad = (CONV_WIDTH - 1) * SUBLANES

    @pl.when(tc == 0)
    def _():
        xe_scr[0:pad, :] = jnp.zeros((pad, LRU_CW), F32)
        h_scr[...] = jnp.zeros_like(h_scr)

    xe_scr[pad:pad + rows, :] = xa_ref[...].astype(F32)
    cw = cw_ref[...]
    xc = cb_ref[...] + xe_scr[0:rows, :] * cw[0:1, :]
    for w in range(1, CONV_WIDTH):
        xc = xc + xe_scr[w * SUBLANES:w * SUBLANES + rows, :] * cw[w:w + 1, :]
    xe_scr[0:pad, :] = xe_scr[rows:rows + pad, :]

    nlam = -lam_ref[...]
    softplus = jnp.maximum(nlam, 0.0) + jnp.log1p(jnp.exp(-jnp.abs(nlam)))
    for k in range(LRU_CW // LRU_BLOCK_DIM):
        sl = slice(k * LRU_BLOCK_DIM, (k + 1) * LRU_BLOCK_DIM)
        xck = xc[:, sl]
        xb = xck.astype(BF16)
        r = jax.nn.sigmoid(_bdot(xb, wr_ref[k]) + br_ref[:, sl])
        ig = jax.nn.sigmoid(_bdot(xb, wi_ref[k]) + bi_ref[:, sl])
        log_a = (-LRU_C * r) * softplus[:, sl]
        em1 = jnp.tanh(log_a) * (jnp.exp(2.0 * log_a) + 1.0)
        a_scr[:, sl] = jnp.exp(log_a)
        b_scr[:, sl] = jnp.sqrt(-em1) * (ig * xck)

    def step(t, h):
        r0 = pl.multiple_of(t * SUBLANES, SUBLANES)
        h = a_scr[pl.ds(r0, SUBLANES), :] * h + b_scr[pl.ds(r0, SUBLANES), :]
        b_scr[pl.ds(r0, SUBLANES), :] = h
        return h

    h_scr[...] = lax.fori_loop(0, LRU_TT, step, h_scr[...], unroll=8)
    o_ref[...] = (b_scr[...] * _gelu_tanh(ya_ref[...].astype(F32))).astype(BF16)


def _rglru(proj_rows, conv_w, conv_b, wr, br, wi, bi, lam):
    rows = LRU_TT * SUBLANES
    ncb = D_RNN // LRU_CW
    kb = LRU_CW // LRU_BLOCK_DIM
    vec = pl.BlockSpec((1, LRU_CW), lambda cb, tc: (0, cb))
    return pl.pallas_call(
        _lru_kernel,
        grid=(ncb, SEQ // LRU_TT),
        in_specs=[
            pl.BlockSpec((rows, LRU_CW), lambda cb, tc: (tc, COL_XA // LRU_CW + cb)),
            pl.BlockSpec((rows, LRU_CW), lambda cb, tc: (tc, COL_YA // LRU_CW + cb)),
            pl.BlockSpec((CONV_WIDTH, LRU_CW), lambda cb, tc: (0, cb)),
            vec,
            pl.BlockSpec((kb, LRU_BLOCK_DIM, LRU_BLOCK_DIM), lambda cb, tc: (cb, 0, 0)),
            vec,
            pl.BlockSpec((kb, LRU_BLOCK_DIM, LRU_BLOCK_DIM), lambda cb, tc: (cb, 0, 0)),
            vec,
            vec,
        ],
        out_specs=pl.BlockSpec((rows, LRU_CW), lambda cb, tc: (tc, cb)),
        out_shape=jax.ShapeDtypeStruct((SEQ * BATCH, D_RNN), BF16),
        scratch_shapes=[
            pltpu.VMEM((rows + (CONV_WIDTH - 1) * SUBLANES, LRU_CW), F32),
            pltpu.VMEM((rows, LRU_CW), F32),
            pltpu.VMEM((rows, LRU_CW), F32),
            pltpu.VMEM((SUBLANES, LRU_CW), F32),
        ],
        compiler_params=pltpu.CompilerParams(
            dimension_semantics=("parallel", "arbitrary"), vmem_limit_bytes=_vmem(48)),
        name="rglru",
    )(proj_rows, proj_rows, conv_w, conv_b, wr, br, wi, bi, lam)


def _cmp_kernel(x_ref, pe_ref, w1_ref, w2_ref, o_ref):
    half = CMP_STRIDE * HEAD_DIM
    x = x_ref[...].astype(F32)
    pe = pe_ref[...]
    first = _bdot((x + pe[:, :half]).astype(BF16), w1_ref[0:half, :])
    second = _bdot((x + pe[:, half:]).astype(BF16), w1_ref[half:2 * half, :])
    pre = first + pltpu.roll(second, N_CMP_PAD - 1, axis=0)
    o_ref[...] = _bdot(_gelu_tanh(pre).astype(BF16), w2_ref[...]).astype(BF16)


def _compress(chunks, pe, w1, w2):
    feat = CMP_STRIDE * HEAD_DIM
    return pl.pallas_call(
        _cmp_kernel,
        grid=(2, BATCH, N_KV_GROUPS),
        in_specs=[
            pl.BlockSpec((None, None, None, N_CMP_PAD, feat), lambda t, b, g: (t, b, g, 0, 0)),
            pl.BlockSpec((None, 1, 2 * feat), lambda t, b, g: (t, 0, 0)),
            pl.BlockSpec((None, 2 * feat, HEAD_DIM), lambda t, b, g: (t, 0, 0)),
            pl.BlockSpec((None, HEAD_DIM, HEAD_DIM), lambda t, b, g: (t, 0, 0)),
        ],
        out_specs=pl.BlockSpec((None, None, None, N_CMP_PAD, HEAD_DIM), lambda t, b, g: (t, b, g, 0, 0)),
        out_shape=jax.ShapeDtypeStruct((2, BATCH, N_KV_GROUPS, N_CMP_PAD, HEAD_DIM), BF16),
        compiler_params=pltpu.CompilerParams(
            dimension_semantics=("parallel", "parallel", "parallel"), vmem_limit_bytes=_vmem(32)),
        name="kv_compress",
    )(chunks, pe, w1, w2)


def _attn_kernel(q_ref, ks_ref, vs_ref, kw_ref, vw_ref, kc_ref, vc_ref, gt_ref, o_ref,
                 acc_scr, m_scr, l_scr, ocmp_scr):
    t = ATT_T
    hpg = HEADS_PER_GROUP
    i = pl.program_id(2)
    q0 = i * t

    row = lax.broadcasted_iota(jnp.int32, (t, LANES), 0)
    lane = lax.broadcasted_iota(jnp.int32, (t, LANES), 1)
    cmp_ok = (lane * CMP_STRIDE + (CMP_BLOCK - 1)) <= (q0 + row)
    kc = kc_ref[...]
    vc = vc_ref[...]
    p_sum = jnp.zeros((t, LANES), F32)
    for hh in range(hpg):
        qh = q_ref[:, hh * HEAD_DIM:(hh + 1) * HEAD_DIM]
        s = jnp.where(cmp_ok, _bdot_nt(qh, kc), MASK_NEG)
        e = jnp.where(cmp_ok, jnp.exp(s - jnp.max(s, axis=1, keepdims=True)), 0.0)
        den = jnp.sum(e, axis=1, keepdims=True)
        p = e / jnp.where(den > 0.0, den, 1.0)
        ocmp_scr[hh] = _bdot(p.astype(BF16), vc)
        p_sum = p_sum + p

    jj = lax.broadcasted_iota(jnp.int32, (LANES, LANES), 0)
    nn = lax.broadcasted_iota(jnp.int32, (LANES, LANES), 1)
    overlap = jnp.logical_and(
        jnp.logical_and(nn * CMP_STRIDE < (jj + 1) * SLC_BLOCK, nn * CMP_STRIDE + CMP_BLOCK > jj * SLC_BLOCK),
        jj < N_SLC)
    overlap_t = jnp.where(overlap, 1.0, 0.0).astype(BF16)
    p_hi = p_sum.astype(BF16)
    p_lo = (p_sum - p_hi.astype(F32)).astype(BF16)
    imp_t = (_bdot_nt(overlap_t, p_hi) + _bdot_nt(overlap_t, p_lo))[0:N_SLC, :]

    blk = lax.broadcasted_iota(jnp.int32, (N_SLC, t), 0)
    pos = q0 + lax.broadcasted_iota(jnp.int32, (N_SLC, t), 1)
    cur = jnp.right_shift(pos, 6)
    forced = jnp.logical_or(blk == 0, jnp.logical_or(blk == cur, blk == cur - 1))
    val = jnp.where(forced, jnp.inf, jnp.where(blk * SLC_BLOCK <= pos, imp_t, -jnp.inf))
    rank = jnp.zeros((N_SLC, t), F32)
    for c in range(N_SLC):
        vc_row = val[c:c + 1, :]
        ahead = jnp.logical_or(vc_row > val, jnp.logical_and(vc_row == val, blk > c))
        rank = rank + jnp.where(ahead, 1.0, 0.0)
    notsel_t = jnp.where(rank < float(N_SELECT), 0.0, MASK_NEG)
    notsel_t = jnp.concatenate([notsel_t, jnp.zeros((LANES - N_SLC, t), F32)], axis=0)
    notsel = notsel_t.T.astype(BF16)

    m_scr[...] = jnp.full(m_scr.shape, M_INIT, F32)
    l_scr[...] = jnp.zeros_like(l_scr)
    acc_scr[...] = jnp.zeros_like(acc_scr)

    r2 = lax.broadcasted_iota(jnp.int32, (t, t), 0)
    c2 = lax.broadcasted_iota(jnp.int32, (t, t), 1)
    causal_bias = jnp.where(c2 <= r2, 0.0, MASK_NEG)
    band_bias = jnp.where(c2 > r2, 0.0, MASK_NEG)
    sel_j = lax.broadcasted_iota(jnp.int32, (LANES, t), 0)
    sel_k = jnp.right_shift(lax.broadcasted_iota(jnp.int32, (LANES, t), 1), 6)

    def tile_update(slot0, k_ref, v_ref, kt, bias):
        k0 = pl.multiple_of(kt * t, t)
        kt_ = k_ref[pl.ds(k0, t), :]
        vt_ = v_ref[pl.ds(k0, t), :]
        for hh in range(hpg):
            slot = slot0 + hh
            s = _bdot_nt(q_ref[:, hh * HEAD_DIM:(hh + 1) * HEAD_DIM], kt_)
            if bias is not None:
                s = s + bias
            m_prev = m_scr[slot]
            m_new = jnp.maximum(m_prev, jnp.max(s, axis=1, keepdims=True))
            alpha = jnp.exp(m_prev - m_new)
            p = jnp.exp(s - m_new)
            l_scr[slot] = alpha * l_scr[slot] + jnp.sum(p, axis=1, keepdims=True)
            acc_scr[slot] = alpha * acc_scr[slot] + _bdot(p.astype(BF16), vt_)
            m_scr[slot] = m_new

    def sel_bias(kt):
        expand = jnp.where(sel_j - kt * (t // SLC_BLOCK) == sel_k, 1.0, 0.0).astype(BF16)
        return _bdot(notsel, expand)

    def slc_body(kt, carry):
        tile_update(0, ks_ref, vs_ref, kt, sel_bias(kt))
        return carry

    lax.fori_loop(0, i, slc_body, 0)
    tile_update(0, ks_ref, vs_ref, i, sel_bias(i) + causal_bias)

    @pl.when(i >= 2)
    def _():
        tile_update(hpg, kw_ref, vw_ref, i - 2, band_bias)

    @pl.when(i >= 1)
    def _():
        tile_update(hpg, kw_ref, vw_ref, i - 1, None)

    tile_update(hpg, kw_ref, vw_ref, i, causal_bias)

    gates = jax.nn.sigmoid(gt_ref[...].astype(F32))
    for hh in range(hpg):
        o_slc = acc_scr[hh] / l_scr[hh]
        o_win = acc_scr[hpg + hh] / l_scr[hpg + hh]
        o = (gates[:, 3 * hh:3 * hh + 1] * ocmp_scr[hh]
             + gates[:, 3 * hh + 1:3 * hh + 2] * o_slc
             + gates[:, 3 * hh + 2:3 * hh + 3] * o_win)
        o_ref[:, hh * HEAD_DIM:(hh + 1) * HEAD_DIM] = o.astype(BF16)


def _attention(proj2d, kcvc):
    t = ATT_T
    g_w = HEADS_PER_GROUP * HEAD_DIM
    pw_g = PROJ_W // g_w
    pw_h = PROJ_W // HEAD_DIM

    def kv_spec(j):
        base = (COL_KV + j * KV_WIDTH) // HEAD_DIM
        return pl.BlockSpec((SEQ, HEAD_DIM), lambda b, g, i: (0, b * pw_h + base + g))

    def cmp_spec(tsel):
        return pl.BlockSpec((None, None, None, N_CMP_PAD, HEAD_DIM), lambda b, g, i: (tsel, b, g, 0, 0))

    return pl.pallas_call(
        _attn_kernel,
        grid=(BATCH, N_KV_GROUPS, SEQ // t),
        in_specs=[
            pl.BlockSpec((t, g_w), lambda b, g, i: (i, b * pw_g + COL_Q // g_w + g)),
            kv_spec(2), kv_spec(3), kv_spec(4), kv_spec(5),
            cmp_spec(0), cmp_spec(1),
            pl.BlockSpec((t, LANES), lambda b, g, i: (i, b * pw_h + COL_NG // LANES + g)),
        ],
        out_specs=pl.BlockSpec((t, g_w), lambda b, g, i: (i, b * N_KV_GROUPS + g)),
        out_shape=jax.ShapeDtypeStruct((SEQ, BATCH * Q_WIDTH), BF16),
        scratch_shapes=[
            pltpu.VMEM((2 * HEADS_PER_GROUP, t, HEAD_DIM), F32),
            pltpu.VMEM((2 * HEADS_PER_GROUP, t, 1), F32),
            pltpu.VMEM((2 * HEADS_PER_GROUP, t, 1), F32),
            pltpu.VMEM((HEADS_PER_GROUP, t, HEAD_DIM), F32),
        ],
        compiler_params=pltpu.CompilerParams(
            dimension_semantics=("parallel", "parallel", "arbitrary"), vmem_limit_bytes=_vmem(40)),
        name="nsa_attention",
    )(proj2d, proj2d, proj2d, proj2d, proj2d, kcvc, kcvc, proj2d)


def _merge_kernel(ha_ref, ob_ref, wa_ref, wb_ref, ga_ref, gb_ref, o_ref):
    ya = _bdot(ha_ref[...], wa_ref[...])
    yb = _bdot(ob_ref[...], wb_ref[...])
    ga = jax.nn.sigmoid(ga_ref[...].astype(F32))
    gb = jax.nn.sigmoid(gb_ref[...].astype(F32))
    o_ref[...] = (ga * ya + gb * yb).astype(BF16)


def _merge(hg2d, ob2d, proj2d, wa, wb):
    ts, tn = OUT_TS, PROJ_TN
    nn = D_MODEL // tn
    pw = PROJ_W // tn
    return pl.pallas_call(
        _merge_kernel,
        grid=(SEQ // ts, BATCH, nn),
        in_specs=[
            pl.BlockSpec((ts, D_RNN), lambda i, b, n: (i, b)),
            pl.BlockSpec((ts, Q_WIDTH), lambda i, b, n: (i, b)),
            pl.BlockSpec((D_RNN, tn), lambda i, b, n: (0, n)),
            pl.BlockSpec((Q_WIDTH, tn), lambda i, b, n: (0, n)),
            pl.BlockSpec((ts, tn), lambda i, b, n: (i, b * pw + COL_MG // tn + n)),
            pl.BlockSpec((ts, tn), lambda i, b, n: (i, b * pw + (COL_MG + D_MODEL) // tn + n)),
        ],
        out_specs=pl.BlockSpec((ts, tn), lambda i, b, n: (i, b * nn + n)),
        out_shape=jax.ShapeDtypeStruct((SEQ, BATCH * D_MODEL), BF16),
        compiler_params=pltpu.CompilerParams(
            dimension_semantics=("parallel", "parallel", "arbitrary"), vmem_limit_bytes=_vmem(40)),
        name="branch_merge",
    )(hg2d, ob2d, wa, wb, proj2d, proj2d)


def _mixout_kernel(y_ref, w_ref, x_ref, gt_ref, post_ref, o_ref):
    mixed = _bdot(y_ref[...], w_ref[...])
    o_ref[...] = x_ref[...] + gt_ref[...] * _rms(mixed, post_ref[...])


def _mix_out(ymix, w_out, x1, mod3, post_g):
    ts = OUT_TS
    return pl.pallas_call(
        _mixout_kernel,
        grid=(SEQ // ts, BATCH),
        in_specs=[
            pl.BlockSpec((ts, D_MODEL), lambda i, b: (i, b)),
            pl.BlockSpec((D_MODEL, D_MODEL), lambda i, b: (0, 0)),
            pl.BlockSpec((ts, D_MODEL), lambda i, b: (i, b)),
            pl.BlockSpec((None, 1, D_MODEL), lambda i, b: (b * N_ADA + 5, 0, 0)),
            pl.BlockSpec((1, D_MODEL), lambda i, b: (0, 0)),
        ],
        out_specs=pl.BlockSpec((ts, D_MODEL), lambda i, b: (i, b)),
        out_shape=jax.ShapeDtypeStruct((SEQ, BATCH * D_MODEL), F32),
        compiler_params=pltpu.CompilerParams(
            dimension_semantics=("parallel", "parallel"), vmem_limit_bytes=_vmem(56)),
        name="mix_out",
    )(ymix, w_out, x1, mod3, post_g)


def _rope_tables():
    pos = jnp.arange(SEQ).astype(F32)
    inv_freq = ROPE_THETA ** (-jnp.arange(0, ROPE_DIM, 2, dtype=F32) / ROPE_DIM)
    ang = pos[:, None] * inv_freq[None, :]
    cos, sin = jnp.cos(ang), jnp.sin(ang)
    rest = HEAD_DIM - ROPE_DIM
    cos_t = jnp.concatenate([cos, cos, jnp.ones((SEQ, rest), F32)], axis=1)
    sin_t = jnp.concatenate([-sin, sin, jnp.zeros((SEQ, rest), F32)], axis=1)
    return cos_t, sin_t


def _pack_w_in(w_in):
    n_main = COL_MG
    gates = w_in[:, n_main:n_main + 3 * N_HEADS].reshape(D_MODEL, N_KV_GROUPS, 3 * HEADS_PER_GROUP)
    gates = jnp.pad(gates, ((0, 0), (0, 0), (0, LANES - 3 * HEADS_PER_GROUP))).reshape(D_MODEL, NG_PAD)
    return jnp.concatenate([w_in[:, :n_main], w_in[:, n_main + 3 * N_HEADS:], gates], axis=1).astype(BF16)


def _cmp_chunks(proj3d, col):
    x = proj3d[:, :, col:col + KV_WIDTH].reshape(N_CMP_PAD, CMP_STRIDE, BATCH, N_KV_GROUPS, HEAD_DIM)
    return x.transpose(2, 3, 0, 1, 4).reshape(BATCH, N_KV_GROUPS, N_CMP_PAD, CMP_STRIDE * HEAD_DIM)


def kernel(x, c, w_ada, b_ada, ffn1_pre_g, ffn1_post_g, ffn1_w_gate, ffn1_w_up, ffn1_w_down, mix_pre_g, mix_post_g, w_in, conv_w, conv_b, lru_wr, lru_br, lru_wi, lru_bi, lru_lambda, cmp_pe_k, cmp_w1_k, cmp_w2_k, cmp_pe_v, cmp_w1_v, cmp_w2_v, w_a_out, w_b_out, w_out, ffn2_pre_g, ffn2_post_g, ffn2_w_gate, ffn2_w_up, ffn2_w_down):
    assert x.shape == (BATCH, SEQ, D_MODEL) and w_ada.shape[0] == 1
    mod = _modulation(c, w_ada[0], b_ada)
    mod3 = mod.reshape(BATCH * N_ADA, 1, D_MODEL)

    x1 = _ffn(x, mod3, 0, ffn1_pre_g, ffn1_post_g, ffn1_w_gate[0].astype(BF16),
              ffn1_w_up[0].astype(BF16), ffn1_w_down[0].astype(BF16),
              x_time_major=False, out_time_major=True)

    cos_t, sin_t = _rope_tables()
    proj2d = _projection(x1, mod3, mix_pre_g, _pack_w_in(w_in[0]), cos_t, sin_t)

    hg = _rglru(proj2d.reshape(SEQ * BATCH, PROJ_W), conv_w[0], conv_b,
                lru_wr[0].astype(BF16), lru_br, lru_wi[0].astype(BF16), lru_bi, lru_lambda)

    proj3d = proj2d.reshape(SEQ, BATCH, PROJ_W)
    chunks = jnp.stack([_cmp_chunks(proj3d, COL_KV), _cmp_chunks(proj3d, COL_KV + KV_WIDTH)])
    pe = jnp.stack([cmp_pe_k[0], cmp_pe_v[0]]).reshape(2, 1, CMP_BLOCK * HEAD_DIM)
    w1 = jnp.stack([cmp_w1_k[0], cmp_w1_v[0]]).astype(BF16)
    w2 = jnp.stack([cmp_w2_k[0], cmp_w2_v[0]]).astype(BF16)
    kcvc = _compress(chunks, pe, w1, w2)

    ob = _attention(proj2d, kcvc)
    ymix = _merge(hg.reshape(SEQ, BATCH * D_RNN), ob, proj2d,
                  w_a_out[0].astype(BF16), w_b_out[0].astype(BF16))
    x2 = _mix_out(ymix, w_out[0].astype(BF16), x1, mod3, mix_post_g)

    return _ffn(x2, mod3, 6, ffn2_pre_g, ffn2_post_g, ffn2_w_gate[0].astype(BF16),
                ffn2_w_up[0].astype(BF16), ffn2_w_down[0].astype(BF16),
                x_time_major=True, out_time_major=False)
```

```python
import math

import jax
import jax.numpy as jnp
from jax import lax
from jax.experimental import pallas as pl
from jax.experimental.pallas import tpu as pltpu

F32 = jnp.float32
BF16 = jnp.bfloat16

D_MODEL = 2048
BATCH = 8
SEQ = 2048
D_RNN = D_MODEL
LRU_BLOCKS = 16
LRU_BLOCK_DIM = D_RNN // LRU_BLOCKS
CONV_WIDTH = 4
LRU_C = 8.0
N_HEADS = 16
HEAD_DIM = 128
N_KV_GROUPS = 4
HEADS_PER_GROUP = N_HEADS // N_KV_GROUPS
Q_WIDTH = N_HEADS * HEAD_DIM
KV_WIDTH = N_KV_GROUPS * HEAD_DIM
CMP_STRIDE = 16
CMP_BLOCK = 2 * CMP_STRIDE
SLC_BLOCK = 64
N_SELECT = 16
WINDOW = 512
ROPE_THETA = 500000.0
ROPE_DIM = HEAD_DIM // 4
D_FF = 5632
NORM_EPS = 1e-6
N_ADA = 9
N_SLC = SEQ // SLC_BLOCK
N_CMP_PAD = SEQ // CMP_STRIDE

LANES = 128
SUBLANES = 8

COL_XA = 0
COL_YA = COL_XA + D_RNN
COL_Q = COL_YA + D_RNN
COL_KV = COL_Q + Q_WIDTH
COL_MG = COL_KV + 6 * KV_WIDTH
COL_NG = COL_MG + 2 * D_MODEL
PROJ_TN = 512
NG_PAD = PROJ_TN
PROJ_W = COL_NG + NG_PAD

MASK_NEG = -1e30
M_INIT = -1e29

FFN_TS = 512
FFN_TF = 512
PROJ_TS = 1024
OUT_TS = 512
ATT_T = 256
LRU_TS = 512
LRU_CW = 512
MOD_TN = 1024


def _vmem(mb):
    return mb * 1024 * 1024


def _rms(x, g):
    return x * lax.rsqrt(jnp.mean(x * x, axis=-1, keepdims=True) + NORM_EPS) * g


def _gelu_tanh(x):
    c = math.sqrt(2.0 / math.pi)
    return x * (0.5 * (1.0 + jnp.tanh(c * (x + 0.044715 * (x * x * x)))))


def _bdot(a, b):
    return jnp.dot(a, b, preferred_element_type=F32)


def _bdot_nt(a, b):
    return lax.dot_general(a, b, (((1,), (1,)), ((), ())), preferred_element_type=F32)


def _mod_kernel(c_ref, w_ref, b_ref, o_ref):
    c = c_ref[...]
    ca = c * jax.nn.sigmoid(c)
    o_ref[...] = _bdot(ca.astype(BF16), w_ref[...].astype(BF16)) + b_ref[...]


def _modulation(c, w_ada, b_ada):
    n = N_ADA * D_MODEL
    return pl.pallas_call(
        _mod_kernel,
        grid=(n // MOD_TN,),
        in_specs=[
            pl.BlockSpec((BATCH, D_MODEL), lambda j: (0, 0)),
            pl.BlockSpec((D_MODEL, MOD_TN), lambda j: (0, j)),
            pl.BlockSpec((1, MOD_TN), lambda j: (0, j)),
        ],
        out_specs=pl.BlockSpec((BATCH, MOD_TN), lambda j: (0, j)),
        out_shape=jax.ShapeDtypeStruct((BATCH, n), F32),
        compiler_params=pltpu.CompilerParams(
            dimension_semantics=("arbitrary",), vmem_limit_bytes=_vmem(40)),
        name="adaln_mod",
    )(c, w_ada, b_ada)


def _ffn_kernel(x_ref, sh_ref, sc_ref, gt_ref, pre_ref, post_ref, wg_ref, wu_ref, wd_ref,
                o_ref, u_scr, acc_scr):
    f = pl.program_id(2)

    @pl.when(f == 0)
    def _():
        u = _rms(x_ref[...], pre_ref[...]) * (1.0 + sc_ref[...]) + sh_ref[...]
        u_scr[...] = u.astype(BF16)
        acc_scr[...] = jnp.zeros_like(acc_scr)

    u = u_scr[...]
    gate = _bdot(u, wg_ref[...])
    up = _bdot(u, wu_ref[...])
    h = (gate * jax.nn.sigmoid(gate)) * up
    acc_scr[...] += _bdot(h.astype(BF16), wd_ref[...])

    @pl.when(f == pl.num_programs(2) - 1)
    def _():
        y = _rms(acc_scr[...], post_ref[...])
        o_ref[...] = x_ref[...] + 0.5 * gt_ref[...] * y


def _ffn(x, mod3, k_mod, pre_g, post_g, wg, wu, wd, *, x_time_major, out_time_major):
    ts, tf = FFN_TS, FFN_TF
    if x_time_major:
        x_spec = pl.BlockSpec((ts, D_MODEL), lambda i, b, f: (i, b))
    else:
        x_spec = pl.BlockSpec((None, ts, D_MODEL), lambda i, b, f: (b, i, 0))
    if out_time_major:
        o_spec = pl.BlockSpec((ts, D_MODEL), lambda i, b, f: (i, b))
        o_shape = jax.ShapeDtypeStruct((SEQ, BATCH * D_MODEL), F32)
    else:
        o_spec = pl.BlockSpec((None, ts, D_MODEL), lambda i, b, f: (b, i, 0))
        o_shape = jax.ShapeDtypeStruct((BATCH, SEQ, D_MODEL), F32)

    def mod_spec(k):
        return pl.BlockSpec((None, 1, D_MODEL), lambda i, b, f: (b * N_ADA + k, 0, 0))

    vec_spec = pl.BlockSpec((1, D_MODEL), lambda i, b, f: (0, 0))
    return pl.pallas_call(
        _ffn_kernel,
        grid=(SEQ // ts, BATCH, D_FF // tf),
        in_specs=[
            x_spec, mod_spec(k_mod), mod_spec(k_mod + 1), mod_spec(k_mod + 2), vec_spec, vec_spec,
            pl.BlockSpec((D_MODEL, tf), lambda i, b, f: (0, f)),
            pl.BlockSpec((D_MODEL, tf), lambda i, b, f: (0, f)),
            pl.BlockSpec((tf, D_MODEL), lambda i, b, f: (f, 0)),
        ],
        out_specs=o_spec,
        out_shape=o_shape,
        scratch_shapes=[pltpu.VMEM((ts, D_MODEL), BF16), pltpu.VMEM((ts, D_MODEL), F32)],
        compiler_params=pltpu.CompilerParams(
            dimension_semantics=("parallel", "parallel", "arbitrary"), vmem_limit_bytes=_vmem(56)),
        name="macaron_ffn",
    )(x, mod3, mod3, mod3, pre_g, post_g, wg, wu, wd)


def _rope_slice(x, c, s):
    lane = lax.broadcasted_iota(jnp.int32, x.shape, 1)
    half = ROPE_DIM // 2
    partner = jnp.where(lane < half, pltpu.roll(x, LANES - half, axis=1), pltpu.roll(x, half, axis=1))
    return x * c + partner * s


def _proj_kernel(x_ref, sh_ref, sc_ref, pre_ref, w_ref, cos_ref, sin_ref, o_ref, u_scr):
    n = pl.program_id(2)

    @pl.when(n == 0)
    def _():
        u = _rms(x_ref[...], pre_ref[...]) * (1.0 + sc_ref[...]) + sh_ref[...]
        u_scr[...] = u.astype(BF16)

    r = _bdot(u_scr[...], w_ref[...])
    q_lo, q_hi = COL_Q // PROJ_TN, COL_KV // PROJ_TN
    k_tiles = [(COL_KV + 2 * j * KV_WIDTH) // PROJ_TN for j in range(3)]
    is_q = jnp.logical_and(n >= q_lo, n < q_hi)
    is_rope = is_q
    for kt in k_tiles:
        is_rope = jnp.logical_or(is_rope, n == kt)

    @pl.when(is_rope)
    def _():
        scale = jnp.where(is_q, HEAD_DIM ** -0.5, 1.0).astype(F32)
        c = cos_ref[...]
        s = sin_ref[...]
        for hh in range(PROJ_TN // HEAD_DIM):
            sl = slice(hh * HEAD_DIM, (hh + 1) * HEAD_DIM)
            o_ref[:, sl] = (_rope_slice(r[:, sl], c, s) * scale).astype(BF16)

    @pl.when(jnp.logical_not(is_rope))
    def _():
        o_ref[...] = r.astype(BF16)


def _projection(x1, mod3, pre_g, w_all, cos_t, sin_t):
    ts, tn = PROJ_TS, PROJ_TN
    nn = PROJ_W // tn

    def mod_spec(k):
        return pl.BlockSpec((None, 1, D_MODEL), lambda i, b, n: (b * N_ADA + k, 0, 0))

    return pl.pallas_call(
        _proj_kernel,
        grid=(SEQ // ts, BATCH, nn),
        in_specs=[
            pl.BlockSpec((ts, D_MODEL), lambda i, b, n: (i, b)),
            mod_spec(3), mod_spec(4),
            pl.BlockSpec((1, D_MODEL), lambda i, b, n: (0, 0)),
            pl.BlockSpec((D_MODEL, tn), lambda i, b, n: (0, n)),
            pl.BlockSpec((ts, LANES), lambda i, b, n: (i, 0)),
            pl.BlockSpec((ts, LANES), lambda i, b, n: (i, 0)),
        ],
        out_specs=pl.BlockSpec((ts, tn), lambda i, b, n: (i, b * nn + n)),
        out_shape=jax.ShapeDtypeStruct((SEQ, BATCH * PROJ_W), BF16),
        scratch_shapes=[pltpu.VMEM((ts, D_MODEL), BF16)],
        compiler_params=pltpu.CompilerParams(
            dimension_semantics=("parallel", "parallel", "arbitrary"), vmem_limit_bytes=_vmem(48)),
        name="mix_in_proj",
    )(x1, mod3, mod3, pre_g, w_all, cos_t, sin_t)


def _lru_kernel(xa_ref, ya_ref, cw_ref, cb_ref, wr_ref, br_ref, wi_ref, bi_ref, lam_ref,
                o_ref, xe_scr, a_scr, b_scr, h_scr):
    tc = pl.program_id(2)
    ts = LRU_TS
    groups = ts // SUBLANES

    @pl.when(tc == 0)
    def _():
        xe_scr[0:SUBLANES, :] = jnp.zeros((SUBLANES, LRU_CW), F32)
        h_scr[...] = jnp.zeros_like(h_scr)

    xe_scr[SUBLANES:SUBLANES + ts, :] = xa_ref[...].astype(F32)
    cw = cw_ref[...]
    lead = SUBLANES - (CONV_WIDTH - 1)
    xc = cb_ref[...] + xe_scr[lead:lead + ts, :] * cw[0:1, :]
    for w in range(1, CONV_WIDTH):
        xc = xc + xe_scr[lead + w:lead + w + ts, :] * cw[w:w + 1, :]
    xe_scr[0:SUBLANES, :] = xe_scr[ts:ts + SUBLANES, :]

    nlam = -lam_ref[...]
    softplus = jnp.maximum(nlam, 0.0) + jnp.log1p(jnp.exp(-jnp.abs(nlam)))
    sub = lax.broadcasted_iota(jnp.int32, (groups, SUBLANES, LRU_BLOCK_DIM), 1)
    for k in range(LRU_CW // LRU_BLOCK_DIM):
        sl = slice(k * LRU_BLOCK_DIM, (k + 1) * LRU_BLOCK_DIM)
        xck = xc[:, sl]
        xb = xck.astype(BF16)
        r = jax.nn.sigmoid(_bdot(xb, wr_ref[k]) + br_ref[:, sl])
        ig = jax.nn.sigmoid(_bdot(xb, wi_ref[k]) + bi_ref[:, sl])
        log_a = (-LRU_C * r) * softplus[:, sl]
        em1 = jnp.tanh(log_a) * (jnp.exp(2.0 * log_a) + 1.0)
        a = jnp.exp(log_a).reshape(groups, SUBLANES, LRU_BLOCK_DIM)
        b = (jnp.sqrt(-em1) * (ig * xck)).reshape(groups, SUBLANES, LRU_BLOCK_DIM)
        d = 1
        while d < SUBLANES:
            a_prev = jnp.where(sub < d, 1.0, pltpu.roll(a, d, axis=1))
            b_prev = jnp.where(sub < d, 0.0, pltpu.roll(b, d, axis=1))
            b = a * b_prev + b
            a = a * a_prev
            d *= 2
        a_scr[:, sl] = a.reshape(ts, LRU_BLOCK_DIM)
        b_scr[:, sl] = b.reshape(ts, LRU_BLOCK_DIM)

    def step(g, h_prev):
        r0 = pl.multiple_of(g * SUBLANES, SUBLANES)
        h = b_scr[pl.ds(r0, SUBLANES), :] + a_scr[pl.ds(r0, SUBLANES), :] * h_prev
        b_scr[pl.ds(r0, SUBLANES), :] = h
        return jnp.broadcast_to(h[SUBLANES - 1:SUBLANES, :], (SUBLANES, LRU_CW))

    h_scr[...] = lax.fori_loop(0, groups, step, h_scr[...], unroll=4)
    o_ref[...] = (b_scr[...] * _gelu_tanh(ya_ref[...].astype(F32))).astype(BF16)


def _rglru(proj2d, conv_w, conv_b, wr, br, wi, bi, lam):
    ts = LRU_TS
    ncb = D_RNN // LRU_CW
    kb = LRU_CW // LRU_BLOCK_DIM
    pw = PROJ_W // LRU_CW
    vec = pl.BlockSpec((1, LRU_CW), lambda b, cb, tc: (0, cb))
    blk = pl.BlockSpec((kb, LRU_BLOCK_DIM, LRU_BLOCK_DIM), lambda b, cb, tc: (cb, 0, 0))
    return pl.pallas_call(
        _lru_kernel,
        grid=(BATCH, ncb, SEQ // ts),
        in_specs=[
            pl.BlockSpec((ts, LRU_CW), lambda b, cb, tc: (tc, b * pw + COL_XA // LRU_CW + cb)),
            pl.BlockSpec((ts, LRU_CW), lambda b, cb, tc: (tc, b * pw + COL_YA // LRU_CW + cb)),
            pl.BlockSpec((CONV_WIDTH, LRU_CW), lambda b, cb, tc: (0, cb)),
            vec, blk, vec, blk, vec, vec,
        ],
        out_specs=pl.BlockSpec((ts, LRU_CW), lambda b, cb, tc: (tc, b * ncb + cb)),
        out_shape=jax.ShapeDtypeStruct((SEQ, BATCH * D_RNN), BF16),
        scratch_shapes=[
            pltpu.VMEM((ts + SUBLANES, LRU_CW), F32),
            pltpu.VMEM((ts, LRU_CW), F32),
            pltpu.VMEM((ts, LRU_CW), F32),
            pltpu.VMEM((SUBLANES, LRU_CW), F32),
        ],
        compiler_params=pltpu.CompilerParams(
            dimension_semantics=("parallel", "parallel", "arbitrary"), vmem_limit_bytes=_vmem(40)),
        name="rglru",
    )(proj2d, proj2d, conv_w, conv_b, wr, br, wi, bi, lam)


def _cmp_kernel(x_ref, pe_ref, w1_ref, w2_ref, o_ref, xs_scr, cat_scr):
    xs_scr[0:SEQ, :] = x_ref[...].astype(F32)
    xs_scr[SEQ:SEQ + CMP_STRIDE, :] = jnp.zeros((CMP_STRIDE, HEAD_DIM), F32)
    pe = pe_ref[...]
    for p in range(CMP_BLOCK):
        tok = xs_scr[pl.ds(p, N_CMP_PAD, stride=CMP_STRIDE), :]
        cat_scr[:, p * HEAD_DIM:(p + 1) * HEAD_DIM] = (tok + pe[p:p + 1, :]).astype(BF16)
    pre = _bdot(cat_scr[...], w1_ref[...])
    o_ref[...] = _bdot(_gelu_tanh(pre).astype(BF16), w2_ref[...]).astype(BF16)


def _compress(proj2d, pe, w1, w2):
    pw_h = PROJ_W // HEAD_DIM
    base = COL_KV // HEAD_DIM
    return pl.pallas_call(
        _cmp_kernel,
        grid=(2, BATCH, N_KV_GROUPS),
        in_specs=[
            pl.BlockSpec((SEQ, HEAD_DIM), lambda t, b, g: (0, b * pw_h + base + t * N_KV_GROUPS + g)),
            pl.BlockSpec((None, CMP_BLOCK, HEAD_DIM), lambda t, b, g: (t, 0, 0)),
            pl.BlockSpec((None, CMP_BLOCK * HEAD_DIM, HEAD_DIM), lambda t, b, g: (t, 0, 0)),
            pl.BlockSpec((None, HEAD_DIM, HEAD_DIM), lambda t, b, g: (t, 0, 0)),
        ],
        out_specs=pl.BlockSpec((None, None, None, N_CMP_PAD, HEAD_DIM), lambda t, b, g: (t, b, g, 0, 0)),
        out_shape=jax.ShapeDtypeStruct((2, BATCH, N_KV_GROUPS, N_CMP_PAD, HEAD_DIM), BF16),
        scratch_shapes=[
            pltpu.VMEM((SEQ + CMP_STRIDE, HEAD_DIM), F32),
            pltpu.VMEM((N_CMP_PAD, CMP_BLOCK * HEAD_DIM), BF16),
        ],
        compiler_params=pltpu.CompilerParams(
            dimension_semantics=("parallel", "parallel", "parallel"), vmem_limit_bytes=_vmem(32)),
        name="kv_compress",
    )(proj2d, pe, w1, w2)


def _attn_kernel(q_ref, ks_ref, vs_ref, kw_ref, vw_ref, kc_ref, vc_ref, gt_ref, o_ref,
                 q4_scr, vse_scr, vwe_scr, acc_scr, m_scr, ocmp_scr):
    t = ATT_T
    hpg = HEADS_PER_GROUP
    rows = hpg * t
    i = pl.program_id(2)
    q0 = i * t

    @pl.when(i == 0)
    def _():
        ones = jnp.ones((SEQ, HEAD_DIM), BF16)
        vse_scr[:, 0:HEAD_DIM] = vs_ref[...]
        vse_scr[:, HEAD_DIM:2 * HEAD_DIM] = ones
        vwe_scr[:, 0:HEAD_DIM] = vw_ref[...]
        vwe_scr[:, HEAD_DIM:2 * HEAD_DIM] = ones

    for hh in range(hpg):
        q4_scr[hh * t:(hh + 1) * t, :] = q_ref[:, hh * HEAD_DIM:(hh + 1) * HEAD_DIM]
    q4 = q4_scr[...]

    row = lax.broadcasted_iota(jnp.int32, (rows, LANES), 0)
    lane = lax.broadcasted_iota(jnp.int32, (rows, LANES), 1)
    cmp_ok = (lane * CMP_STRIDE + (CMP_BLOCK - 1)) <= (q0 + jnp.bitwise_and(row, t - 1))
    s = jnp.where(cmp_ok, _bdot_nt(q4, kc_ref[...]), MASK_NEG)
    e = jnp.where(cmp_ok, jnp.exp(s - jnp.max(s, axis=1, keepdims=True)), 0.0)
    den = jnp.sum(e, axis=1, keepdims=True)
    p = e / jnp.where(den > 0.0, den, 1.0)
    ocmp_scr[...] = _bdot(p.astype(BF16), vc_ref[...])
    p_sum = p[0:t]
    for hh in range(1, hpg):
        p_sum = p_sum + p[hh * t:(hh + 1) * t]

    jj = lax.broadcasted_iota(jnp.int32, (LANES, LANES), 0)
    nn = lax.broadcasted_iota(jnp.int32, (LANES, LANES), 1)
    overlap = jnp.logical_and(
        jnp.logical_and(nn * CMP_STRIDE < (jj + 1) * SLC_BLOCK, nn * CMP_STRIDE + CMP_BLOCK > jj * SLC_BLOCK),
        jj < N_SLC)
    overlap_t = jnp.where(overlap, 1.0, 0.0).astype(BF16)
    p_hi = p_sum.astype(BF16)
    p_lo = (p_sum - p_hi.astype(F32)).astype(BF16)
    imp_t = (_bdot_nt(overlap_t, p_hi) + _bdot_nt(overlap_t, p_lo))[0:N_SLC, :]

    blk = lax.broadcasted_iota(jnp.int32, (N_SLC, t), 0)
    pos = q0 + lax.broadcasted_iota(jnp.int32, (N_SLC, t), 1)
    cur = jnp.right_shift(pos, 6)
    forced = jnp.logical_or(blk == 0, jnp.logical_or(blk == cur, blk == cur - 1))
    val = jnp.where(forced, jnp.inf, jnp.where(blk * SLC_BLOCK <= pos, imp_t, -jnp.inf))
    rank = jnp.zeros((N_SLC, t), F32)
    for c in range(N_SLC):
        vc_row = val[c:c + 1, :]
        ahead = jnp.logical_or(vc_row > val, jnp.logical_and(vc_row == val, blk > c))
        rank = rank + jnp.where(ahead, 1.0, 0.0)
    notsel_t = jnp.where(rank < float(N_SELECT), 0.0, MASK_NEG)
    notsel_t = jnp.concatenate([notsel_t, jnp.zeros((LANES - N_SLC, t), F32)], axis=0)
    notsel = notsel_t.T.astype(BF16)

    m_scr[...] = jnp.full(m_scr.shape, M_INIT, F32)
    acc_scr[...] = jnp.zeros_like(acc_scr)

    r2 = lax.broadcasted_iota(jnp.int32, (t, t), 0)
    c2 = lax.broadcasted_iota(jnp.int32, (t, t), 1)
    causal_bias = jnp.where(c2 <= r2, 0.0, MASK_NEG)
    band_bias = jnp.where(c2 > r2, 0.0, MASK_NEG)
    sel_j = lax.broadcasted_iota(jnp.int32, (LANES, t), 0)
    sel_k = jnp.right_shift(lax.broadcasted_iota(jnp.int32, (LANES, t), 1), 6)

    def tile_update(slot, k_ref, ve_scr, kt, bias):
        k0 = pl.multiple_of(kt * t, t)
        k_tile = k_ref[pl.ds(k0, t), :]
        v_tile = ve_scr[pl.ds(k0, t), :]
        for hh in range(hpg):
            sl = slice(hh * t, (hh + 1) * t)
            s = _bdot_nt(q4_scr[sl, :], k_tile)
            if bias is not None:
                s = s + bias
            m_prev = m_scr[slot, sl, :]
            m_new = jnp.maximum(m_prev, jnp.max(s, axis=1, keepdims=True))
            alpha = jnp.exp(m_prev - m_new)
            p = jnp.exp(s - jnp.concatenate([m_new] * (t // LANES), axis=1))
            pv = _bdot(p.astype(BF16), v_tile)
            acc_scr[slot, sl, :] = jnp.concatenate([alpha, alpha], axis=1) * acc_scr[slot, sl, :] + pv
            m_scr[slot, sl, :] = m_new

    def sel_bias(kt):
        expand = jnp.where(sel_j - kt * (t // SLC_BLOCK) == sel_k, 1.0, 0.0).astype(BF16)
        return _bdot(notsel, expand)

    def slc_body(kt, carry):
        tile_update(0, ks_ref, vse_scr, kt, sel_bias(kt))
        return carry

    lax.fori_loop(0, i, slc_body, 0)
    tile_update(0, ks_ref, vse_scr, i, sel_bias(i) + causal_bias)

    @pl.when(i >= 2)
    def _():
        tile_update(1, kw_ref, vwe_scr, i - 2, band_bias)

    @pl.when(i >= 1)
    def _():
        tile_update(1, kw_ref, vwe_scr, i - 1, None)

    tile_update(1, kw_ref, vwe_scr, i, causal_bias)

    gates = jax.nn.sigmoid(gt_ref[...].astype(F32))
    acc_s = acc_scr[0]
    acc_w = acc_scr[1]
    o_slc = acc_s[:, 0:HEAD_DIM] / acc_s[:, HEAD_DIM:2 * HEAD_DIM]
    o_win = acc_w[:, 0:HEAD_DIM] / acc_w[:, HEAD_DIM:2 * HEAD_DIM]
    o_cmp = ocmp_scr[...]
    for hh in range(hpg):
        sl = slice(hh * t, (hh + 1) * t)
        o = (gates[:, 3 * hh:3 * hh + 1] * o_cmp[sl]
             + gates[:, 3 * hh + 1:3 * hh + 2] * o_slc[sl]
             + gates[:, 3 * hh + 2:3 * hh + 3] * o_win[sl])
        o_ref[:, hh * HEAD_DIM:(hh + 1) * HEAD_DIM] = o.astype(BF16)


def _attention(proj2d, kcvc):
    t = ATT_T
    g_w = HEADS_PER_GROUP * HEAD_DIM
    rows = HEADS_PER_GROUP * t
    pw_g = PROJ_W // g_w
    pw_h = PROJ_W // HEAD_DIM

    def kv_spec(j):
        base = (COL_KV + j * KV_WIDTH) // HEAD_DIM
        return pl.BlockSpec((SEQ, HEAD_DIM), lambda b, g, i: (0, b * pw_h + base + g))

    def cmp_spec(tsel):
        return pl.BlockSpec((None, None, None, N_CMP_PAD, HEAD_DIM), lambda b, g, i: (tsel, b, g, 0, 0))

    return pl.pallas_call(
        _attn_kernel,
        grid=(BATCH, N_KV_GROUPS, SEQ // t),
        in_specs=[
            pl.BlockSpec((t, g_w), lambda b, g, i: (i, b * pw_g + COL_Q // g_w + g)),
            kv_spec(2), kv_spec(3), kv_spec(4), kv_spec(5),
            cmp_spec(0), cmp_spec(1),
            pl.BlockSpec((t, LANES), lambda b, g, i: (i, b * pw_h + COL_NG // LANES + g)),
        ],
        out_specs=pl.BlockSpec((t, g_w), lambda b, g, i: (i, b * N_KV_GROUPS + g)),
        out_shape=jax.ShapeDtypeStruct((SEQ, BATCH * Q_WIDTH), BF16),
        scratch_shapes=[
            pltpu.VMEM((rows, HEAD_DIM), BF16),
            pltpu.VMEM((SEQ, 2 * HEAD_DIM), BF16),
            pltpu.VMEM((SEQ, 2 * HEAD_DIM), BF16),
            pltpu.VMEM((2, rows, 2 * HEAD_DIM), F32),
            pltpu.VMEM((2, rows, LANES), F32),
            pltpu.VMEM((rows, HEAD_DIM), F32),
        ],
        compiler_params=pltpu.CompilerParams(
            dimension_semantics=("parallel", "parallel", "arbitrary"), vmem_limit_bytes=_vmem(40)),
        name="nsa_attention",
    )(proj2d, proj2d, proj2d, proj2d, proj2d, kcvc, kcvc, proj2d)


def _merge_kernel(ha_ref, ob_ref, wa_ref, wb_ref, ga_ref, gb_ref, o_ref):
    ya = _bdot(ha_ref[...], wa_ref[...])
    yb = _bdot(ob_ref[...], wb_ref[...])
    ga = jax.nn.sigmoid(ga_ref[...].astype(F32))
    gb = jax.nn.sigmoid(gb_ref[...].astype(F32))
    o_ref[...] = (ga * ya + gb * yb).astype(BF16)


def _merge(hg2d, ob2d, proj2d, wa, wb):
    ts, tn = OUT_TS, PROJ_TN
    nn = D_MODEL // tn
    pw = PROJ_W // tn
    return pl.pallas_call(
        _merge_kernel,
        grid=(SEQ // ts, BATCH, nn),
        in_specs=[
            pl.BlockSpec((ts, D_RNN), lambda i, b, n: (i, b)),
            pl.BlockSpec((ts, Q_WIDTH), lambda i, b, n: (i, b)),
            pl.BlockSpec((D_RNN, tn), lambda i, b, n: (0, n)),
            pl.BlockSpec((Q_WIDTH, tn), lambda i, b, n: (0, n)),
            pl.BlockSpec((ts, tn), lambda i, b, n: (i, b * pw + COL_MG // tn + n)),
            pl.BlockSpec((ts, tn), lambda i, b, n: (i, b * pw + (COL_MG + D_MODEL) // tn + n)),
        ],
        out_specs=pl.BlockSpec((ts, tn), lambda i, b, n: (i, b * nn + n)),
        out_shape=jax.ShapeDtypeStruct((SEQ, BATCH * D_MODEL), BF16),
        compiler_params=pltpu.CompilerParams(
            dimension_semantics=("parallel", "parallel", "arbitrary"), vmem_limit_bytes=_vmem(40)),
        name="branch_merge",
    )(hg2d, ob2d, wa, wb, proj2d, proj2d)


def _mixout_kernel(y_ref, w_ref, x_ref, gt_ref, post_ref, o_ref):
    mixed = _bdot(y_ref[...], w_ref[...])
    o_ref[...] = x_ref[...] + gt_ref[...] * _rms(mixed, post_ref[...])


def _mix_out(ymix, w_out, x1, mod3, post_g):
    ts = OUT_TS
    return pl.pallas_call(
        _mixout_kernel,
        grid=(SEQ // ts, BATCH),
        in_specs=[
            pl.BlockSpec((ts, D_MODEL), lambda i, b: (i, b)),
            pl.BlockSpec((D_MODEL, D_MODEL), lambda i, b: (0, 0)),
            pl.BlockSpec((ts, D_MODEL), lambda i, b: (i, b)),
            pl.BlockSpec((None, 1, D_MODEL), lambda i, b: (b * N_ADA + 5, 0, 0)),
            pl.BlockSpec((1, D_MODEL), lambda i, b: (0, 0)),
        ],
        out_specs=pl.BlockSpec((ts, D_MODEL), lambda i, b: (i, b)),
        out_shape=jax.ShapeDtypeStruct((SEQ, BATCH * D_MODEL), F32),
        compiler_params=pltpu.CompilerParams(
            dimension_semantics=("parallel", "parallel"), vmem_limit_bytes=_vmem(56)),
        name="mix_out",
    )(ymix, w_out, x1, mod3, post_g)


def _rope_tables():
    pos = jnp.arange(SEQ).astype(F32)
    inv_freq = ROPE_THETA ** (-jnp.arange(0, ROPE_DIM, 2, dtype=F32) / ROPE_DIM)
    ang = pos[:, None] * inv_freq[None, :]
    cos, sin = jnp.cos(ang), jnp.sin(ang)
    rest = HEAD_DIM - ROPE_DIM
    cos_t = jnp.concatenate([cos, cos, jnp.ones((SEQ, rest), F32)], axis=1)
    sin_t = jnp.concatenate([-sin, sin, jnp.zeros((SEQ, rest), F32)], axis=1)
    return cos_t, sin_t


def _pack_w_in(w_in):
    n_main = COL_MG
    gates = w_in[:, n_main:n_main + 3 * N_HEADS].reshape(D_MODEL, N_KV_GROUPS, 3 * HEADS_PER_GROUP)
    gates = jnp.pad(gates, ((0, 0), (0, 0), (0, LANES - 3 * HEADS_PER_GROUP))).reshape(D_MODEL, NG_PAD)
    return jnp.concatenate([w_in[:, :n_main], w_in[:, n_main + 3 * N_HEADS:], gates], axis=1).astype(BF16)


def kernel(x, c, w_ada, b_ada, ffn1_pre_g, ffn1_post_g, ffn1_w_gate, ffn1_w_up, ffn1_w_down, mix_pre_g, mix_post_g, w_in, conv_w, conv_b, lru_wr, lru_br, lru_wi, lru_bi, lru_lambda, cmp_pe_k, cmp_w1_k, cmp_w2_k, cmp_pe_v, cmp_w1_v, cmp_w2_v, w_a_out, w_b_out, w_out, ffn2_pre_g, ffn2_post_g, ffn2_w_gate, ffn2_w_up, ffn2_w_down):
    assert x.shape == (BATCH, SEQ, D_MODEL) and w_ada.shape[0] == 1
    mod = _modulation(c, w_ada[0], b_ada)
    mod3 = mod.reshape(BATCH * N_ADA, 1, D_MODEL)

    x1 = _ffn(x, mod3, 0, ffn1_pre_g, ffn1_post_g, ffn1_w_gate[0].astype(BF16),
              ffn1_w_up[0].astype(BF16), ffn1_w_down[0].astype(BF16),
              x_time_major=False, out_time_major=True)

    cos_t, sin_t = _rope_tables()
    proj2d = _projection(x1, mod3, mix_pre_g, _pack_w_in(w_in[0]), cos_t, sin_t)

    hg = _rglru(proj2d, conv_w[0], conv_b, lru_wr[0].astype(BF16), lru_br,
                lru_wi[0].astype(BF16), lru_bi, lru_lambda)

    pe = jnp.stack([cmp_pe_k[0], cmp_pe_v[0]])
    w1 = jnp.stack([cmp_w1_k[0], cmp_w1_v[0]]).astype(BF16)
    w2 = jnp.stack([cmp_w2_k[0], cmp_w2_v[0]]).astype(BF16)
    kcvc = _compress(proj2d, pe, w1, w2)

    ob = _attention(proj2d, kcvc)
    ymix = _merge(hg, ob, proj2d, w_a_out[0].astype(BF16), w_b_out[0].astype(BF16))
    x2 = _mix_out(ymix, w_out[0].astype(BF16), x1, mod3, mix_post_g)

    return _ffn(x2, mod3, 6, ffn2_pre_g, ffn2_post_g, ffn2_w_gate[0].astype(BF16),
                ffn2_w_up[0].astype(BF16), ffn2_w_down[0].astype(BF16),
                x_time_major=True, out_time_major=False)
```

```python
import math

import jax
import jax.numpy as jnp
from jax import lax
from jax.experimental import pallas as pl
from jax.experimental.pallas import tpu as pltpu

F32 = jnp.float32
BF16 = jnp.bfloat16

D_MODEL = 2048
BATCH = 8
SEQ = 2048
D_RNN = D_MODEL
LRU_BLOCKS = 16
LRU_BLOCK_DIM = D_RNN // LRU_BLOCKS
CONV_WIDTH = 4
LRU_C = 8.0
N_HEADS = 16
HEAD_DIM = 128
N_KV_GROUPS = 4
HEADS_PER_GROUP = N_HEADS // N_KV_GROUPS
Q_WIDTH = N_HEADS * HEAD_DIM
KV_WIDTH = N_KV_GROUPS * HEAD_DIM
CMP_STRIDE = 16
CMP_BLOCK = 2 * CMP_STRIDE
SLC_BLOCK = 64
N_SELECT = 16
WINDOW = 512
ROPE_THETA = 500000.0
ROPE_DIM = HEAD_DIM // 4
D_FF = 5632
NORM_EPS = 1e-6
N_ADA = 9
N_SLC = SEQ // SLC_BLOCK
N_CMP_PAD = SEQ // CMP_STRIDE

LANES = 128
SUBLANES = 8

COL_XA = 0
COL_YA = COL_XA + D_RNN
COL_Q = COL_YA + D_RNN
COL_KV = COL_Q + Q_WIDTH
COL_MG = COL_KV + 6 * KV_WIDTH
COL_NG = COL_MG + 2 * D_MODEL
PROJ_TN = 512
NG_PAD = PROJ_TN
PROJ_W = COL_NG + NG_PAD

MASK_NEG = -1e30
M_INIT = -1e29

FFN_TS = 512
FFN_TF = 512
PROJ_TS = 1024
PROJ_MC = 256
OUT_TS = 512
OUT_MC = 256
ATT_T = 256
LRU_TS = 512
LRU_CW = 512
MOD_TN = 1024


def _vmem(mb):
    return mb * 1024 * 1024


def _rms(x, g):
    return x * lax.rsqrt(jnp.mean(x * x, axis=-1, keepdims=True) + NORM_EPS) * g


def _gelu_tanh(x):
    c = math.sqrt(2.0 / math.pi)
    return x * (0.5 * (1.0 + jnp.tanh(c * (x + 0.044715 * (x * x * x)))))


def _bdot(a, b):
    return jnp.dot(a, b, preferred_element_type=F32)


def _bdot_nt(a, b):
    return lax.dot_general(a, b, (((1,), (1,)), ((), ())), preferred_element_type=F32)


def _mod_kernel(c_ref, w_ref, b_ref, o_ref):
    c = c_ref[...]
    ca = c * jax.nn.sigmoid(c)
    o_ref[...] = _bdot(ca.astype(BF16), w_ref[...].astype(BF16)) + b_ref[...]


def _modulation(c, w_ada, b_ada):
    n = N_ADA * D_MODEL
    return pl.pallas_call(
        _mod_kernel,
        grid=(n // MOD_TN,),
        in_specs=[
            pl.BlockSpec((BATCH, D_MODEL), lambda j: (0, 0)),
            pl.BlockSpec((D_MODEL, MOD_TN), lambda j: (0, j)),
            pl.BlockSpec((1, MOD_TN), lambda j: (0, j)),
        ],
        out_specs=pl.BlockSpec((BATCH, MOD_TN), lambda j: (0, j)),
        out_shape=jax.ShapeDtypeStruct((BATCH, n), F32),
        compiler_params=pltpu.CompilerParams(
            dimension_semantics=("arbitrary",), vmem_limit_bytes=_vmem(40)),
        name="adaln_mod",
    )(c, w_ada, b_ada)


def _ffn_kernel(x_ref, sh_ref, sc_ref, gt_ref, pre_ref, post_ref, wg_ref, wu_ref, wd_ref,
                o_ref, u_scr, acc_scr):
    f = pl.program_id(2)

    @pl.when(f == 0)
    def _():
        u = _rms(x_ref[...], pre_ref[...]) * (1.0 + sc_ref[...]) + sh_ref[...]
        u_scr[...] = u.astype(BF16)
        acc_scr[...] = jnp.zeros_like(acc_scr)

    u = u_scr[...]
    gate = _bdot(u, wg_ref[...])
    up = _bdot(u, wu_ref[...])
    h = (gate * jax.nn.sigmoid(gate)) * up
    acc_scr[...] += _bdot(h.astype(BF16), wd_ref[...])

    @pl.when(f == pl.num_programs(2) - 1)
    def _():
        y = _rms(acc_scr[...], post_ref[...])
        o_ref[...] = x_ref[...] + 0.5 * gt_ref[...] * y


def _ffn(x, mod3, k_mod, pre_g, post_g, wg, wu, wd, *, x_time_major, out_time_major):
    ts, tf = FFN_TS, FFN_TF
    if x_time_major:
        x_spec = pl.BlockSpec((ts, D_MODEL), lambda i, b, f: (i, b))
    else:
        x_spec = pl.BlockSpec((None, ts, D_MODEL), lambda i, b, f: (b, i, 0))
    if out_time_major:
        o_spec = pl.BlockSpec((ts, D_MODEL), lambda i, b, f: (i, b))
        o_shape = jax.ShapeDtypeStruct((SEQ, BATCH * D_MODEL), F32)
    else:
        o_spec = pl.BlockSpec((None, ts, D_MODEL), lambda i, b, f: (b, i, 0))
        o_shape = jax.ShapeDtypeStruct((BATCH, SEQ, D_MODEL), F32)

    def mod_spec(k):
        return pl.BlockSpec((None, 1, D_MODEL), lambda i, b, f: (b * N_ADA + k, 0, 0))

    vec_spec = pl.BlockSpec((1, D_MODEL), lambda i, b, f: (0, 0))
    return pl.pallas_call(
        _ffn_kernel,
        grid=(SEQ // ts, BATCH, D_FF // tf),
        in_specs=[
            x_spec, mod_spec(k_mod), mod_spec(k_mod + 1), mod_spec(k_mod + 2), vec_spec, vec_spec,
            pl.BlockSpec((D_MODEL, tf), lambda i, b, f: (0, f)),
            pl.BlockSpec((D_MODEL, tf), lambda i, b, f: (0, f)),
            pl.BlockSpec((tf, D_MODEL), lambda i, b, f: (f, 0)),
        ],
        out_specs=o_spec,
        out_shape=o_shape,
        scratch_shapes=[pltpu.VMEM((ts, D_MODEL), BF16), pltpu.VMEM((ts, D_MODEL), F32)],
        compiler_params=pltpu.CompilerParams(
            dimension_semantics=("parallel", "parallel", "arbitrary"), vmem_limit_bytes=_vmem(56)),
        name="macaron_ffn",
    )(x, mod3, mod3, mod3, pre_g, post_g, wg, wu, wd)


def _rope_slice(x, c, s):
    return x * c + pltpu.roll(x, LANES // 2, axis=1) * s


def _proj_kernel(x_ref, sh_ref, sc_ref, pre_ref, w_ref, cos_ref, sin_ref, o_ref, u_scr):
    n = pl.program_id(2)

    @pl.when(n == 0)
    def _():
        u = _rms(x_ref[...], pre_ref[...]) * (1.0 + sc_ref[...]) + sh_ref[...]
        u_scr[...] = u.astype(BF16)

    w = w_ref[...]
    for mc in range(PROJ_TS // PROJ_MC):
        rs = slice(mc * PROJ_MC, (mc + 1) * PROJ_MC)
        r = _bdot(u_scr[rs, :], w)
        c = cos_ref[rs, :]
        s = sin_ref[rs, :]
        for hh in range(PROJ_TN // HEAD_DIM):
            sl = slice(hh * HEAD_DIM, (hh + 1) * HEAD_DIM)
            o_ref[rs, sl] = _rope_slice(r[:, sl], c, s).astype(BF16)


def _rope_kind(n):
    q_lo, q_hi = COL_Q // PROJ_TN, COL_KV // PROJ_TN
    is_k = n == COL_KV // PROJ_TN
    for j in range(1, 3):
        is_k = jnp.logical_or(is_k, n == (COL_KV + 2 * j * KV_WIDTH) // PROJ_TN)
    return jnp.where(jnp.logical_and(n >= q_lo, n < q_hi), 2, jnp.where(is_k, 1, 0))


def _projection(x1, mod3, pre_g, w_all, cos_t, sin_t):
    ts, tn = PROJ_TS, PROJ_TN
    nn = PROJ_W // tn

    def mod_spec(k):
        return pl.BlockSpec((None, 1, D_MODEL), lambda i, b, n: (b * N_ADA + k, 0, 0))

    tab_spec = pl.BlockSpec((None, ts, LANES), lambda i, b, n: (_rope_kind(n), i, 0))

    return pl.pallas_call(
        _proj_kernel,
        grid=(SEQ // ts, BATCH, nn),
        in_specs=[
            pl.BlockSpec((ts, D_MODEL), lambda i, b, n: (i, b)),
            mod_spec(3), mod_spec(4),
            pl.BlockSpec((1, D_MODEL), lambda i, b, n: (0, 0)),
            pl.BlockSpec((D_MODEL, tn), lambda i, b, n: (0, n)),
            tab_spec, tab_spec,
        ],
        out_specs=pl.BlockSpec((ts, tn), lambda i, b, n: (i, b * nn + n)),
        out_shape=jax.ShapeDtypeStruct((SEQ, BATCH * PROJ_W), BF16),
        scratch_shapes=[pltpu.VMEM((ts, D_MODEL), BF16)],
        compiler_params=pltpu.CompilerParams(
            dimension_semantics=("parallel", "parallel", "arbitrary"), vmem_limit_bytes=_vmem(48)),
        name="mix_in_proj",
    )(x1, mod3, mod3, pre_g, w_all, cos_t, sin_t)


def _lru_kernel(xa_ref, ya_ref, cw_ref, cb_ref, wr_ref, br_ref, wi_ref, bi_ref, lam_ref,
                o_ref, xe_scr, a_scr, b_scr, h_scr):
    tc = pl.program_id(2)
    ts = LRU_TS
    groups = ts // SUBLANES

    @pl.when(tc == 0)
    def _():
        xe_scr[0:SUBLANES, :] = jnp.zeros((SUBLANES, LRU_CW), F32)
        h_scr[...] = jnp.zeros_like(h_scr)

    xe_scr[SUBLANES:SUBLANES + ts, :] = xa_ref[...].astype(F32)
    cw = cw_ref[...]
    lead = SUBLANES - (CONV_WIDTH - 1)
    xc = cb_ref[...] + xe_scr[lead:lead + ts, :] * cw[0:1, :]
    for w in range(1, CONV_WIDTH):
        xc = xc + xe_scr[lead + w:lead + w + ts, :] * cw[w:w + 1, :]
    xe_scr[0:SUBLANES, :] = xe_scr[ts:ts + SUBLANES, :]

    nlam = -lam_ref[...]
    softplus = jnp.maximum(nlam, 0.0) + jnp.log1p(jnp.exp(-jnp.abs(nlam)))
    sub = lax.broadcasted_iota(jnp.int32, (groups, SUBLANES, LRU_BLOCK_DIM), 1)
    for k in range(LRU_CW // LRU_BLOCK_DIM):
        sl = slice(k * LRU_BLOCK_DIM, (k + 1) * LRU_BLOCK_DIM)
        xck = xc[:, sl]
        xb = xck.astype(BF16)
        r = jax.nn.sigmoid(_bdot(xb, wr_ref[k]) + br_ref[:, sl])
        ig = jax.nn.sigmoid(_bdot(xb, wi_ref[k]) + bi_ref[:, sl])
        log_a = (-LRU_C * r) * softplus[:, sl]
        em1 = jnp.tanh(log_a) * (jnp.exp(2.0 * log_a) + 1.0)
        a = jnp.exp(log_a).reshape(groups, SUBLANES, LRU_BLOCK_DIM)
        b = (jnp.sqrt(-em1) * (ig * xck)).reshape(groups, SUBLANES, LRU_BLOCK_DIM)
        d = 1
        while d < SUBLANES:
            a_prev = jnp.where(sub < d, 1.0, pltpu.roll(a, d, axis=1))
            b_prev = jnp.where(sub < d, 0.0, pltpu.roll(b, d, axis=1))
            b = a * b_prev + b
            a = a * a_prev
            d *= 2
        a_scr[:, sl] = a.reshape(ts, LRU_BLOCK_DIM)
        b_scr[:, sl] = b.reshape(ts, LRU_BLOCK_DIM)

    def step(g, h_prev):
        r0 = pl.multiple_of(g * SUBLANES, SUBLANES)
        h = b_scr[pl.ds(r0, SUBLANES), :] + a_scr[pl.ds(r0, SUBLANES), :] * h_prev
        b_scr[pl.ds(r0, SUBLANES), :] = h
        return jnp.broadcast_to(h[SUBLANES - 1:SUBLANES, :], (SUBLANES, LRU_CW))

    h_scr[...] = lax.fori_loop(0, groups, step, h_scr[...], unroll=4)
    o_ref[...] = (b_scr[...] * _gelu_tanh(ya_ref[...].astype(F32))).astype(BF16)


def _rglru(proj2d, conv_w, conv_b, wr, br, wi, bi, lam):
    ts = LRU_TS
    ncb = D_RNN // LRU_CW
    kb = LRU_CW // LRU_BLOCK_DIM
    pw = PROJ_W // LRU_CW
    vec = pl.BlockSpec((1, LRU_CW), lambda b, cb, tc: (0, cb))
    blk = pl.BlockSpec((kb, LRU_BLOCK_DIM, LRU_BLOCK_DIM), lambda b, cb, tc: (cb, 0, 0))
    return pl.pallas_call(
        _lru_kernel,
        grid=(BATCH, ncb, SEQ // ts),
        in_specs=[
            pl.BlockSpec((ts, LRU_CW), lambda b, cb, tc: (tc, b * pw + COL_XA // LRU_CW + cb)),
            pl.BlockSpec((ts, LRU_CW), lambda b, cb, tc: (tc, b * pw + COL_YA // LRU_CW + cb)),
            pl.BlockSpec((CONV_WIDTH, LRU_CW), lambda b, cb, tc: (0, cb)),
            vec, blk, vec, blk, vec, vec,
        ],
        out_specs=pl.BlockSpec((ts, LRU_CW), lambda b, cb, tc: (tc, b * ncb + cb)),
        out_shape=jax.ShapeDtypeStruct((SEQ, BATCH * D_RNN), BF16),
        scratch_shapes=[
            pltpu.VMEM((ts + SUBLANES, LRU_CW), F32),
            pltpu.VMEM((ts, LRU_CW), F32),
            pltpu.VMEM((ts, LRU_CW), F32),
            pltpu.VMEM((SUBLANES, LRU_CW), F32),
        ],
        compiler_params=pltpu.CompilerParams(
            dimension_semantics=("parallel", "parallel", "arbitrary"), vmem_limit_bytes=_vmem(40)),
        name="rglru",
    )(proj2d, proj2d, conv_w, conv_b, wr, br, wi, bi, lam)


def _cmp_kernel(x_ref, pe_ref, w1_ref, w2_ref, o_ref, xs_scr, cat_scr):
    xs_scr[0:SEQ, :] = x_ref[...].astype(F32)
    xs_scr[SEQ:SEQ + CMP_STRIDE, :] = jnp.zeros((CMP_STRIDE, HEAD_DIM), F32)
    pe = pe_ref[...]
    for p in range(CMP_BLOCK):
        tok = xs_scr[pl.ds(p, N_CMP_PAD, stride=CMP_STRIDE), :]
        cat_scr[:, p * HEAD_DIM:(p + 1) * HEAD_DIM] = (tok + pe[p:p + 1, :]).astype(BF16)
    pre = _bdot(cat_scr[...], w1_ref[...])
    o_ref[...] = _bdot(_gelu_tanh(pre).astype(BF16), w2_ref[...]).astype(BF16)


def _compress(proj2d, pe, w1, w2):
    pw_h = PROJ_W // HEAD_DIM
    base = COL_KV // HEAD_DIM
    return pl.pallas_call(
        _cmp_kernel,
        grid=(2, BATCH, N_KV_GROUPS),
        in_specs=[
            pl.BlockSpec((SEQ, HEAD_DIM), lambda t, b, g: (0, b * pw_h + base + t * N_KV_GROUPS + g)),
            pl.BlockSpec((None, CMP_BLOCK, HEAD_DIM), lambda t, b, g: (t, 0, 0)),
            pl.BlockSpec((None, CMP_BLOCK * HEAD_DIM, HEAD_DIM), lambda t, b, g: (t, 0, 0)),
            pl.BlockSpec((None, HEAD_DIM, HEAD_DIM), lambda t, b, g: (t, 0, 0)),
        ],
        out_specs=pl.BlockSpec((None, None, None, N_CMP_PAD, HEAD_DIM), lambda t, b, g: (t, b, g, 0, 0)),
        out_shape=jax.ShapeDtypeStruct((2, BATCH, N_KV_GROUPS, N_CMP_PAD, HEAD_DIM), BF16),
        scratch_shapes=[
            pltpu.VMEM((SEQ + CMP_STRIDE, HEAD_DIM), F32),
            pltpu.VMEM((N_CMP_PAD, CMP_BLOCK * HEAD_DIM), BF16),
        ],
        compiler_params=pltpu.CompilerParams(
            dimension_semantics=("parallel", "parallel", "parallel"), vmem_limit_bytes=_vmem(32)),
        name="kv_compress",
    )(proj2d, pe, w1, w2)


def _attn_kernel(q_ref, ks_ref, vs_ref, kw_ref, vw_ref, kc_ref, vc_ref, gt_ref, o_ref,
                 q4_scr, kcat_scr, vcat_scr, s0_scr, s1_scr, acc_scr, m_scr, ocmp_scr):
    t = ATT_T
    hpg = HEADS_PER_GROUP
    rows = hpg * t
    i = pl.program_id(2)
    q0 = i * t

    @pl.when(i == 0)
    def _():
        ones = jnp.ones((SEQ, HEAD_DIM), BF16)
        kcat_scr[0] = ks_ref[...]
        kcat_scr[1] = kw_ref[...]
        vcat_scr[0, :, 0:HEAD_DIM] = vs_ref[...]
        vcat_scr[0, :, HEAD_DIM:2 * HEAD_DIM] = ones
        vcat_scr[1, :, 0:HEAD_DIM] = vw_ref[...]
        vcat_scr[1, :, HEAD_DIM:2 * HEAD_DIM] = ones

    for hh in range(hpg):
        q4_scr[hh * t:(hh + 1) * t, :] = q_ref[:, hh * HEAD_DIM:(hh + 1) * HEAD_DIM]
    q4 = q4_scr[...]

    n_slc = i + 1
    n_win = jnp.minimum(i + 1, WINDOW // t + 1)
    n_jobs = n_slc + n_win

    def job(j):
        is_win = j >= n_slc
        return is_win, jnp.where(is_win, i - n_win + 1 + (j - n_slc), j)

    def scores(j, s_scr):
        is_win, kt = job(j)
        k0 = pl.multiple_of(jnp.minimum(kt, i) * t, t)
        s_scr[...] = _bdot_nt(q4, kcat_scr[is_win.astype(jnp.int32), pl.ds(k0, t), :])

    scores(0, s0_scr)

    row = lax.broadcasted_iota(jnp.int32, (rows, LANES), 0)
    lane = lax.broadcasted_iota(jnp.int32, (rows, LANES), 1)
    cmp_ok = (lane * CMP_STRIDE + (CMP_BLOCK - 1)) <= (q0 + jnp.bitwise_and(row, t - 1))
    s = jnp.where(cmp_ok, _bdot_nt(q4, kc_ref[...]), MASK_NEG)
    e = jnp.where(cmp_ok, jnp.exp(s - jnp.max(s, axis=1, keepdims=True)), 0.0)
    den = jnp.sum(e, axis=1, keepdims=True)
    p = e / jnp.where(den > 0.0, den, 1.0)
    ocmp_scr[...] = _bdot(p.astype(BF16), vc_ref[...])
    p_sum = p[0:t]
    for hh in range(1, hpg):
        p_sum = p_sum + p[hh * t:(hh + 1) * t]

    jj = lax.broadcasted_iota(jnp.int32, (LANES, LANES), 0)
    nn = lax.broadcasted_iota(jnp.int32, (LANES, LANES), 1)
    overlap = jnp.logical_and(
        jnp.logical_and(nn * CMP_STRIDE < (jj + 1) * SLC_BLOCK, nn * CMP_STRIDE + CMP_BLOCK > jj * SLC_BLOCK),
        jj < N_SLC)
    overlap_t = jnp.where(overlap, 1.0, 0.0).astype(BF16)
    p_hi = p_sum.astype(BF16)
    p_lo = (p_sum - p_hi.astype(F32)).astype(BF16)
    imp_t = (_bdot_nt(overlap_t, p_hi) + _bdot_nt(overlap_t, p_lo))[0:N_SLC, :]

    blk = lax.broadcasted_iota(jnp.int32, (N_SLC, t), 0)
    pos = q0 + lax.broadcasted_iota(jnp.int32, (N_SLC, t), 1)
    cur = jnp.right_shift(pos, 6)
    forced = jnp.logical_or(blk == 0, jnp.logical_or(blk == cur, blk == cur - 1))
    val = jnp.where(forced, jnp.inf, jnp.where(blk * SLC_BLOCK <= pos, imp_t, -jnp.inf))
    rank = jnp.zeros((N_SLC, t), F32)
    for c in range(N_SLC):
        vc_row = val[c:c + 1, :]
        ahead = jnp.logical_or(vc_row > val, jnp.logical_and(vc_row == val, blk > c))
        rank = rank + jnp.where(ahead, 1.0, 0.0)
    notsel_t = jnp.where(rank < float(N_SELECT), 0.0, MASK_NEG)
    notsel_t = jnp.concatenate([notsel_t, jnp.zeros((LANES - N_SLC, t), F32)], axis=0)
    notsel = notsel_t.T

    m_scr[...] = jnp.full(m_scr.shape, M_INIT, F32)
    acc_scr[...] = jnp.zeros_like(acc_scr)

    key_minus_query = (lax.broadcasted_iota(jnp.int32, (t, t), 1)
                       - lax.broadcasted_iota(jnp.int32, (t, t), 0))
    sel_j = lax.broadcasted_iota(jnp.int32, (LANES, t), 0)
    sel_k = jnp.right_shift(lax.broadcasted_iota(jnp.int32, (LANES, t), 1), 6)

    def update(j, s_scr):
        is_win, kt = job(j)
        slot = is_win.astype(jnp.int32)
        k0 = pl.multiple_of(jnp.minimum(kt, i) * t, t)
        v_tile = vcat_scr[slot, pl.ds(k0, t), :]
        delta = (kt - i) * t
        low = jnp.where(is_win, -WINDOW - delta, -2 * SEQ)
        visible = jnp.logical_and(key_minus_query <= -delta, key_minus_query > low)
        expand = jnp.where(sel_j - kt * (t // SLC_BLOCK) == sel_k, 1.0, 0.0).astype(BF16)
        nots = jnp.where(is_win, 0.0, notsel).astype(BF16)
        bias = jnp.where(visible, 0.0, MASK_NEG) + _bdot(nots, expand)
        for hh in range(hpg):
            sl = slice(hh * t, (hh + 1) * t)
            s = s_scr[sl, :] + bias
            m_prev = m_scr[slot, sl, :]
            m_new = jnp.maximum(m_prev, jnp.max(s, axis=1, keepdims=True))
            alpha = jnp.exp(m_prev - m_new)
            p = jnp.exp(s - jnp.concatenate([m_new] * (t // LANES), axis=1))
            pv = _bdot(p.astype(BF16), v_tile)
            acc_scr[slot, sl, :] = jnp.concatenate([alpha, alpha], axis=1) * acc_scr[slot, sl, :] + pv
            m_scr[slot, sl, :] = m_new

    def pair(pi, carry):
        j = 2 * pi
        scores(j + 1, s1_scr)
        update(j, s0_scr)
        scores(j + 2, s0_scr)
        update(j + 1, s1_scr)
        return carry

    lax.fori_loop(0, jnp.right_shift(n_jobs + 1, 1), pair, 0)

    gates = jax.nn.sigmoid(gt_ref[...].astype(F32))
    acc_s = acc_scr[0]
    acc_w = acc_scr[1]
    o_slc = acc_s[:, 0:HEAD_DIM] / acc_s[:, HEAD_DIM:2 * HEAD_DIM]
    o_win = acc_w[:, 0:HEAD_DIM] / acc_w[:, HEAD_DIM:2 * HEAD_DIM]
    o_cmp = ocmp_scr[...]
    for hh in range(hpg):
        sl = slice(hh * t, (hh + 1) * t)
        o = (gates[:, 3 * hh:3 * hh + 1] * o_cmp[sl]
             + gates[:, 3 * hh + 1:3 * hh + 2] * o_slc[sl]
             + gates[:, 3 * hh + 2:3 * hh + 3] * o_win[sl])
        o_ref[:, hh * HEAD_DIM:(hh + 1) * HEAD_DIM] = o.astype(BF16)


def _attention(proj2d, kcvc):
    t = ATT_T
    g_w = HEADS_PER_GROUP * HEAD_DIM
    rows = HEADS_PER_GROUP * t
    pw_g = PROJ_W // g_w
    pw_h = PROJ_W // HEAD_DIM

    def kv_spec(j):
        base = (COL_KV + j * KV_WIDTH) // HEAD_DIM
        return pl.BlockSpec((SEQ, HEAD_DIM), lambda b, g, i: (0, b * pw_h + base + g))

    def cmp_spec(tsel):
        return pl.BlockSpec((None, None, None, N_CMP_PAD, HEAD_DIM), lambda b, g, i: (tsel, b, g, 0, 0))

    return pl.pallas_call(
        _attn_kernel,
        grid=(BATCH, N_KV_GROUPS, SEQ // t),
        in_specs=[
            pl.BlockSpec((t, g_w), lambda b, g, i: (i, b * pw_g + COL_Q // g_w + g)),
            kv_spec(2), kv_spec(3), kv_spec(4), kv_spec(5),
            cmp_spec(0), cmp_spec(1),
            pl.BlockSpec((t, LANES), lambda b, g, i: (i, b * pw_h + COL_NG // LANES + g)),
        ],
        out_specs=pl.BlockSpec((t, g_w), lambda b, g, i: (i, b * N_KV_GROUPS + g)),
        out_shape=jax.ShapeDtypeStruct((SEQ, BATCH * Q_WIDTH), BF16),
        scratch_shapes=[
            pltpu.VMEM((rows, HEAD_DIM), BF16),
            pltpu.VMEM((2, SEQ, HEAD_DIM), BF16),
            pltpu.VMEM((2, SEQ, 2 * HEAD_DIM), BF16),
            pltpu.VMEM((rows, t), F32),
            pltpu.VMEM((rows, t), F32),
            pltpu.VMEM((2, rows, 2 * HEAD_DIM), F32),
            pltpu.VMEM((2, rows, LANES), F32),
            pltpu.VMEM((rows, HEAD_DIM), F32),
        ],
        compiler_params=pltpu.CompilerParams(
            dimension_semantics=("parallel", "parallel", "arbitrary"), vmem_limit_bytes=_vmem(40)),
        name="nsa_attention",
    )(proj2d, proj2d, proj2d, proj2d, proj2d, kcvc, kcvc, proj2d)


def _merge_kernel(ha_ref, ob_ref, wa_ref, wb_ref, ga_ref, gb_ref, o_ref):
    for mc in range(OUT_TS // OUT_MC):
        rs = slice(mc * OUT_MC, (mc + 1) * OUT_MC)
        ya = _bdot(ha_ref[rs, :], wa_ref[...])
        yb = _bdot(ob_ref[rs, :], wb_ref[...])
        ga = jax.nn.sigmoid(ga_ref[rs, :].astype(F32))
        gb = jax.nn.sigmoid(gb_ref[rs, :].astype(F32))
        o_ref[rs, :] = (ga * ya + gb * yb).astype(BF16)


def _merge(hg2d, ob2d, proj2d, wa, wb):
    ts, tn = OUT_TS, PROJ_TN
    nn = D_MODEL // tn
    pw = PROJ_W // tn
    return pl.pallas_call(
        _merge_kernel,
        grid=(SEQ // ts, BATCH, nn),
        in_specs=[
            pl.BlockSpec((ts, D_RNN), lambda i, b, n: (i, b)),
            pl.BlockSpec((ts, Q_WIDTH), lambda i, b, n: (i, b)),
            pl.BlockSpec((D_RNN, tn), lambda i, b, n: (0, n)),
            pl.BlockSpec((Q_WIDTH, tn), lambda i, b, n: (0, n)),
            pl.BlockSpec((ts, tn), lambda i, b, n: (i, b * pw + COL_MG // tn + n)),
            pl.BlockSpec((ts, tn), lambda i, b, n: (i, b * pw + (COL_MG + D_MODEL) // tn + n)),
        ],
        out_specs=pl.BlockSpec((ts, tn), lambda i, b, n: (i, b * nn + n)),
        out_shape=jax.ShapeDtypeStruct((SEQ, BATCH * D_MODEL), BF16),
        compiler_params=pltpu.CompilerParams(
            dimension_semantics=("parallel", "parallel", "arbitrary"), vmem_limit_bytes=_vmem(40)),
        name="branch_merge",
    )(hg2d, ob2d, wa, wb, proj2d, proj2d)


def _mixout_kernel(y_ref, w_ref, x_ref, gt_ref, post_ref, o_ref):
    mixed = _bdot(y_ref[...], w_ref[...])
    o_ref[...] = x_ref[...] + gt_ref[...] * _rms(mixed, post_ref[...])


def _mix_out(ymix, w_out, x1, mod3, post_g):
    ts = OUT_TS
    return pl.pallas_call(
        _mixout_kernel,
        grid=(SEQ // ts, BATCH),
        in_specs=[
            pl.BlockSpec((ts, D_MODEL), lambda i, b: (i, b)),
            pl.BlockSpec((D_MODEL, D_MODEL), lambda i, b: (0, 0)),
            pl.BlockSpec((ts, D_MODEL), lambda i, b: (i, b)),
            pl.BlockSpec((None, 1, D_MODEL), lambda i, b: (b * N_ADA + 5, 0, 0)),
            pl.BlockSpec((1, D_MODEL), lambda i, b: (0, 0)),
        ],
        out_specs=pl.BlockSpec((ts, D_MODEL), lambda i, b: (i, b)),
        out_shape=jax.ShapeDtypeStruct((SEQ, BATCH * D_MODEL), F32),
        compiler_params=pltpu.CompilerParams(
            dimension_semantics=("parallel", "parallel"), vmem_limit_bytes=_vmem(56)),
        name="mix_out",
    )(ymix, w_out, x1, mod3, post_g)


_HEAD_PERM_RUNS = ((0, ROPE_DIM // 2), (ROPE_DIM, LANES // 2 + ROPE_DIM // 2),
                   (ROPE_DIM // 2, ROPE_DIM), (LANES // 2 + ROPE_DIM // 2, HEAD_DIM))


def _permute_head(a, axis=-1):
    return jnp.concatenate([lax.slice_in_dim(a, lo, hi, axis=axis) for lo, hi in _HEAD_PERM_RUNS], axis=axis)


def _rope_tables():
    pos = jnp.arange(SEQ).astype(F32)
    inv_freq = ROPE_THETA ** (-jnp.arange(0, ROPE_DIM, 2, dtype=F32) / ROPE_DIM)
    ang = pos[:, None] * inv_freq[None, :]
    cos, sin = jnp.cos(ang), jnp.sin(ang)
    gap = LANES // 2 - ROPE_DIM // 2
    ones, zeros = jnp.ones((SEQ, gap), F32), jnp.zeros((SEQ, gap), F32)
    cos_t = jnp.concatenate([cos, ones, cos, ones], axis=1)
    sin_t = jnp.concatenate([-sin, zeros, sin, zeros], axis=1)
    scale = HEAD_DIM ** -0.5
    cos_all = jnp.stack([jnp.ones_like(cos_t), cos_t, cos_t * scale])
    sin_all = jnp.stack([jnp.zeros_like(sin_t), sin_t, sin_t * scale])
    return cos_all, sin_all


def _pack_w_in(w_in):
    n_main = COL_MG
    q = _permute_head(w_in[:, COL_Q:COL_KV].reshape(D_MODEL, N_HEADS, HEAD_DIM)).reshape(D_MODEL, Q_WIDTH)
    kv = w_in[:, COL_KV:COL_MG].reshape(D_MODEL, 3, 2, N_KV_GROUPS, HEAD_DIM)
    kv = jnp.stack([_permute_head(kv[:, :, 0]), kv[:, :, 1]], axis=2).reshape(D_MODEL, 6 * KV_WIDTH)
    gates = w_in[:, n_main:n_main + 3 * N_HEADS].reshape(D_MODEL, N_KV_GROUPS, 3 * HEADS_PER_GROUP)
    gates = jnp.pad(gates, ((0, 0), (0, 0), (0, LANES - 3 * HEADS_PER_GROUP))).reshape(D_MODEL, NG_PAD)
    return jnp.concatenate(
        [w_in[:, :COL_Q], q, kv, w_in[:, n_main + 3 * N_HEADS:], gates], axis=1).astype(BF16)


def _pack_compress_weights(pe_k, w1_k, w2_k, pe_v, w1_v, w2_v):
    pe_k = _permute_head(pe_k)
    w1_k = _permute_head(w1_k.reshape(CMP_BLOCK, HEAD_DIM, HEAD_DIM), axis=1).reshape(CMP_BLOCK * HEAD_DIM, HEAD_DIM)
    w2_k = _permute_head(w2_k)
    return (jnp.stack([pe_k, pe_v]), jnp.stack([w1_k, w1_v]).astype(BF16),
            jnp.stack([w2_k, w2_v]).astype(BF16))


def kernel(x, c, w_ada, b_ada, ffn1_pre_g, ffn1_post_g, ffn1_w_gate, ffn1_w_up, ffn1_w_down, mix_pre_g, mix_post_g, w_in, conv_w, conv_b, lru_wr, lru_br, lru_wi, lru_bi, lru_lambda, cmp_pe_k, cmp_w1_k, cmp_w2_k, cmp_pe_v, cmp_w1_v, cmp_w2_v, w_a_out, w_b_out, w_out, ffn2_pre_g, ffn2_post_g, ffn2_w_gate, ffn2_w_up, ffn2_w_down):
    assert x.shape == (BATCH, SEQ, D_MODEL) and w_ada.shape[0] == 1
    mod = _modulation(c, w_ada[0], b_ada)
    mod3 = mod.reshape(BATCH * N_ADA, 1, D_MODEL)

    x1 = _ffn(x, mod3, 0, ffn1_pre_g, ffn1_post_g, ffn1_w_gate[0].astype(BF16),
              ffn1_w_up[0].astype(BF16), ffn1_w_down[0].astype(BF16),
              x_time_major=False, out_time_major=True)

    cos_t, sin_t = _rope_tables()
    proj2d = _projection(x1, mod3, mix_pre_g, _pack_w_in(w_in[0]), cos_t, sin_t)

    hg = _rglru(proj2d, conv_w[0], conv_b, lru_wr[0].astype(BF16), lru_br,
                lru_wi[0].astype(BF16), lru_bi, lru_lambda)

    pe, w1, w2 = _pack_compress_weights(cmp_pe_k[0], cmp_w1_k[0], cmp_w2_k[0],
                                        cmp_pe_v[0], cmp_w1_v[0], cmp_w2_v[0])
    kcvc = _compress(proj2d, pe, w1, w2)

    ob = _attention(proj2d, kcvc)
    ymix = _merge(hg, ob, proj2d, w_a_out[0].astype(BF16), w_b_out[0].astype(BF16))
    x2 = _mix_out(ymix, w_out[0].astype(BF16), x1, mod3, mix_post_g)

    return _ffn(x2, mod3, 6, ffn2_pre_g, ffn2_post_g, ffn2_w_gate[0].astype(BF16),
                ffn2_w_up[0].astype(BF16), ffn2_w_down[0].astype(BF16),
                x_time_major=True, out_time_major=False)
```

```python
import math

import jax
import jax.numpy as jnp
import numpy as np
from jax import lax
from jax.experimental import pallas as pl
from jax.experimental.pallas import tpu as pltpu

F32 = jnp.float32
BF16 = jnp.bfloat16

D_MODEL = 2048
BATCH = 8
SEQ = 2048
D_RNN = D_MODEL
LRU_BLOCKS = 16
LRU_BLOCK_DIM = D_RNN // LRU_BLOCKS
CONV_WIDTH = 4
LRU_C = 8.0
N_HEADS = 16
HEAD_DIM = 128
N_KV_GROUPS = 4
HEADS_PER_GROUP = N_HEADS // N_KV_GROUPS
Q_WIDTH = N_HEADS * HEAD_DIM
KV_WIDTH = N_KV_GROUPS * HEAD_DIM
CMP_STRIDE = 16
CMP_BLOCK = 2 * CMP_STRIDE
SLC_BLOCK = 64
N_SELECT = 16
WINDOW = 512
ROPE_THETA = 500000.0
ROPE_DIM = HEAD_DIM // 4
D_FF = 5632
NORM_EPS = 1e-6
N_ADA = 9
N_SLC = SEQ // SLC_BLOCK
N_CMP_PAD = SEQ // CMP_STRIDE

LANES = 128
SUBLANES = 8

COL_XA = 0
COL_YA = COL_XA + D_RNN
COL_Q = COL_YA + D_RNN
COL_KV = COL_Q + Q_WIDTH
COL_MG = COL_KV + 6 * KV_WIDTH
COL_NG = COL_MG + 2 * D_MODEL
PROJ_TN = 512
NG_PAD = PROJ_TN
PROJ_W = COL_NG + NG_PAD

MASK_NEG = -1e30
M_INIT = -1e29

FFN_TS = 512
FFN_TF = 512
PROJ_TS = 1024
PROJ_MC = 256
OUT_TS = 512
OUT_MC = 256
ATT_T = 256
LRU_TS = 512
LRU_CW = 512
MOD_TN = 1024


def _vmem(mb):
    return mb * 1024 * 1024


def _rms(x, g):
    return x * lax.rsqrt(jnp.mean(x * x, axis=-1, keepdims=True) + NORM_EPS) * g


def _gelu_tanh(x):
    c = math.sqrt(2.0 / math.pi)
    return x * (0.5 * (1.0 + jnp.tanh(c * (x + 0.044715 * (x * x * x)))))


def _bdot(a, b):
    return jnp.dot(a, b, preferred_element_type=F32)


def _bdot_nt(a, b):
    return lax.dot_general(a, b, (((1,), (1,)), ((), ())), preferred_element_type=F32)


def _mod_kernel(c_ref, w_ref, b_ref, o_ref):
    c = c_ref[...]
    ca = c * jax.nn.sigmoid(c)
    o_ref[...] = _bdot(ca.astype(BF16), w_ref[...].astype(BF16)) + b_ref[...]


def _modulation(c, w_ada, b_ada):
    n = N_ADA * D_MODEL
    return pl.pallas_call(
        _mod_kernel,
        grid=(n // MOD_TN,),
        in_specs=[
            pl.BlockSpec((BATCH, D_MODEL), lambda j: (0, 0)),
            pl.BlockSpec((D_MODEL, MOD_TN), lambda j: (0, j)),
            pl.BlockSpec((1, MOD_TN), lambda j: (0, j)),
        ],
        out_specs=pl.BlockSpec((BATCH, MOD_TN), lambda j: (0, j)),
        out_shape=jax.ShapeDtypeStruct((BATCH, n), F32),
        compiler_params=pltpu.CompilerParams(
            dimension_semantics=("arbitrary",), vmem_limit_bytes=_vmem(40)),
        name="adaln_mod",
    )(c, w_ada, b_ada)


def _ffn_kernel(x_ref, sh_ref, sc_ref, gt_ref, pre_ref, post_ref, wg_ref, wu_ref, wd_ref,
                o_ref, u_scr, acc_scr):
    f = pl.program_id(2)

    @pl.when(f == 0)
    def _():
        u = _rms(x_ref[...], pre_ref[...]) * (1.0 + sc_ref[...]) + sh_ref[...]
        u_scr[...] = u.astype(BF16)
        acc_scr[...] = jnp.zeros_like(acc_scr)

    u = u_scr[...]
    gate = _bdot(u, wg_ref[...])
    up = _bdot(u, wu_ref[...])
    h = (gate * jax.nn.sigmoid(gate)) * up
    acc_scr[...] += _bdot(h.astype(BF16), wd_ref[...])

    @pl.when(f == pl.num_programs(2) - 1)
    def _():
        y = _rms(acc_scr[...], post_ref[...])
        o_ref[...] = x_ref[...] + 0.5 * gt_ref[...] * y


def _ffn(x, mod3, k_mod, pre_g, post_g, wg, wu, wd, *, x_time_major, out_time_major):
    ts, tf = FFN_TS, FFN_TF
    if x_time_major:
        x_spec = pl.BlockSpec((ts, D_MODEL), lambda i, b, f: (i, b))
    else:
        x_spec = pl.BlockSpec((None, ts, D_MODEL), lambda i, b, f: (b, i, 0))
    if out_time_major:
        o_spec = pl.BlockSpec((ts, D_MODEL), lambda i, b, f: (i, b))
        o_shape = jax.ShapeDtypeStruct((SEQ, BATCH * D_MODEL), F32)
    else:
        o_spec = pl.BlockSpec((None, ts, D_MODEL), lambda i, b, f: (b, i, 0))
        o_shape = jax.ShapeDtypeStruct((BATCH, SEQ, D_MODEL), F32)

    def mod_spec(k):
        return pl.BlockSpec((None, 1, D_MODEL), lambda i, b, f: (b * N_ADA + k, 0, 0))

    vec_spec = pl.BlockSpec((1, D_MODEL), lambda i, b, f: (0, 0))
    return pl.pallas_call(
        _ffn_kernel,
        grid=(SEQ // ts, BATCH, D_FF // tf),
        in_specs=[
            x_spec, mod_spec(k_mod), mod_spec(k_mod + 1), mod_spec(k_mod + 2), vec_spec, vec_spec,
            pl.BlockSpec((D_MODEL, tf), lambda i, b, f: (0, f)),
            pl.BlockSpec((D_MODEL, tf), lambda i, b, f: (0, f)),
            pl.BlockSpec((tf, D_MODEL), lambda i, b, f: (f, 0)),
        ],
        out_specs=o_spec,
        out_shape=o_shape,
        scratch_shapes=[pltpu.VMEM((ts, D_MODEL), BF16), pltpu.VMEM((ts, D_MODEL), F32)],
        compiler_params=pltpu.CompilerParams(
            dimension_semantics=("parallel", "parallel", "arbitrary"), vmem_limit_bytes=_vmem(56)),
        name="macaron_ffn",
    )(x, mod3, mod3, mod3, pre_g, post_g, wg, wu, wd)


def _rope_slice(x, c, s):
    return x * c + pltpu.roll(x, LANES // 2, axis=1) * s


def _proj_kernel(x_ref, sh_ref, sc_ref, pre_ref, w_ref, cos_ref, sin_ref, o_ref, u_scr):
    n = pl.program_id(2)

    @pl.when(n == 0)
    def _():
        u = _rms(x_ref[...], pre_ref[...]) * (1.0 + sc_ref[...]) + sh_ref[...]
        u_scr[...] = u.astype(BF16)

    w = w_ref[...]
    for mc in range(PROJ_TS // PROJ_MC):
        rs = slice(mc * PROJ_MC, (mc + 1) * PROJ_MC)
        r = _bdot(u_scr[rs, :], w)
        c = cos_ref[rs, :]
        s = sin_ref[rs, :]
        for hh in range(PROJ_TN // HEAD_DIM):
            sl = slice(hh * HEAD_DIM, (hh + 1) * HEAD_DIM)
            o_ref[rs, sl] = _rope_slice(r[:, sl], c, s).astype(BF16)


def _is_key_tile(n):
    is_k = n == COL_KV // PROJ_TN
    for j in range(1, 3):
        is_k = jnp.logical_or(is_k, n == (COL_KV + 2 * j * KV_WIDTH) // PROJ_TN)
    return is_k


def _is_query_tile(n):
    return jnp.logical_and(n >= COL_Q // PROJ_TN, n < COL_KV // PROJ_TN)


def _rope_kind(n):
    return jnp.where(_is_query_tile(n), 2, jnp.where(_is_key_tile(n), 1, 0))


def _projection(x1, mod3, pre_g, w_all, cos_t, sin_t):
    ts, tn = PROJ_TS, PROJ_TN
    nn = PROJ_W // tn

    def mod_spec(k):
        return pl.BlockSpec((None, 1, D_MODEL), lambda i, b, n: (b * N_ADA + k, 0, 0))

    tab_spec = pl.BlockSpec((None, ts, LANES), lambda i, b, n: (_rope_kind(n), i, 0))

    return pl.pallas_call(
        _proj_kernel,
        grid=(SEQ // ts, BATCH, nn),
        in_specs=[
            pl.BlockSpec((ts, D_MODEL), lambda i, b, n: (i, b)),
            mod_spec(3), mod_spec(4),
            pl.BlockSpec((1, D_MODEL), lambda i, b, n: (0, 0)),
            pl.BlockSpec((D_MODEL, tn), lambda i, b, n: (0, n)),
            tab_spec, tab_spec,
        ],
        out_specs=pl.BlockSpec((ts, tn), lambda i, b, n: (i, b * nn + n)),
        out_shape=jax.ShapeDtypeStruct((SEQ, BATCH * PROJ_W), BF16),
        scratch_shapes=[pltpu.VMEM((ts, D_MODEL), BF16)],
        compiler_params=pltpu.CompilerParams(
            dimension_semantics=("parallel", "parallel", "arbitrary"), vmem_limit_bytes=_vmem(48)),
        name="mix_in_proj",
    )(x1, mod3, mod3, pre_g, w_all, cos_t, sin_t)


def _lru_kernel(xa_ref, ya_ref, cw_ref, cb_ref, wr_ref, br_ref, wi_ref, bi_ref, lam_ref,
                o_ref, xe_scr, a_scr, b_scr, h_scr):
    tc = pl.program_id(2)
    ts = LRU_TS
    groups = ts // SUBLANES

    @pl.when(tc == 0)
    def _():
        xe_scr[0:SUBLANES, :] = jnp.zeros((SUBLANES, LRU_CW), F32)
        h_scr[...] = jnp.zeros_like(h_scr)

    xe_scr[SUBLANES:SUBLANES + ts, :] = xa_ref[...].astype(F32)
    cw = cw_ref[...]
    lead = SUBLANES - (CONV_WIDTH - 1)
    xc = cb_ref[...] + xe_scr[lead:lead + ts, :] * cw[0:1, :]
    for w in range(1, CONV_WIDTH):
        xc = xc + xe_scr[lead + w:lead + w + ts, :] * cw[w:w + 1, :]
    xe_scr[0:SUBLANES, :] = xe_scr[ts:ts + SUBLANES, :]

    nlam = -lam_ref[...]
    softplus = jnp.maximum(nlam, 0.0) + jnp.log1p(jnp.exp(-jnp.abs(nlam)))
    sub = lax.broadcasted_iota(jnp.int32, (groups, SUBLANES, LRU_BLOCK_DIM), 1)
    for k in range(LRU_CW // LRU_BLOCK_DIM):
        sl = slice(k * LRU_BLOCK_DIM, (k + 1) * LRU_BLOCK_DIM)
        xck = xc[:, sl]
        xb = xck.astype(BF16)
        r = jax.nn.sigmoid(_bdot(xb, wr_ref[k]) + br_ref[:, sl])
        ig = jax.nn.sigmoid(_bdot(xb, wi_ref[k]) + bi_ref[:, sl])
        log_a = (-LRU_C * r) * softplus[:, sl]
        em1 = jnp.tanh(log_a) * (jnp.exp(2.0 * log_a) + 1.0)
        a = jnp.exp(log_a).reshape(groups, SUBLANES, LRU_BLOCK_DIM)
        b = (jnp.sqrt(-em1) * (ig * xck)).reshape(groups, SUBLANES, LRU_BLOCK_DIM)
        d = 1
        while d < SUBLANES:
            a_prev = jnp.where(sub < d, 1.0, pltpu.roll(a, d, axis=1))
            b_prev = jnp.where(sub < d, 0.0, pltpu.roll(b, d, axis=1))
            b = a * b_prev + b
            a = a * a_prev
            d *= 2
        a_scr[:, sl] = a.reshape(ts, LRU_BLOCK_DIM)
        b_scr[:, sl] = b.reshape(ts, LRU_BLOCK_DIM)

    def step(g, h_prev):
        r0 = pl.multiple_of(g * SUBLANES, SUBLANES)
        h = b_scr[pl.ds(r0, SUBLANES), :] + a_scr[pl.ds(r0, SUBLANES), :] * h_prev
        b_scr[pl.ds(r0, SUBLANES), :] = h
        return jnp.broadcast_to(h[SUBLANES - 1:SUBLANES, :], (SUBLANES, LRU_CW))

    h_scr[...] = lax.fori_loop(0, groups, step, h_scr[...], unroll=4)
    o_ref[...] = (b_scr[...] * _gelu_tanh(ya_ref[...].astype(F32))).astype(BF16)


def _rglru(proj2d, conv_w, conv_b, wr, br, wi, bi, lam):
    ts = LRU_TS
    ncb = D_RNN // LRU_CW
    kb = LRU_CW // LRU_BLOCK_DIM
    pw = PROJ_W // LRU_CW
    vec = pl.BlockSpec((1, LRU_CW), lambda b, cb, tc: (0, cb))
    blk = pl.BlockSpec((kb, LRU_BLOCK_DIM, LRU_BLOCK_DIM), lambda b, cb, tc: (cb, 0, 0))
    return pl.pallas_call(
        _lru_kernel,
        grid=(BATCH, ncb, SEQ // ts),
        in_specs=[
            pl.BlockSpec((ts, LRU_CW), lambda b, cb, tc: (tc, b * pw + COL_XA // LRU_CW + cb)),
            pl.BlockSpec((ts, LRU_CW), lambda b, cb, tc: (tc, b * pw + COL_YA // LRU_CW + cb)),
            pl.BlockSpec((CONV_WIDTH, LRU_CW), lambda b, cb, tc: (0, cb)),
            vec, blk, vec, blk, vec, vec,
        ],
        out_specs=pl.BlockSpec((ts, LRU_CW), lambda b, cb, tc: (tc, b * ncb + cb)),
        out_shape=jax.ShapeDtypeStruct((SEQ, BATCH * D_RNN), BF16),
        scratch_shapes=[
            pltpu.VMEM((ts + SUBLANES, LRU_CW), F32),
            pltpu.VMEM((ts, LRU_CW), F32),
            pltpu.VMEM((ts, LRU_CW), F32),
            pltpu.VMEM((SUBLANES, LRU_CW), F32),
        ],
        compiler_params=pltpu.CompilerParams(
            dimension_semantics=("parallel", "parallel", "arbitrary"), vmem_limit_bytes=_vmem(40)),
        name="rglru",
    )(proj2d, proj2d, conv_w, conv_b, wr, br, wi, bi, lam)


def _cmp_kernel(x_ref, pe_ref, w1_ref, w2_ref, o_ref, xs_scr, cat_scr):
    xs_scr[0:SEQ, :] = x_ref[...].astype(F32)
    xs_scr[SEQ:SEQ + CMP_STRIDE, :] = jnp.zeros((CMP_STRIDE, HEAD_DIM), F32)
    pe = pe_ref[...]
    for p in range(CMP_BLOCK):
        tok = xs_scr[pl.ds(p, N_CMP_PAD, stride=CMP_STRIDE), :]
        cat_scr[:, p * HEAD_DIM:(p + 1) * HEAD_DIM] = (tok + pe[p:p + 1, :]).astype(BF16)
    pre = _bdot(cat_scr[...], w1_ref[...])
    o_ref[...] = _bdot(_gelu_tanh(pre).astype(BF16), w2_ref[...]).astype(BF16)


def _compress(proj2d, pe, w1, w2):
    pw_h = PROJ_W // HEAD_DIM
    base = COL_KV // HEAD_DIM
    return pl.pallas_call(
        _cmp_kernel,
        grid=(2, BATCH, N_KV_GROUPS),
        in_specs=[
            pl.BlockSpec((SEQ, HEAD_DIM), lambda t, b, g: (0, b * pw_h + base + t * N_KV_GROUPS + g)),
            pl.BlockSpec((None, CMP_BLOCK, HEAD_DIM), lambda t, b, g: (t, 0, 0)),
            pl.BlockSpec((None, CMP_BLOCK * HEAD_DIM, HEAD_DIM), lambda t, b, g: (t, 0, 0)),
            pl.BlockSpec((None, HEAD_DIM, HEAD_DIM), lambda t, b, g: (t, 0, 0)),
        ],
        out_specs=pl.BlockSpec((None, None, None, N_CMP_PAD, HEAD_DIM), lambda t, b, g: (t, b, g, 0, 0)),
        out_shape=jax.ShapeDtypeStruct((2, BATCH, N_KV_GROUPS, N_CMP_PAD, HEAD_DIM), BF16),
        scratch_shapes=[
            pltpu.VMEM((SEQ + CMP_STRIDE, HEAD_DIM), F32),
            pltpu.VMEM((N_CMP_PAD, CMP_BLOCK * HEAD_DIM), BF16),
        ],
        compiler_params=pltpu.CompilerParams(
            dimension_semantics=("parallel", "parallel", "parallel"), vmem_limit_bytes=_vmem(32)),
        name="kv_compress",
    )(proj2d, pe, w1, w2)


def _attn_kernel(q_ref, ks_ref, vs_ref, kw_ref, vw_ref, kc_ref, vc_ref, gt_ref, o_ref,
                 qx_scr, kx_scr, vx_scr, s0_scr, s1_scr, acc_scr, m_scr, ocmp_scr):
    t = ATT_T
    hpg = HEADS_PER_GROUP
    rows = hpg * t
    i = pl.program_id(2)
    q0 = i * t
    ext = slice(HEAD_DIM, 2 * HEAD_DIM)

    @pl.when(i == 0)
    def _():
        key_blk = jnp.right_shift(lax.broadcasted_iota(jnp.int32, (SEQ, LANES), 0), 6)
        onehot = jnp.where(key_blk == lax.broadcasted_iota(jnp.int32, (SEQ, LANES), 1), 1.0, 0.0)
        ones = jnp.ones((SEQ, HEAD_DIM), BF16)
        kx_scr[0, :, 0:HEAD_DIM] = ks_ref[...]
        kx_scr[0, :, ext] = onehot.astype(BF16)
        kx_scr[1, :, 0:HEAD_DIM] = kw_ref[...]
        kx_scr[1, :, ext] = jnp.zeros((SEQ, HEAD_DIM), BF16)
        vx_scr[0, :, 0:HEAD_DIM] = vs_ref[...]
        vx_scr[0, :, ext] = ones
        vx_scr[1, :, 0:HEAD_DIM] = vw_ref[...]
        vx_scr[1, :, ext] = ones
        qx_scr[1, :, ext] = jnp.zeros((rows, HEAD_DIM), BF16)
        qx_scr[2, :, 0:HEAD_DIM] = jnp.zeros((rows, HEAD_DIM), BF16)
        qx_scr[2, :, ext] = jnp.full((rows, HEAD_DIM), MASK_NEG, BF16)

    for hh in range(hpg):
        qh = q_ref[:, hh * HEAD_DIM:(hh + 1) * HEAD_DIM]
        qx_scr[0, hh * t:(hh + 1) * t, 0:HEAD_DIM] = qh
        qx_scr[1, hh * t:(hh + 1) * t, 0:HEAD_DIM] = qh
    q4 = qx_scr[0, :, 0:HEAD_DIM]

    row = lax.broadcasted_iota(jnp.int32, (rows, LANES), 0)
    lane = lax.broadcasted_iota(jnp.int32, (rows, LANES), 1)
    cmp_ok = (lane * CMP_STRIDE + (CMP_BLOCK - 1)) <= (q0 + jnp.bitwise_and(row, t - 1))
    s = jnp.where(cmp_ok, _bdot_nt(q4, kc_ref[...]), MASK_NEG)
    e = jnp.where(cmp_ok, jnp.exp(s - jnp.max(s, axis=1, keepdims=True)), 0.0)
    den = jnp.sum(e, axis=1, keepdims=True)
    p = e / jnp.where(den > 0.0, den, 1.0)
    ocmp_scr[...] = _bdot(p.astype(BF16), vc_ref[...])
    p_sum = p[0:t]
    for hh in range(1, hpg):
        p_sum = p_sum + p[hh * t:(hh + 1) * t]

    jj = lax.broadcasted_iota(jnp.int32, (LANES, LANES), 0)
    nn = lax.broadcasted_iota(jnp.int32, (LANES, LANES), 1)
    overlap = jnp.logical_and(
        jnp.logical_and(nn * CMP_STRIDE < (jj + 1) * SLC_BLOCK, nn * CMP_STRIDE + CMP_BLOCK > jj * SLC_BLOCK),
        jj < N_SLC)
    overlap_t = jnp.where(overlap, 1.0, 0.0).astype(BF16)
    p_hi = p_sum.astype(BF16)
    p_lo = (p_sum - p_hi.astype(F32)).astype(BF16)
    imp_t = (_bdot_nt(overlap_t, p_hi) + _bdot_nt(overlap_t, p_lo))[0:N_SLC, :]

    blk = lax.broadcasted_iota(jnp.int32, (N_SLC, t), 0)
    pos = q0 + lax.broadcasted_iota(jnp.int32, (N_SLC, t), 1)
    cur = jnp.right_shift(pos, 6)
    forced = jnp.logical_or(blk == 0, jnp.logical_or(blk == cur, blk == cur - 1))
    val = jnp.where(forced, jnp.inf, jnp.where(blk * SLC_BLOCK <= pos, imp_t, -jnp.inf))
    rank = jnp.zeros((N_SLC, t), F32)
    for c in range(N_SLC):
        vc_row = val[c:c + 1, :]
        ahead = jnp.logical_or(vc_row > val, jnp.logical_and(vc_row == val, blk > c))
        rank = rank + jnp.where(ahead, 1.0, 0.0)
    notsel_t = jnp.where(rank < float(N_SELECT), 0.0, MASK_NEG)
    notsel_t = jnp.concatenate([notsel_t, jnp.zeros((LANES - N_SLC, t), F32)], axis=0)
    notsel = notsel_t.T.astype(BF16)
    for hh in range(hpg):
        qx_scr[0, hh * t:(hh + 1) * t, ext] = notsel

    m_scr[...] = jnp.full(m_scr.shape, M_INIT, F32)
    acc_scr[...] = jnp.zeros_like(acc_scr)

    def flash_update(slot, head, s, v_tile):
        sl = slice(head * t, (head + 1) * t)
        m_prev = m_scr[slot, sl, :]
        m_new = jnp.maximum(m_prev, jnp.max(s, axis=1, keepdims=True))
        alpha = jnp.exp(m_prev - m_new)
        p = jnp.exp(s - jnp.concatenate([m_new] * (t // LANES), axis=1))
        pv = _bdot(p.astype(BF16), v_tile)
        acc_scr[slot, sl, :] = jnp.concatenate([alpha, alpha], axis=1) * acc_scr[slot, sl, :] + pv
        m_scr[slot, sl, :] = m_new

    n_int = i + jnp.minimum(i, 1)

    def job(j):
        is_win = jnp.logical_and(j == i, j < n_int)
        src = is_win.astype(jnp.int32)
        variant = jnp.where(j >= n_int, 2, src)
        kt = jnp.where(is_win, i - 1, jnp.minimum(j, i))
        return variant, src, pl.multiple_of(kt * t, t)

    def scores(j, s_scr):
        variant, src, k0 = job(j)
        s_scr[...] = _bdot_nt(qx_scr[variant], kx_scr[src, pl.ds(k0, t), :])

    def update(j, s_scr):
        _, src, k0 = job(j)
        v_tile = vx_scr[src, pl.ds(k0, t), :]
        for hh in range(hpg):
            flash_update(src, hh, s_scr[hh * t:(hh + 1) * t, :], v_tile)

    def pair(pi, carry):
        j = 2 * pi
        scores(j + 1, s1_scr)
        update(j, s0_scr)
        scores(j + 2, s0_scr)
        update(j + 1, s1_scr)
        return carry

    scores(0, s0_scr)
    lax.fori_loop(0, jnp.right_shift(n_int + 1, 1), pair, 0)

    r2 = lax.broadcasted_iota(jnp.int32, (t, t), 0)
    c2 = lax.broadcasted_iota(jnp.int32, (t, t), 1)
    causal_bias = jnp.where(c2 <= r2, 0.0, MASK_NEG)
    band_bias = jnp.where(c2 > r2, 0.0, MASK_NEG)

    def masked_tile(variant, src, k0, bias):
        s = _bdot_nt(qx_scr[variant], kx_scr[src, pl.ds(k0, t), :])
        v_tile = vx_scr[src, pl.ds(k0, t), :]
        for hh in range(hpg):
            flash_update(src, hh, s[hh * t:(hh + 1) * t, :] + bias, v_tile)

    has_far = i >= WINDOW // t
    far_src = has_far.astype(jnp.int32)
    masked_tile(jnp.where(has_far, 1, 2), far_src,
                pl.multiple_of(jnp.maximum(i - WINDOW // t, 0) * t, t), band_bias)
    masked_tile(1, 1, pl.multiple_of(q0, t), causal_bias)
    masked_tile(0, 0, pl.multiple_of(q0, t), causal_bias)

    gates = jax.nn.sigmoid(gt_ref[...].astype(F32))
    acc_s = acc_scr[0]
    acc_w = acc_scr[1]
    o_slc = acc_s[:, 0:HEAD_DIM] / acc_s[:, HEAD_DIM:2 * HEAD_DIM]
    o_win = acc_w[:, 0:HEAD_DIM] / acc_w[:, HEAD_DIM:2 * HEAD_DIM]
    o_cmp = ocmp_scr[...]
    for hh in range(hpg):
        sl = slice(hh * t, (hh + 1) * t)
        o = (gates[:, 3 * hh:3 * hh + 1] * o_cmp[sl]
             + gates[:, 3 * hh + 1:3 * hh + 2] * o_slc[sl]
             + gates[:, 3 * hh + 2:3 * hh + 3] * o_win[sl])
        o_ref[:, hh * HEAD_DIM:(hh + 1) * HEAD_DIM] = o.astype(BF16)


def _attention(proj2d, kcvc):
    t = ATT_T
    g_w = HEADS_PER_GROUP * HEAD_DIM
    rows = HEADS_PER_GROUP * t
    pw_g = PROJ_W // g_w
    pw_h = PROJ_W // HEAD_DIM

    def kv_spec(j):
        base = (COL_KV + j * KV_WIDTH) // HEAD_DIM
        return pl.BlockSpec((SEQ, HEAD_DIM), lambda b, g, i: (0, b * pw_h + base + g))

    def cmp_spec(tsel):
        return pl.BlockSpec((None, None, None, N_CMP_PAD, HEAD_DIM), lambda b, g, i: (tsel, b, g, 0, 0))

    return pl.pallas_call(
        _attn_kernel,
        grid=(BATCH, N_KV_GROUPS, SEQ // t),
        in_specs=[
            pl.BlockSpec((t, g_w), lambda b, g, i: (i, b * pw_g + COL_Q // g_w + g)),
            kv_spec(2), kv_spec(3), kv_spec(4), kv_spec(5),
            cmp_spec(0), cmp_spec(1),
            pl.BlockSpec((t, LANES), lambda b, g, i: (i, b * pw_h + COL_NG // LANES + g)),
        ],
        out_specs=pl.BlockSpec((t, g_w), lambda b, g, i: (i, b * N_KV_GROUPS + g)),
        out_shape=jax.ShapeDtypeStruct((SEQ, BATCH * Q_WIDTH), BF16),
        scratch_shapes=[
            pltpu.VMEM((3, rows, 2 * HEAD_DIM), BF16),
            pltpu.VMEM((2, SEQ, 2 * HEAD_DIM), BF16),
            pltpu.VMEM((2, SEQ, 2 * HEAD_DIM), BF16),
            pltpu.VMEM((rows, t), F32),
            pltpu.VMEM((rows, t), F32),
            pltpu.VMEM((2, rows, 2 * HEAD_DIM), F32),
            pltpu.VMEM((2, rows, LANES), F32),
            pltpu.VMEM((rows, HEAD_DIM), F32),
        ],
        compiler_params=pltpu.CompilerParams(
            dimension_semantics=("parallel", "parallel", "arbitrary"), vmem_limit_bytes=_vmem(40)),
        name="nsa_attention",
    )(proj2d, proj2d, proj2d, proj2d, proj2d, kcvc, kcvc, proj2d)


def _merge_kernel(ha_ref, ob_ref, wa_ref, wb_ref, ga_ref, gb_ref, o_ref):
    for mc in range(OUT_TS // OUT_MC):
        rs = slice(mc * OUT_MC, (mc + 1) * OUT_MC)
        ya = _bdot(ha_ref[rs, :], wa_ref[...])
        yb = _bdot(ob_ref[rs, :], wb_ref[...])
        ga = jax.nn.sigmoid(ga_ref[rs, :].astype(F32))
        gb = jax.nn.sigmoid(gb_ref[rs, :].astype(F32))
        o_ref[rs, :] = (ga * ya + gb * yb).astype(BF16)


def _merge(hg2d, ob2d, proj2d, wa, wb):
    ts, tn = OUT_TS, PROJ_TN
    nn = D_MODEL // tn
    pw = PROJ_W // tn
    return pl.pallas_call(
        _merge_kernel,
        grid=(SEQ // ts, BATCH, nn),
        in_specs=[
            pl.BlockSpec((ts, D_RNN), lambda i, b, n: (i, b)),
            pl.BlockSpec((ts, Q_WIDTH), lambda i, b, n: (i, b)),
            pl.BlockSpec((D_RNN, tn), lambda i, b, n: (0, n)),
            pl.BlockSpec((Q_WIDTH, tn), lambda i, b, n: (0, n)),
            pl.BlockSpec((ts, tn), lambda i, b, n: (i, b * pw + COL_MG // tn + n)),
            pl.BlockSpec((ts, tn), lambda i, b, n: (i, b * pw + (COL_MG + D_MODEL) // tn + n)),
        ],
        out_specs=pl.BlockSpec((ts, tn), lambda i, b, n: (i, b * nn + n)),
        out_shape=jax.ShapeDtypeStruct((SEQ, BATCH * D_MODEL), BF16),
        compiler_params=pltpu.CompilerParams(
            dimension_semantics=("parallel", "parallel", "arbitrary"), vmem_limit_bytes=_vmem(40)),
        name="branch_merge",
    )(hg2d, ob2d, wa, wb, proj2d, proj2d)


def _mixout_kernel(y_ref, w_ref, x_ref, gt_ref, post_ref, o_ref):
    mixed = _bdot(y_ref[...], w_ref[...])
    o_ref[...] = x_ref[...] + gt_ref[...] * _rms(mixed, post_ref[...])


def _mix_out(ymix, w_out, x1, mod3, post_g):
    ts = OUT_TS
    return pl.pallas_call(
        _mixout_kernel,
        grid=(SEQ // ts, BATCH),
        in_specs=[
            pl.BlockSpec((ts, D_MODEL), lambda i, b: (i, b)),
            pl.BlockSpec((D_MODEL, D_MODEL), lambda i, b: (0, 0)),
            pl.BlockSpec((ts, D_MODEL), lambda i, b: (i, b)),
            pl.BlockSpec((None, 1, D_MODEL), lambda i, b: (b * N_ADA + 5, 0, 0)),
            pl.BlockSpec((1, D_MODEL), lambda i, b: (0, 0)),
        ],
        out_specs=pl.BlockSpec((ts, D_MODEL), lambda i, b: (i, b)),
        out_shape=jax.ShapeDtypeStruct((SEQ, BATCH * D_MODEL), F32),
        compiler_params=pltpu.CompilerParams(
            dimension_semantics=("parallel", "parallel"), vmem_limit_bytes=_vmem(56)),
        name="mix_out",
    )(ymix, w_out, x1, mod3, post_g)


_HEAD_PERM_RUNS = ((0, ROPE_DIM // 2), (ROPE_DIM, LANES // 2 + ROPE_DIM // 2),
                   (ROPE_DIM // 2, ROPE_DIM), (LANES // 2 + ROPE_DIM // 2, HEAD_DIM))


def _permute_head(a, axis=-1):
    return jnp.concatenate([lax.slice_in_dim(a, lo, hi, axis=axis) for lo, hi in _HEAD_PERM_RUNS], axis=axis)


def _rope_tables():
    pos = jnp.arange(SEQ).astype(F32)
    inv_freq = ROPE_THETA ** (-jnp.arange(0, ROPE_DIM, 2, dtype=F32) / ROPE_DIM)
    ang = pos[:, None] * inv_freq[None, :]
    cos, sin = jnp.cos(ang), jnp.sin(ang)
    gap = LANES // 2 - ROPE_DIM // 2
    ones, zeros = jnp.ones((SEQ, gap), F32), jnp.zeros((SEQ, gap), F32)
    cos_t = jnp.concatenate([cos, ones, cos, ones], axis=1)
    sin_t = jnp.concatenate([-sin, zeros, sin, zeros], axis=1)
    scale = HEAD_DIM ** -0.5
    cos_all = jnp.stack([jnp.ones_like(cos_t), cos_t, cos_t * scale])
    sin_all = jnp.stack([jnp.zeros_like(sin_t), sin_t, sin_t * scale])
    return cos_all, sin_all


N_GATE_LOGITS = 3 * N_HEADS
PACK_TR = 256
PACK_WIN = PROJ_TN + LANES


def _pack_matrices():
    m = np.zeros((4, PACK_WIN, PROJ_TN), np.float32)
    cols = np.arange(PROJ_TN)
    m[0, cols, cols] = 1.0
    perm = np.concatenate([np.arange(lo, hi) for lo, hi in _HEAD_PERM_RUNS])
    m[1, (cols // HEAD_DIM) * HEAD_DIM + perm[cols % HEAD_DIM], cols] = 1.0
    m[2, cols + N_GATE_LOGITS, cols] = 1.0
    per_group = 3 * HEADS_PER_GROUP
    for g in range(N_KV_GROUPS):
        m[3, g * per_group + np.arange(per_group), g * LANES + np.arange(per_group)] = 1.0
    return jnp.asarray(m, dtype=BF16)


def _pack_kernel(a_ref, b_ref, m_ref, o_ref):
    lane = lax.broadcasted_iota(jnp.int32, (PACK_TR, LANES), 1)
    tail = jnp.where(lane < N_GATE_LOGITS, b_ref[...], 0.0)
    win = jnp.concatenate([a_ref[...].astype(BF16), tail.astype(BF16)], axis=1)
    o_ref[...] = _bdot(win, m_ref[...]).astype(BF16)


def _pack_w_in(w_in):
    n_tiles = PROJ_W // PROJ_TN
    mg_lo, mg_hi = COL_MG // PROJ_TN, COL_NG // PROJ_TN

    def kind(n):
        is_perm = jnp.logical_or(_is_query_tile(n), _is_key_tile(n))
        return jnp.where(n >= mg_hi, 3, jnp.where(n >= mg_lo, 2, jnp.where(is_perm, 1, 0)))

    def a_idx(n):
        return jnp.where(n >= mg_hi, mg_lo, n)

    def b_idx(n):
        ratio = PROJ_TN // LANES
        return jnp.where(jnp.logical_and(n >= mg_lo, n < mg_hi), (n + 1) * ratio, 0)

    return pl.pallas_call(
        _pack_kernel,
        grid=(n_tiles, D_MODEL // PACK_TR),
        in_specs=[
            pl.BlockSpec((PACK_TR, PROJ_TN), lambda n, r: (r, a_idx(n))),
            pl.BlockSpec((PACK_TR, LANES), lambda n, r: (r, b_idx(n))),
            pl.BlockSpec((None, PACK_WIN, PROJ_TN), lambda n, r: (kind(n), 0, 0)),
        ],
        out_specs=pl.BlockSpec((PACK_TR, PROJ_TN), lambda n, r: (r, n)),
        out_shape=jax.ShapeDtypeStruct((D_MODEL, PROJ_W), BF16),
        compiler_params=pltpu.CompilerParams(
            dimension_semantics=("parallel", "parallel"), vmem_limit_bytes=_vmem(16)),
        name="pack_w_in",
    )(w_in, w_in, _pack_matrices())


def _pack_compress_weights(pe_k, w1_k, w2_k, pe_v, w1_v, w2_v):
    pe_k = _permute_head(pe_k)
    w1_k = _permute_head(w1_k.reshape(CMP_BLOCK, HEAD_DIM, HEAD_DIM), axis=1).reshape(CMP_BLOCK * HEAD_DIM, HEAD_DIM)
    w2_k = _permute_head(w2_k)
    return (jnp.stack([pe_k, pe_v]), jnp.stack([w1_k, w1_v]).astype(BF16),
            jnp.stack([w2_k, w2_v]).astype(BF16))


def kernel(x, c, w_ada, b_ada, ffn1_pre_g, ffn1_post_g, ffn1_w_gate, ffn1_w_up, ffn1_w_down, mix_pre_g, mix_post_g, w_in, conv_w, conv_b, lru_wr, lru_br, lru_wi, lru_bi, lru_lambda, cmp_pe_k, cmp_w1_k, cmp_w2_k, cmp_pe_v, cmp_w1_v, cmp_w2_v, w_a_out, w_b_out, w_out, ffn2_pre_g, ffn2_post_g, ffn2_w_gate, ffn2_w_up, ffn2_w_down):
    assert x.shape == (BATCH, SEQ, D_MODEL) and w_ada.shape[0] == 1
    mod = _modulation(c, w_ada[0], b_ada)
    mod3 = mod.reshape(BATCH * N_ADA, 1, D_MODEL)

    x1 = _ffn(x, mod3, 0, ffn1_pre_g, ffn1_post_g, ffn1_w_gate[0].astype(BF16),
              ffn1_w_up[0].astype(BF16), ffn1_w_down[0].astype(BF16),
              x_time_major=False, out_time_major=True)

    cos_t, sin_t = _rope_tables()
    proj2d = _projection(x1, mod3, mix_pre_g, _pack_w_in(w_in[0]), cos_t, sin_t)

    hg = _rglru(proj2d, conv_w[0], conv_b, lru_wr[0].astype(BF16), lru_br,
                lru_wi[0].astype(BF16), lru_bi, lru_lambda)

    pe, w1, w2 = _pack_compress_weights(cmp_pe_k[0], cmp_w1_k[0], cmp_w2_k[0],
                                        cmp_pe_v[0], cmp_w1_v[0], cmp_w2_v[0])
    kcvc = _compress(proj2d, pe, w1, w2)

    ob = _attention(proj2d, kcvc)
    ymix = _merge(hg, ob, proj2d, w_a_out[0].astype(BF16), w_b_out[0].astype(BF16))
    x2 = _mix_out(ymix, w_out[0].astype(BF16), x1, mod3, mix_post_g)

    return _ffn(x2, mod3, 6, ffn2_pre_g, ffn2_post_g, ffn2_w_gate[0].astype(BF16),
                ffn2_w_up[0].astype(BF16), ffn2_w_down[0].astype(BF16),
                x_time_major=True, out_time_major=False)
```

```python
import math

import jax
import jax.numpy as jnp
import numpy as np
from jax import lax
from jax.experimental import pallas as pl
from jax.experimental.pallas import tpu as pltpu

F32 = jnp.float32
BF16 = jnp.bfloat16

D_MODEL = 2048
BATCH = 8
SEQ = 2048
D_RNN = D_MODEL
LRU_BLOCKS = 16
LRU_BLOCK_DIM = D_RNN // LRU_BLOCKS
CONV_WIDTH = 4
LRU_C = 8.0
N_HEADS = 16
HEAD_DIM = 128
N_KV_GROUPS = 4
HEADS_PER_GROUP = N_HEADS // N_KV_GROUPS
Q_WIDTH = N_HEADS * HEAD_DIM
KV_WIDTH = N_KV_GROUPS * HEAD_DIM
CMP_STRIDE = 16
CMP_BLOCK = 2 * CMP_STRIDE
SLC_BLOCK = 64
N_SELECT = 16
WINDOW = 512
ROPE_THETA = 500000.0
ROPE_DIM = HEAD_DIM // 4
D_FF = 5632
NORM_EPS = 1e-6
N_ADA = 9
N_SLC = SEQ // SLC_BLOCK
N_CMP_PAD = SEQ // CMP_STRIDE

LANES = 128
SUBLANES = 8

COL_XA = 0
COL_YA = COL_XA + D_RNN
COL_Q = COL_YA + D_RNN
COL_KV = COL_Q + Q_WIDTH
COL_MG = COL_KV + 6 * KV_WIDTH
COL_NG = COL_MG + 2 * D_MODEL
PROJ_TN = 512
NG_PAD = PROJ_TN
PROJ_W = COL_NG + NG_PAD

MASK_NEG = -1e30
M_INIT = -1e29

FFN_TS = 512
FFN_TF = 512
PROJ_TS = 1024
PROJ_MC = 128
OUT_TS = 512
OUT_MC = 256
ATT_T = 256
LRU_TS = 512
LRU_CW = 512
MOD_TN = 1024


def _vmem(mb):
    return mb * 1024 * 1024


def _rms(x, g):
    return x * lax.rsqrt(jnp.mean(x * x, axis=-1, keepdims=True) + NORM_EPS) * g


def _gelu_tanh(x):
    c = math.sqrt(2.0 / math.pi)
    return x * (0.5 * (1.0 + jnp.tanh(c * (x + 0.044715 * (x * x * x)))))


def _bdot(a, b):
    return jnp.dot(a, b, preferred_element_type=F32)


def _bdot_nt(a, b):
    return lax.dot_general(a, b, (((1,), (1,)), ((), ())), preferred_element_type=F32)


def _mod_kernel(c_ref, w_ref, b_ref, o_ref):
    c = c_ref[...]
    ca = c * jax.nn.sigmoid(c)
    o_ref[...] = _bdot(ca.astype(BF16), w_ref[...].astype(BF16)) + b_ref[...]


def _modulation(c, w_ada, b_ada):
    n = N_ADA * D_MODEL
    return pl.pallas_call(
        _mod_kernel,
        grid=(n // MOD_TN,),
        in_specs=[
            pl.BlockSpec((BATCH, D_MODEL), lambda j: (0, 0)),
            pl.BlockSpec((D_MODEL, MOD_TN), lambda j: (0, j)),
            pl.BlockSpec((1, MOD_TN), lambda j: (0, j)),
        ],
        out_specs=pl.BlockSpec((BATCH, MOD_TN), lambda j: (0, j)),
        out_shape=jax.ShapeDtypeStruct((BATCH, n), F32),
        compiler_params=pltpu.CompilerParams(
            dimension_semantics=("arbitrary",), vmem_limit_bytes=_vmem(40)),
        name="adaln_mod",
    )(c, w_ada, b_ada)


def _ffn_kernel(x_ref, sh_ref, sc_ref, gt_ref, pre_ref, post_ref, wg_ref, wu_ref, wd_ref,
                o_ref, u_scr, acc_scr):
    f = pl.program_id(2)

    @pl.when(f == 0)
    def _():
        u = _rms(x_ref[...], pre_ref[...]) * (1.0 + sc_ref[...]) + sh_ref[...]
        u_scr[...] = u.astype(BF16)
        acc_scr[...] = jnp.zeros_like(acc_scr)

    u = u_scr[...]
    gate = _bdot(u, wg_ref[...])
    up = _bdot(u, wu_ref[...])
    h = (gate * jax.nn.sigmoid(gate)) * up
    acc_scr[...] += _bdot(h.astype(BF16), wd_ref[...])

    @pl.when(f == pl.num_programs(2) - 1)
    def _():
        y = _rms(acc_scr[...], post_ref[...])
        o_ref[...] = x_ref[...] + 0.5 * gt_ref[...] * y


def _ffn(x, mod3, k_mod, pre_g, post_g, wg, wu, wd, *, x_time_major, out_time_major):
    ts, tf = FFN_TS, FFN_TF
    if x_time_major:
        x_spec = pl.BlockSpec((ts, D_MODEL), lambda i, b, f: (i, b))
    else:
        x_spec = pl.BlockSpec((None, ts, D_MODEL), lambda i, b, f: (b, i, 0))
    if out_time_major:
        o_spec = pl.BlockSpec((ts, D_MODEL), lambda i, b, f: (i, b))
        o_shape = jax.ShapeDtypeStruct((SEQ, BATCH * D_MODEL), F32)
    else:
        o_spec = pl.BlockSpec((None, ts, D_MODEL), lambda i, b, f: (b, i, 0))
        o_shape = jax.ShapeDtypeStruct((BATCH, SEQ, D_MODEL), F32)

    def mod_spec(k):
        return pl.BlockSpec((None, 1, D_MODEL), lambda i, b, f: (b * N_ADA + k, 0, 0))

    vec_spec = pl.BlockSpec((1, D_MODEL), lambda i, b, f: (0, 0))
    return pl.pallas_call(
        _ffn_kernel,
        grid=(SEQ // ts, BATCH, D_FF // tf),
        in_specs=[
            x_spec, mod_spec(k_mod), mod_spec(k_mod + 1), mod_spec(k_mod + 2), vec_spec, vec_spec,
            pl.BlockSpec((D_MODEL, tf), lambda i, b, f: (0, f)),
            pl.BlockSpec((D_MODEL, tf), lambda i, b, f: (0, f)),
            pl.BlockSpec((tf, D_MODEL), lambda i, b, f: (f, 0)),
        ],
        out_specs=o_spec,
        out_shape=o_shape,
        scratch_shapes=[pltpu.VMEM((ts, D_MODEL), BF16), pltpu.VMEM((ts, D_MODEL), F32)],
        compiler_params=pltpu.CompilerParams(
            dimension_semantics=("parallel", "parallel", "arbitrary"), vmem_limit_bytes=_vmem(56)),
        name="macaron_ffn",
    )(x, mod3, mod3, mod3, pre_g, post_g, wg, wu, wd)


def _rope_slice(x, c, s):
    return x * c + pltpu.roll(x, LANES // 2, axis=1) * s


def _proj_kernel(x_ref, sh_ref, sc_ref, pre_ref, w_ref, cos_ref, sin_ref, o_ref, u_scr):
    n = pl.program_id(2)

    @pl.when(n == 0)
    def _():
        u = _rms(x_ref[...], pre_ref[...]) * (1.0 + sc_ref[...]) + sh_ref[...]
        u_scr[...] = u.astype(BF16)

    w = w_ref[...]
    for mc in range(PROJ_TS // PROJ_MC):
        rs = slice(mc * PROJ_MC, (mc + 1) * PROJ_MC)
        r = _bdot(u_scr[rs, :], w)
        c = cos_ref[rs, :]
        s = sin_ref[rs, :]
        for hh in range(PROJ_TN // HEAD_DIM):
            sl = slice(hh * HEAD_DIM, (hh + 1) * HEAD_DIM)
            o_ref[rs, sl] = _rope_slice(r[:, sl], c, s).astype(BF16)


def _is_key_tile(n):
    is_k = n == COL_KV // PROJ_TN
    for j in range(1, 3):
        is_k = jnp.logical_or(is_k, n == (COL_KV + 2 * j * KV_WIDTH) // PROJ_TN)
    return is_k


def _is_query_tile(n):
    return jnp.logical_and(n >= COL_Q // PROJ_TN, n < COL_KV // PROJ_TN)


def _rope_kind(n):
    return jnp.where(_is_query_tile(n), 2, jnp.where(_is_key_tile(n), 1, 0))


def _projection(x1, mod3, pre_g, w_all, cos_t, sin_t):
    ts, tn = PROJ_TS, PROJ_TN
    nn = PROJ_W // tn

    def mod_spec(k):
        return pl.BlockSpec((None, 1, D_MODEL), lambda i, b, n: (b * N_ADA + k, 0, 0))

    tab_spec = pl.BlockSpec((None, ts, LANES), lambda i, b, n: (_rope_kind(n), i, 0))

    return pl.pallas_call(
        _proj_kernel,
        grid=(SEQ // ts, BATCH, nn),
        in_specs=[
            pl.BlockSpec((ts, D_MODEL), lambda i, b, n: (i, b)),
            mod_spec(3), mod_spec(4),
            pl.BlockSpec((1, D_MODEL), lambda i, b, n: (0, 0)),
            pl.BlockSpec((D_MODEL, tn), lambda i, b, n: (0, n)),
            tab_spec, tab_spec,
        ],
        out_specs=pl.BlockSpec((ts, tn), lambda i, b, n: (i, b * nn + n)),
        out_shape=jax.ShapeDtypeStruct((SEQ, BATCH * PROJ_W), BF16),
        scratch_shapes=[pltpu.VMEM((ts, D_MODEL), BF16)],
        compiler_params=pltpu.CompilerParams(
            dimension_semantics=("parallel", "parallel", "arbitrary"), vmem_limit_bytes=_vmem(48)),
        name="mix_in_proj",
    )(x1, mod3, mod3, pre_g, w_all, cos_t, sin_t)


def _lru_kernel(xa_ref, ya_ref, cw_ref, cb_ref, wr_ref, br_ref, wi_ref, bi_ref, lam_ref,
                o_ref, xe_scr, a_scr, b_scr, h_scr):
    tc = pl.program_id(2)
    ts = LRU_TS
    groups = ts // SUBLANES

    @pl.when(tc == 0)
    def _():
        xe_scr[0:SUBLANES, :] = jnp.zeros((SUBLANES, LRU_CW), F32)
        h_scr[...] = jnp.zeros_like(h_scr)

    xe_scr[SUBLANES:SUBLANES + ts, :] = xa_ref[...].astype(F32)
    cw = cw_ref[...]
    lead = SUBLANES - (CONV_WIDTH - 1)
    xc = cb_ref[...] + xe_scr[lead:lead + ts, :] * cw[0:1, :]
    for w in range(1, CONV_WIDTH):
        xc = xc + xe_scr[lead + w:lead + w + ts, :] * cw[w:w + 1, :]
    xe_scr[0:SUBLANES, :] = xe_scr[ts:ts + SUBLANES, :]

    nlam = -lam_ref[...]
    softplus = jnp.maximum(nlam, 0.0) + jnp.log1p(jnp.exp(-jnp.abs(nlam)))
    sub = lax.broadcasted_iota(jnp.int32, (groups, SUBLANES, LRU_BLOCK_DIM), 1)
    for k in range(LRU_CW // LRU_BLOCK_DIM):
        sl = slice(k * LRU_BLOCK_DIM, (k + 1) * LRU_BLOCK_DIM)
        xck = xc[:, sl]
        xb = xck.astype(BF16)
        r = jax.nn.sigmoid(_bdot(xb, wr_ref[k]) + br_ref[:, sl])
        ig = jax.nn.sigmoid(_bdot(xb, wi_ref[k]) + bi_ref[:, sl])
        log_a = (-LRU_C * r) * softplus[:, sl]
        em1 = jnp.tanh(log_a) * (jnp.exp(2.0 * log_a) + 1.0)
        a = jnp.exp(log_a).reshape(groups, SUBLANES, LRU_BLOCK_DIM)
        b = (jnp.sqrt(-em1) * (ig * xck)).reshape(groups, SUBLANES, LRU_BLOCK_DIM)
        d = 1
        while d < SUBLANES:
            a_prev = jnp.where(sub < d, 1.0, pltpu.roll(a, d, axis=1))
            b_prev = jnp.where(sub < d, 0.0, pltpu.roll(b, d, axis=1))
            b = a * b_prev + b
            a = a * a_prev
            d *= 2
        a_scr[:, sl] = a.reshape(ts, LRU_BLOCK_DIM)
        b_scr[:, sl] = b.reshape(ts, LRU_BLOCK_DIM)

    def step(g, h_prev):
        r0 = pl.multiple_of(g * SUBLANES, SUBLANES)
        h = b_scr[pl.ds(r0, SUBLANES), :] + a_scr[pl.ds(r0, SUBLANES), :] * h_prev
        b_scr[pl.ds(r0, SUBLANES), :] = h
        return jnp.broadcast_to(h[SUBLANES - 1:SUBLANES, :], (SUBLANES, LRU_CW))

    h_scr[...] = lax.fori_loop(0, groups, step, h_scr[...], unroll=4)
    o_ref[...] = (b_scr[...] * _gelu_tanh(ya_ref[...].astype(F32))).astype(BF16)


def _rglru(proj2d, conv_w, conv_b, wr, br, wi, bi, lam):
    ts = LRU_TS
    ncb = D_RNN // LRU_CW
    kb = LRU_CW // LRU_BLOCK_DIM
    pw = PROJ_W // LRU_CW
    vec = pl.BlockSpec((1, LRU_CW), lambda b, cb, tc: (0, cb))
    blk = pl.BlockSpec((kb, LRU_BLOCK_DIM, LRU_BLOCK_DIM), lambda b, cb, tc: (cb, 0, 0))
    return pl.pallas_call(
        _lru_kernel,
        grid=(BATCH, ncb, SEQ // ts),
        in_specs=[
            pl.BlockSpec((ts, LRU_CW), lambda b, cb, tc: (tc, b * pw + COL_XA // LRU_CW + cb)),
            pl.BlockSpec((ts, LRU_CW), lambda b, cb, tc: (tc, b * pw + COL_YA // LRU_CW + cb)),
            pl.BlockSpec((CONV_WIDTH, LRU_CW), lambda b, cb, tc: (0, cb)),
            vec, blk, vec, blk, vec, vec,
        ],
        out_specs=pl.BlockSpec((ts, LRU_CW), lambda b, cb, tc: (tc, b * ncb + cb)),
        out_shape=jax.ShapeDtypeStruct((SEQ, BATCH * D_RNN), BF16),
        scratch_shapes=[
            pltpu.VMEM((ts + SUBLANES, LRU_CW), F32),
            pltpu.VMEM((ts, LRU_CW), F32),
            pltpu.VMEM((ts, LRU_CW), F32),
            pltpu.VMEM((SUBLANES, LRU_CW), F32),
        ],
        compiler_params=pltpu.CompilerParams(
            dimension_semantics=("parallel", "parallel", "arbitrary"), vmem_limit_bytes=_vmem(40)),
        name="rglru",
    )(proj2d, proj2d, conv_w, conv_b, wr, br, wi, bi, lam)


def _cmp_kernel(x_ref, pe_ref, w1_ref, w2_ref, o_ref, xs_scr, cat_scr):
    xs_scr[0:SEQ, :] = x_ref[...].astype(F32)
    xs_scr[SEQ:SEQ + CMP_STRIDE, :] = jnp.zeros((CMP_STRIDE, HEAD_DIM), F32)
    pe = pe_ref[...]
    for p in range(CMP_BLOCK):
        tok = xs_scr[pl.ds(p, N_CMP_PAD, stride=CMP_STRIDE), :]
        cat_scr[:, p * HEAD_DIM:(p + 1) * HEAD_DIM] = (tok + pe[p:p + 1, :]).astype(BF16)
    pre = _bdot(cat_scr[...], w1_ref[...])
    o_ref[...] = _bdot(_gelu_tanh(pre).astype(BF16), w2_ref[...]).astype(BF16)


def _compress(proj2d, pe, w1, w2):
    pw_h = PROJ_W // HEAD_DIM
    base = COL_KV // HEAD_DIM
    return pl.pallas_call(
        _cmp_kernel,
        grid=(2, BATCH, N_KV_GROUPS),
        in_specs=[
            pl.BlockSpec((SEQ, HEAD_DIM), lambda t, b, g: (0, b * pw_h + base + t * N_KV_GROUPS + g)),
            pl.BlockSpec((None, CMP_BLOCK, HEAD_DIM), lambda t, b, g: (t, 0, 0)),
            pl.BlockSpec((None, CMP_BLOCK * HEAD_DIM, HEAD_DIM), lambda t, b, g: (t, 0, 0)),
            pl.BlockSpec((None, HEAD_DIM, HEAD_DIM), lambda t, b, g: (t, 0, 0)),
        ],
        out_specs=pl.BlockSpec((None, None, None, N_CMP_PAD, HEAD_DIM), lambda t, b, g: (t, b, g, 0, 0)),
        out_shape=jax.ShapeDtypeStruct((2, BATCH, N_KV_GROUPS, N_CMP_PAD, HEAD_DIM), BF16),
        scratch_shapes=[
            pltpu.VMEM((SEQ + CMP_STRIDE, HEAD_DIM), F32),
            pltpu.VMEM((N_CMP_PAD, CMP_BLOCK * HEAD_DIM), BF16),
        ],
        compiler_params=pltpu.CompilerParams(
            dimension_semantics=("parallel", "parallel", "parallel"), vmem_limit_bytes=_vmem(32)),
        name="kv_compress",
    )(proj2d, pe, w1, w2)


def _attn_kernel(q_ref, ks_ref, vs_ref, kw_ref, vw_ref, kc_ref, vc_ref, gt_ref, o_ref,
                 qx_scr, kx_scr, vx_scr, s0_scr, s1_scr, acc_scr, m_scr, ocmp_scr):
    t = ATT_T
    hpg = HEADS_PER_GROUP
    rows = hpg * t
    i = pl.program_id(2)
    q0 = i * t
    ext = slice(HEAD_DIM, 2 * HEAD_DIM)

    @pl.when(i == 0)
    def _():
        key_blk = jnp.right_shift(lax.broadcasted_iota(jnp.int32, (SEQ, LANES), 0), 6)
        onehot = jnp.where(key_blk == lax.broadcasted_iota(jnp.int32, (SEQ, LANES), 1), 1.0, 0.0)
        ones = jnp.ones((SEQ, HEAD_DIM), BF16)
        kx_scr[0, :, 0:HEAD_DIM] = ks_ref[...]
        kx_scr[0, :, ext] = onehot.astype(BF16)
        kx_scr[1, :, 0:HEAD_DIM] = kw_ref[...]
        kx_scr[1, :, ext] = jnp.zeros((SEQ, HEAD_DIM), BF16)
        vx_scr[0, :, 0:HEAD_DIM] = vs_ref[...]
        vx_scr[0, :, ext] = ones
        vx_scr[1, :, 0:HEAD_DIM] = vw_ref[...]
        vx_scr[1, :, ext] = ones
        qx_scr[1, :, 0:HEAD_DIM] = jnp.zeros((rows, HEAD_DIM), BF16)
        qx_scr[1, :, ext] = jnp.full((rows, HEAD_DIM), MASK_NEG, BF16)

    for hh in range(hpg):
        qx_scr[0, hh * t:(hh + 1) * t, 0:HEAD_DIM] = q_ref[:, hh * HEAD_DIM:(hh + 1) * HEAD_DIM]
    q4 = qx_scr[0, :, 0:HEAD_DIM]

    row = lax.broadcasted_iota(jnp.int32, (rows, LANES), 0)
    lane = lax.broadcasted_iota(jnp.int32, (rows, LANES), 1)
    cmp_ok = (lane * CMP_STRIDE + (CMP_BLOCK - 1)) <= (q0 + jnp.bitwise_and(row, t - 1))
    s = jnp.where(cmp_ok, _bdot_nt(q4, kc_ref[...]), MASK_NEG)
    e = jnp.where(cmp_ok, jnp.exp(s - jnp.max(s, axis=1, keepdims=True)), 0.0)
    den = jnp.sum(e, axis=1, keepdims=True)
    p = e / jnp.where(den > 0.0, den, 1.0)
    ocmp_scr[...] = _bdot(p.astype(BF16), vc_ref[...])
    p_sum = p[0:t]
    for hh in range(1, hpg):
        p_sum = p_sum + p[hh * t:(hh + 1) * t]

    jj = lax.broadcasted_iota(jnp.int32, (LANES, LANES), 0)
    nn = lax.broadcasted_iota(jnp.int32, (LANES, LANES), 1)
    overlap = jnp.logical_and(
        jnp.logical_and(nn * CMP_STRIDE < (jj + 1) * SLC_BLOCK, nn * CMP_STRIDE + CMP_BLOCK > jj * SLC_BLOCK),
        jj < N_SLC)
    overlap_t = jnp.where(overlap, 1.0, 0.0).astype(BF16)
    p_hi = p_sum.astype(BF16)
    p_lo = (p_sum - p_hi.astype(F32)).astype(BF16)
    imp_t = (_bdot_nt(overlap_t, p_hi) + _bdot_nt(overlap_t, p_lo))[0:N_SLC, :]

    blk = lax.broadcasted_iota(jnp.int32, (N_SLC, t), 0)
    pos = q0 + lax.broadcasted_iota(jnp.int32, (N_SLC, t), 1)
    cur = jnp.right_shift(pos, 6)
    forced = jnp.logical_or(blk == 0, jnp.logical_or(blk == cur, blk == cur - 1))
    val = jnp.where(forced, jnp.inf, jnp.where(blk * SLC_BLOCK <= pos, imp_t, -jnp.inf))
    rank = jnp.zeros((N_SLC, t), F32)
    for c in range(N_SLC):
        vc_row = val[c:c + 1, :]
        ahead = jnp.logical_or(vc_row > val, jnp.logical_and(vc_row == val, blk > c))
        rank = rank + jnp.where(ahead, 1.0, 0.0)
    notsel_t = jnp.where(rank < float(N_SELECT), 0.0, MASK_NEG)
    notsel_t = jnp.concatenate([notsel_t, jnp.zeros((LANES - N_SLC, t), F32)], axis=0)
    notsel = notsel_t.T.astype(BF16)
    for hh in range(hpg):
        qx_scr[0, hh * t:(hh + 1) * t, ext] = notsel

    m_scr[...] = jnp.full(m_scr.shape, M_INIT, F32)
    acc_scr[...] = jnp.zeros_like(acc_scr)

    def flash_update(slot, head, s, v_tile):
        sl = slice(head * t, (head + 1) * t)
        m_prev = m_scr[slot, sl, :]
        m_new = jnp.maximum(m_prev, jnp.max(s, axis=1, keepdims=True))
        alpha = jnp.exp(m_prev - m_new)
        p = jnp.exp(s - jnp.concatenate([m_new] * (t // LANES), axis=1))
        pv = _bdot(p.astype(BF16), v_tile)
        acc_scr[slot, sl, :] = jnp.concatenate([alpha, alpha], axis=1) * acc_scr[slot, sl, :] + pv
        m_scr[slot, sl, :] = m_new

    n_int = i + jnp.minimum(i, 1)

    def job(j):
        is_win = jnp.logical_and(j == i, j < n_int)
        src = is_win.astype(jnp.int32)
        variant = (j >= n_int).astype(jnp.int32)
        kt = jnp.where(is_win, i - 1, jnp.minimum(j, i))
        return variant, src, pl.multiple_of(kt * t, t)

    def scores(j, s_scr):
        variant, src, k0 = job(j)
        s_scr[...] = _bdot_nt(qx_scr[variant], kx_scr[src, pl.ds(k0, t), :])

    def update(j, s_scr):
        _, src, k0 = job(j)
        v_tile = vx_scr[src, pl.ds(k0, t), :]
        for hh in range(hpg):
            flash_update(src, hh, s_scr[hh * t:(hh + 1) * t, :], v_tile)

    def pair(pi, carry):
        j = 2 * pi
        scores(j + 1, s1_scr)
        update(j, s0_scr)
        scores(j + 2, s0_scr)
        update(j + 1, s1_scr)
        return carry

    scores(0, s0_scr)
    lax.fori_loop(0, jnp.right_shift(n_int + 1, 1), pair, 0)

    r2 = lax.broadcasted_iota(jnp.int32, (t, t), 0)
    c2 = lax.broadcasted_iota(jnp.int32, (t, t), 1)
    causal_bias = jnp.where(c2 <= r2, 0.0, MASK_NEG)
    band_bias = jnp.where(c2 > r2, 0.0, MASK_NEG)

    def masked_tile(variant, src, k0, bias):
        s = _bdot_nt(qx_scr[variant], kx_scr[src, pl.ds(k0, t), :])
        v_tile = vx_scr[src, pl.ds(k0, t), :]
        for hh in range(hpg):
            flash_update(src, hh, s[hh * t:(hh + 1) * t, :] + bias, v_tile)

    has_far = i >= WINDOW // t
    far_src = has_far.astype(jnp.int32)
    masked_tile(1 - far_src, far_src,
                pl.multiple_of(jnp.maximum(i - WINDOW // t, 0) * t, t), band_bias)
    masked_tile(0, 1, pl.multiple_of(q0, t), causal_bias)
    masked_tile(0, 0, pl.multiple_of(q0, t), causal_bias)

    gates = jax.nn.sigmoid(gt_ref[...].astype(F32))
    acc_s = acc_scr[0]
    acc_w = acc_scr[1]
    o_slc = acc_s[:, 0:HEAD_DIM] / acc_s[:, HEAD_DIM:2 * HEAD_DIM]
    o_win = acc_w[:, 0:HEAD_DIM] / acc_w[:, HEAD_DIM:2 * HEAD_DIM]
    o_cmp = ocmp_scr[...]
    for hh in range(hpg):
        sl = slice(hh * t, (hh + 1) * t)
        o = (gates[:, 3 * hh:3 * hh + 1] * o_cmp[sl]
             + gates[:, 3 * hh + 1:3 * hh + 2] * o_slc[sl]
             + gates[:, 3 * hh + 2:3 * hh + 3] * o_win[sl])
        o_ref[:, hh * HEAD_DIM:(hh + 1) * HEAD_DIM] = o.astype(BF16)


def _attention(proj2d, kcvc):
    t = ATT_T
    g_w = HEADS_PER_GROUP * HEAD_DIM
    rows = HEADS_PER_GROUP * t
    pw_g = PROJ_W // g_w
    pw_h = PROJ_W // HEAD_DIM

    def kv_spec(j):
        base = (COL_KV + j * KV_WIDTH) // HEAD_DIM
        return pl.BlockSpec((SEQ, HEAD_DIM), lambda b, g, i: (0, b * pw_h + base + g))

    def cmp_spec(tsel):
        return pl.BlockSpec((None, None, None, N_CMP_PAD, HEAD_DIM), lambda b, g, i: (tsel, b, g, 0, 0))

    return pl.pallas_call(
        _attn_kernel,
        grid=(BATCH, N_KV_GROUPS, SEQ // t),
        in_specs=[
            pl.BlockSpec((t, g_w), lambda b, g, i: (i, b * pw_g + COL_Q // g_w + g)),
            kv_spec(2), kv_spec(3), kv_spec(4), kv_spec(5),
            cmp_spec(0), cmp_spec(1),
            pl.BlockSpec((t, LANES), lambda b, g, i: (i, b * pw_h + COL_NG // LANES + g)),
        ],
        out_specs=pl.BlockSpec((t, g_w), lambda b, g, i: (i, b * N_KV_GROUPS + g)),
        out_shape=jax.ShapeDtypeStruct((SEQ, BATCH * Q_WIDTH), BF16),
        scratch_shapes=[
            pltpu.VMEM((2, rows, 2 * HEAD_DIM), BF16),
            pltpu.VMEM((2, SEQ, 2 * HEAD_DIM), BF16),
            pltpu.VMEM((2, SEQ, 2 * HEAD_DIM), BF16),
            pltpu.VMEM((rows, t), F32),
            pltpu.VMEM((rows, t), F32),
            pltpu.VMEM((2, rows, 2 * HEAD_DIM), F32),
            pltpu.VMEM((2, rows, LANES), F32),
            pltpu.VMEM((rows, HEAD_DIM), F32),
        ],
        compiler_params=pltpu.CompilerParams(
            dimension_semantics=("parallel", "parallel", "arbitrary"), vmem_limit_bytes=_vmem(40)),
        name="nsa_attention",
    )(proj2d, proj2d, proj2d, proj2d, proj2d, kcvc, kcvc, proj2d)


def _merge_kernel(ha_ref, ob_ref, wa_ref, wb_ref, ga_ref, gb_ref, o_ref):
    for mc in range(OUT_TS // OUT_MC):
        rs = slice(mc * OUT_MC, (mc + 1) * OUT_MC)
        ya = _bdot(ha_ref[rs, :], wa_ref[...])
        yb = _bdot(ob_ref[rs, :], wb_ref[...])
        ga = jax.nn.sigmoid(ga_ref[rs, :].astype(F32))
        gb = jax.nn.sigmoid(gb_ref[rs, :].astype(F32))
        o_ref[rs, :] = (ga * ya + gb * yb).astype(BF16)


def _merge(hg2d, ob2d, proj2d, wa, wb):
    ts, tn = OUT_TS, PROJ_TN
    nn = D_MODEL // tn
    pw = PROJ_W // tn
    return pl.pallas_call(
        _merge_kernel,
        grid=(SEQ // ts, BATCH, nn),
        in_specs=[
            pl.BlockSpec((ts, D_RNN), lambda i, b, n: (i, b)),
            pl.BlockSpec((ts, Q_WIDTH), lambda i, b, n: (i, b)),
            pl.BlockSpec((D_RNN, tn), lambda i, b, n: (0, n)),
            pl.BlockSpec((Q_WIDTH, tn), lambda i, b, n: (0, n)),
            pl.BlockSpec((ts, tn), lambda i, b, n: (i, b * pw + COL_MG // tn + n)),
            pl.BlockSpec((ts, tn), lambda i, b, n: (i, b * pw + (COL_MG + D_MODEL) // tn + n)),
        ],
        out_specs=pl.BlockSpec((ts, tn), lambda i, b, n: (i, b * nn + n)),
        out_shape=jax.ShapeDtypeStruct((SEQ, BATCH * D_MODEL), BF16),
        compiler_params=pltpu.CompilerParams(
            dimension_semantics=("parallel", "parallel", "arbitrary"), vmem_limit_bytes=_vmem(40)),
        name="branch_merge",
    )(hg2d, ob2d, wa, wb, proj2d, proj2d)


def _mixout_kernel(y_ref, w_ref, x_ref, gt_ref, post_ref, o_ref):
    mixed = _bdot(y_ref[...], w_ref[...])
    o_ref[...] = x_ref[...] + gt_ref[...] * _rms(mixed, post_ref[...])


def _mix_out(ymix, w_out, x1, mod3, post_g):
    ts = OUT_TS
    return pl.pallas_call(
        _mixout_kernel,
        grid=(SEQ // ts, BATCH),
        in_specs=[
            pl.BlockSpec((ts, D_MODEL), lambda i, b: (i, b)),
            pl.BlockSpec((D_MODEL, D_MODEL), lambda i, b: (0, 0)),
            pl.BlockSpec((ts, D_MODEL), lambda i, b: (i, b)),
            pl.BlockSpec((None, 1, D_MODEL), lambda i, b: (b * N_ADA + 5, 0, 0)),
            pl.BlockSpec((1, D_MODEL), lambda i, b: (0, 0)),
        ],
        out_specs=pl.BlockSpec((ts, D_MODEL), lambda i, b: (i, b)),
        out_shape=jax.ShapeDtypeStruct((SEQ, BATCH * D_MODEL), F32),
        compiler_params=pltpu.CompilerParams(
            dimension_semantics=("parallel", "parallel"), vmem_limit_bytes=_vmem(56)),
        name="mix_out",
    )(ymix, w_out, x1, mod3, post_g)


_HEAD_PERM_RUNS = ((0, ROPE_DIM // 2), (ROPE_DIM, LANES // 2 + ROPE_DIM // 2),
                   (ROPE_DIM // 2, ROPE_DIM), (LANES // 2 + ROPE_DIM // 2, HEAD_DIM))


def _permute_head(a, axis=-1):
    return jnp.concatenate([lax.slice_in_dim(a, lo, hi, axis=axis) for lo, hi in _HEAD_PERM_RUNS], axis=axis)


def _rope_tables():
    pos = jnp.arange(SEQ).astype(F32)
    inv_freq = ROPE_THETA ** (-jnp.arange(0, ROPE_DIM, 2, dtype=F32) / ROPE_DIM)
    ang = pos[:, None] * inv_freq[None, :]
    cos, sin = jnp.cos(ang), jnp.sin(ang)
    gap = LANES // 2 - ROPE_DIM // 2
    ones, zeros = jnp.ones((SEQ, gap), F32), jnp.zeros((SEQ, gap), F32)
    cos_t = jnp.concatenate([cos, ones, cos, ones], axis=1)
    sin_t = jnp.concatenate([-sin, zeros, sin, zeros], axis=1)
    scale = HEAD_DIM ** -0.5
    cos_all = jnp.stack([jnp.ones_like(cos_t), cos_t, cos_t * scale])
    sin_all = jnp.stack([jnp.zeros_like(sin_t), sin_t, sin_t * scale])
    return cos_all, sin_all


N_GATE_LOGITS = 3 * N_HEADS
PACK_TR = 1024
PACK_WIN = PROJ_TN + LANES


def _pack_matrices():
    m = np.zeros((4, PACK_WIN, PROJ_TN), np.float32)
    cols = np.arange(PROJ_TN)
    m[0, cols, cols] = 1.0
    perm = np.concatenate([np.arange(lo, hi) for lo, hi in _HEAD_PERM_RUNS])
    m[1, (cols // HEAD_DIM) * HEAD_DIM + perm[cols % HEAD_DIM], cols] = 1.0
    m[2, cols + N_GATE_LOGITS, cols] = 1.0
    per_group = 3 * HEADS_PER_GROUP
    for g in range(N_KV_GROUPS):
        m[3, g * per_group + np.arange(per_group), g * LANES + np.arange(per_group)] = 1.0
    return jnp.asarray(m, dtype=BF16)


def _pack_kernel(a_ref, b_ref, m_ref, o_ref):
    lane = lax.broadcasted_iota(jnp.int32, (PACK_TR, LANES), 1)
    tail = jnp.where(lane < N_GATE_LOGITS, b_ref[...], 0.0)
    win = jnp.concatenate([a_ref[...].astype(BF16), tail.astype(BF16)], axis=1)
    o_ref[...] = _bdot(win, m_ref[...]).astype(BF16)


def _pack_w_in(w_in):
    n_tiles = PROJ_W // PROJ_TN
    mg_lo, mg_hi = COL_MG // PROJ_TN, COL_NG // PROJ_TN

    def kind(n):
        is_perm = jnp.logical_or(_is_query_tile(n), _is_key_tile(n))
        return jnp.where(n >= mg_hi, 3, jnp.where(n >= mg_lo, 2, jnp.where(is_perm, 1, 0)))

    def a_idx(n):
        return jnp.where(n >= mg_hi, mg_lo, n)

    def b_idx(n):
        ratio = PROJ_TN // LANES
        return jnp.where(jnp.logical_and(n >= mg_lo, n < mg_hi), (n + 1) * ratio, 0)

    return pl.pallas_call(
        _pack_kernel,
        grid=(n_tiles, D_MODEL // PACK_TR),
        in_specs=[
            pl.BlockSpec((None, PACK_TR, PROJ_TN), lambda n, r: (0, r, a_idx(n))),
            pl.BlockSpec((None, PACK_TR, LANES), lambda n, r: (0, r, b_idx(n))),
            pl.BlockSpec((None, PACK_WIN, PROJ_TN), lambda n, r: (kind(n), 0, 0)),
        ],
        out_specs=pl.BlockSpec((PACK_TR, PROJ_TN), lambda n, r: (r, n)),
        out_shape=jax.ShapeDtypeStruct((D_MODEL, PROJ_W), BF16),
        compiler_params=pltpu.CompilerParams(
            dimension_semantics=("parallel", "parallel"), vmem_limit_bytes=_vmem(32)),
        name="pack_w_in",
    )(w_in, w_in, _pack_matrices())


def _pack_compress_weights(pe_k, w1_k, w2_k, pe_v, w1_v, w2_v):
    pe_k = _permute_head(pe_k)
    w1_k = _permute_head(w1_k.reshape(CMP_BLOCK, HEAD_DIM, HEAD_DIM), axis=1).reshape(CMP_BLOCK * HEAD_DIM, HEAD_DIM)
    w2_k = _permute_head(w2_k)
    return (jnp.stack([pe_k, pe_v]), jnp.stack([w1_k, w1_v]).astype(BF16),
            jnp.stack([w2_k, w2_v]).astype(BF16))


def kernel(x, c, w_ada, b_ada, ffn1_pre_g, ffn1_post_g, ffn1_w_gate, ffn1_w_up, ffn1_w_down, mix_pre_g, mix_post_g, w_in, conv_w, conv_b, lru_wr, lru_br, lru_wi, lru_bi, lru_lambda, cmp_pe_k, cmp_w1_k, cmp_w2_k, cmp_pe_v, cmp_w1_v, cmp_w2_v, w_a_out, w_b_out, w_out, ffn2_pre_g, ffn2_post_g, ffn2_w_gate, ffn2_w_up, ffn2_w_down):
    assert x.shape == (BATCH, SEQ, D_MODEL) and w_ada.shape[0] == 1
    mod = _modulation(c, w_ada[0], b_ada)
    mod3 = mod.reshape(BATCH * N_ADA, 1, D_MODEL)

    x1 = _ffn(x, mod3, 0, ffn1_pre_g, ffn1_post_g, ffn1_w_gate[0].astype(BF16),
              ffn1_w_up[0].astype(BF16), ffn1_w_down[0].astype(BF16),
              x_time_major=False, out_time_major=True)

    cos_t, sin_t = _rope_tables()
    proj2d = _projection(x1, mod3, mix_pre_g, _pack_w_in(w_in), cos_t, sin_t)

    hg = _rglru(proj2d, conv_w[0], conv_b, lru_wr[0].astype(BF16), lru_br,
                lru_wi[0].astype(BF16), lru_bi, lru_lambda)

    pe, w1, w2 = _pack_compress_weights(cmp_pe_k[0], cmp_w1_k[0], cmp_w2_k[0],
                                        cmp_pe_v[0], cmp_w1_v[0], cmp_w2_v[0])
    kcvc = _compress(proj2d, pe, w1, w2)

    ob = _attention(proj2d, kcvc)
    ymix = _merge(hg, ob, proj2d, w_a_out[0].astype(BF16), w_b_out[0].astype(BF16))
    x2 = _mix_out(ymix, w_out[0].astype(BF16), x1, mod3, mix_post_g)

    return _ffn(x2, mod3, 6, ffn2_pre_g, ffn2_post_g, ffn2_w_gate[0].astype(BF16),
                ffn2_w_up[0].astype(BF16), ffn2_w_down[0].astype(BF16),
                x_time_major=True, out_time_major=False)
```

```python
import math

import jax
import jax.numpy as jnp
import numpy as np
from jax import lax
from jax.experimental import pallas as pl
from jax.experimental.pallas import tpu as pltpu

F32 = jnp.float32
BF16 = jnp.bfloat16

D_MODEL = 2048
BATCH = 8
SEQ = 2048
D_RNN = D_MODEL
LRU_BLOCKS = 16
LRU_BLOCK_DIM = D_RNN // LRU_BLOCKS
CONV_WIDTH = 4
LRU_C = 8.0
N_HEADS = 16
HEAD_DIM = 128
N_KV_GROUPS = 4
HEADS_PER_GROUP = N_HEADS // N_KV_GROUPS
Q_WIDTH = N_HEADS * HEAD_DIM
KV_WIDTH = N_KV_GROUPS * HEAD_DIM
CMP_STRIDE = 16
CMP_BLOCK = 2 * CMP_STRIDE
SLC_BLOCK = 64
N_SELECT = 16
WINDOW = 512
ROPE_THETA = 500000.0
ROPE_DIM = HEAD_DIM // 4
D_FF = 5632
NORM_EPS = 1e-6
N_ADA = 9
N_SLC = SEQ // SLC_BLOCK
N_CMP_PAD = SEQ // CMP_STRIDE

LANES = 128
SUBLANES = 8
BF16_ROWS = 16

COL_XA = 0
COL_YA = COL_XA + D_RNN
COL_Q = COL_YA + D_RNN
COL_KV = COL_Q + Q_WIDTH
COL_MG = COL_KV + 6 * KV_WIDTH
COL_NG = COL_MG + 2 * D_MODEL
PROJ_TN = 512
NG_PAD = PROJ_TN
PROJ_W = COL_NG + NG_PAD

MASK_NEG = -1e30
M_INIT = -1e29

FFN_TS = 512
FFN_TF = 512
FFN_PRE_ROWS = 48
PROJ_TS = 1024
PROJ_MC = 128
OUT_TS = 512
OUT_MC = 256
ATT_T = 256
LRU_TS = 512
LRU_CW = 512
MOD_TN = 1024


def _vmem(mb):
    return mb * 1024 * 1024


def _rms(x, g):
    return x * lax.rsqrt(jnp.mean(x * x, axis=-1, keepdims=True) + NORM_EPS) * g


def _gelu_tanh(x):
    c = math.sqrt(2.0 / math.pi)
    return x * (0.5 * (1.0 + jnp.tanh(c * (x + 0.044715 * (x * x * x)))))


def _bdot(a, b):
    return jnp.dot(a, b, preferred_element_type=F32)


def _bdot_nt(a, b):
    return lax.dot_general(a, b, (((1,), (1,)), ((), ())), preferred_element_type=F32)


def _mod_kernel(c_ref, w_ref, b_ref, o_ref):
    c = c_ref[...]
    ca = c * jax.nn.sigmoid(c)
    o_ref[...] = _bdot(ca.astype(BF16), w_ref[...].astype(BF16)) + b_ref[...]


def _modulation(c, w_ada, b_ada):
    n = N_ADA * D_MODEL
    return pl.pallas_call(
        _mod_kernel,
        grid=(n // MOD_TN,),
        in_specs=[
            pl.BlockSpec((BATCH, D_MODEL), lambda j: (0, 0)),
            pl.BlockSpec((None, D_MODEL, MOD_TN), lambda j: (0, 0, j)),
            pl.BlockSpec((1, MOD_TN), lambda j: (0, j)),
        ],
        out_specs=pl.BlockSpec((BATCH, MOD_TN), lambda j: (0, j)),
        out_shape=jax.ShapeDtypeStruct((BATCH, n), F32),
        compiler_params=pltpu.CompilerParams(
            dimension_semantics=("arbitrary",), vmem_limit_bytes=_vmem(40)),
        name="adaln_mod",
    )(c, w_ada, b_ada)


def _norm_next_rows(step, rows, tile_rows, xn_ref, pre_ref, scn_ref, shn_ref, u_next):
    r0 = pl.multiple_of(jnp.minimum(step * rows, tile_rows - rows), BF16_ROWS)
    un = _rms(xn_ref[pl.ds(r0, rows), :], pre_ref[...]) * (1.0 + scn_ref[...]) + shn_ref[...]
    u_next[pl.ds(r0, rows), :] = un.astype(BF16)


def _ffn_kernel(x_ref, xn_ref, sh_ref, sc_ref, gt_ref, shn_ref, scn_ref, pre_ref, post_ref,
                wg_ref, wu_ref, wd_ref, o_ref, u_scr, acc_scr):
    f = pl.program_id(2)
    tile = pl.program_id(0) * pl.num_programs(1) + pl.program_id(1)
    cur = jnp.bitwise_and(tile, 1)

    @pl.when(jnp.logical_and(tile == 0, f == 0))
    def _():
        u = _rms(x_ref[...], pre_ref[...]) * (1.0 + sc_ref[...]) + sh_ref[...]
        u_scr[0] = u.astype(BF16)

    @pl.when(f == 0)
    def _():
        acc_scr[...] = jnp.zeros_like(acc_scr)

    u = u_scr[cur]
    gate = _bdot(u, wg_ref[...])
    up = _bdot(u, wu_ref[...])
    h = (gate * jax.nn.sigmoid(gate)) * up
    acc_scr[...] += _bdot(h.astype(BF16), wd_ref[...])
    _norm_next_rows(f, FFN_PRE_ROWS, FFN_TS, xn_ref, pre_ref, scn_ref, shn_ref, u_scr.at[1 - cur])

    @pl.when(f == pl.num_programs(2) - 1)
    def _():
        y = _rms(acc_scr[...], post_ref[...])
        o_ref[...] = x_ref[...] + 0.5 * gt_ref[...] * y


def _next_tile(i, b, tile_rows):
    nxt = jnp.minimum(i * BATCH + b + 1, (SEQ // tile_rows) * BATCH - 1)
    return nxt // BATCH, lax.rem(nxt, BATCH)


def _ffn(x, mod3, k_mod, pre_g, post_g, wg, wu, wd, *, x_time_major, out_time_major):
    ts, tf = FFN_TS, FFN_TF
    assert (D_FF // tf) * FFN_PRE_ROWS >= ts

    def tm_spec(idx):
        return pl.BlockSpec((ts, D_MODEL), lambda i, b, f: idx(i, b))

    def bm_spec(idx):
        return pl.BlockSpec((None, ts, D_MODEL), lambda i, b, f: idx(i, b)[::-1] + (0,))

    here = lambda i, b: (i, b)
    nxt = lambda i, b: _next_tile(i, b, ts)
    x_spec, xn_spec = [(tm_spec if x_time_major else bm_spec)(idx) for idx in (here, nxt)]
    if out_time_major:
        o_spec, o_shape = tm_spec(here), jax.ShapeDtypeStruct((SEQ, BATCH * D_MODEL), F32)
    else:
        o_spec, o_shape = bm_spec(here), jax.ShapeDtypeStruct((BATCH, SEQ, D_MODEL), F32)

    def mod_spec(k, idx=here):
        return pl.BlockSpec((None, 1, D_MODEL), lambda i, b, f: (idx(i, b)[1] * N_ADA + k, 0, 0))

    vec_spec = pl.BlockSpec((1, D_MODEL), lambda i, b, f: (0, 0))
    return pl.pallas_call(
        _ffn_kernel,
        grid=(SEQ // ts, BATCH, D_FF // tf),
        in_specs=[
            x_spec, xn_spec, mod_spec(k_mod), mod_spec(k_mod + 1), mod_spec(k_mod + 2),
            mod_spec(k_mod, nxt), mod_spec(k_mod + 1, nxt), vec_spec, vec_spec,
            pl.BlockSpec((D_MODEL, tf), lambda i, b, f: (0, f)),
            pl.BlockSpec((D_MODEL, tf), lambda i, b, f: (0, f)),
            pl.BlockSpec((tf, D_MODEL), lambda i, b, f: (f, 0)),
        ],
        out_specs=o_spec,
        out_shape=o_shape,
        scratch_shapes=[pltpu.VMEM((2, ts, D_MODEL), BF16), pltpu.VMEM((ts, D_MODEL), F32)],
        compiler_params=pltpu.CompilerParams(
            dimension_semantics=("arbitrary", "arbitrary", "arbitrary"), vmem_limit_bytes=_vmem(58)),
        name="macaron_ffn",
    )(x, x, mod3, mod3, mod3, mod3, mod3, pre_g, post_g, wg, wu, wd)


def _rope_slice(x, c, s):
    return x * c + pltpu.roll(x, LANES // 2, axis=1) * s


def _proj_kernel(x_ref, sh_ref, sc_ref, pre_ref, w_ref, cos_ref, sin_ref, o_ref, u_scr):
    n = pl.program_id(2)

    @pl.when(n == 0)
    def _():
        u = _rms(x_ref[...], pre_ref[...]) * (1.0 + sc_ref[...]) + sh_ref[...]
        u_scr[...] = u.astype(BF16)

    w = w_ref[...]
    for mc in range(PROJ_TS // PROJ_MC):
        rs = slice(mc * PROJ_MC, (mc + 1) * PROJ_MC)
        r = _bdot(u_scr[rs, :], w)
        c = cos_ref[rs, :]
        s = sin_ref[rs, :]
        for hh in range(PROJ_TN // HEAD_DIM):
            sl = slice(hh * HEAD_DIM, (hh + 1) * HEAD_DIM)
            o_ref[rs, sl] = _rope_slice(r[:, sl], c, s).astype(BF16)


def _is_key_tile(n):
    is_k = n == COL_KV // PROJ_TN
    for j in range(1, 3):
        is_k = jnp.logical_or(is_k, n == (COL_KV + 2 * j * KV_WIDTH) // PROJ_TN)
    return is_k


def _is_query_tile(n):
    return jnp.logical_and(n >= COL_Q // PROJ_TN, n < COL_KV // PROJ_TN)


def _rope_kind(n):
    return jnp.where(_is_query_tile(n), 2, jnp.where(_is_key_tile(n), 1, 0))


def _projection(x1, mod3, pre_g, w_all, cos_t, sin_t):
    ts, tn = PROJ_TS, PROJ_TN
    nn = PROJ_W // tn

    def mod_spec(k):
        return pl.BlockSpec((None, 1, D_MODEL), lambda i, b, n: (b * N_ADA + k, 0, 0))

    tab_spec = pl.BlockSpec((None, ts, LANES), lambda i, b, n: (_rope_kind(n), i, 0))

    return pl.pallas_call(
        _proj_kernel,
        grid=(SEQ // ts, BATCH, nn),
        in_specs=[
            pl.BlockSpec((ts, D_MODEL), lambda i, b, n: (i, b)),
            mod_spec(3), mod_spec(4),
            pl.BlockSpec((1, D_MODEL), lambda i, b, n: (0, 0)),
            pl.BlockSpec((D_MODEL, tn), lambda i, b, n: (0, n)),
            tab_spec, tab_spec,
        ],
        out_specs=pl.BlockSpec((ts, tn), lambda i, b, n: (i, b * nn + n)),
        out_shape=jax.ShapeDtypeStruct((SEQ, BATCH * PROJ_W), BF16),
        scratch_shapes=[pltpu.VMEM((ts, D_MODEL), BF16)],
        compiler_params=pltpu.CompilerParams(
            dimension_semantics=("parallel", "parallel", "arbitrary"), vmem_limit_bytes=_vmem(48)),
        name="mix_in_proj",
    )(x1, mod3, mod3, pre_g, w_all, cos_t, sin_t)


def _lru_kernel(xa_ref, ya_ref, cw_ref, cb_ref, wr_ref, br_ref, wi_ref, bi_ref, lam_ref,
                o_ref, xe_scr, a_scr, b_scr, h_scr):
    tc = pl.program_id(2)
    ts = LRU_TS
    groups = ts // SUBLANES

    @pl.when(tc == 0)
    def _():
        xe_scr[0:SUBLANES, :] = jnp.zeros((SUBLANES, LRU_CW), F32)
        h_scr[...] = jnp.zeros_like(h_scr)

    xe_scr[SUBLANES:SUBLANES + ts, :] = xa_ref[...].astype(F32)
    cw = cw_ref[...]
    lead = SUBLANES - (CONV_WIDTH - 1)
    xc = cb_ref[...] + xe_scr[lead:lead + ts, :] * cw[0:1, :]
    for w in range(1, CONV_WIDTH):
        xc = xc + xe_scr[lead + w:lead + w + ts, :] * cw[w:w + 1, :]
    xe_scr[0:SUBLANES, :] = xe_scr[ts:ts + SUBLANES, :]

    nlam = -lam_ref[...]
    softplus = jnp.maximum(nlam, 0.0) + jnp.log1p(jnp.exp(-jnp.abs(nlam)))
    sub = lax.broadcasted_iota(jnp.int32, (groups, SUBLANES, LRU_BLOCK_DIM), 1)
    for k in range(LRU_CW // LRU_BLOCK_DIM):
        sl = slice(k * LRU_BLOCK_DIM, (k + 1) * LRU_BLOCK_DIM)
        xck = xc[:, sl]
        xb = xck.astype(BF16)
        r = jax.nn.sigmoid(_bdot(xb, wr_ref[k]) + br_ref[:, sl])
        ig = jax.nn.sigmoid(_bdot(xb, wi_ref[k]) + bi_ref[:, sl])
        log_a = (-LRU_C * r) * softplus[:, sl]
        em1 = jnp.tanh(log_a) * (jnp.exp(2.0 * log_a) + 1.0)
        a = jnp.exp(log_a).reshape(groups, SUBLANES, LRU_BLOCK_DIM)
        b = (jnp.sqrt(-em1) * (ig * xck)).reshape(groups, SUBLANES, LRU_BLOCK_DIM)
        d = 1
        while d < SUBLANES:
            a_prev = jnp.where(sub < d, 1.0, pltpu.roll(a, d, axis=1))
            b_prev = jnp.where(sub < d, 0.0, pltpu.roll(b, d, axis=1))
            b = a * b_prev + b
            a = a * a_prev
            d *= 2
        a_scr[:, sl] = a.reshape(ts, LRU_BLOCK_DIM)
        b_scr[:, sl] = b.reshape(ts, LRU_BLOCK_DIM)

    def step(g, h_prev):
        r0 = pl.multiple_of(g * SUBLANES, SUBLANES)
        h = b_scr[pl.ds(r0, SUBLANES), :] + a_scr[pl.ds(r0, SUBLANES), :] * h_prev
        b_scr[pl.ds(r0, SUBLANES), :] = h
        return jnp.broadcast_to(h[SUBLANES - 1:SUBLANES, :], (SUBLANES, LRU_CW))

    h_scr[...] = lax.fori_loop(0, groups, step, h_scr[...], unroll=4)
    o_ref[...] = (b_scr[...] * _gelu_tanh(ya_ref[...].astype(F32))).astype(BF16)


def _rglru(proj2d, conv_w, conv_b, wr, br, wi, bi, lam):
    ts = LRU_TS
    ncb = D_RNN // LRU_CW
    kb = LRU_CW // LRU_BLOCK_DIM
    pw = PROJ_W // LRU_CW
    vec = pl.BlockSpec((1, LRU_CW), lambda b, cb, tc: (0, cb))
    blk = pl.BlockSpec((kb, LRU_BLOCK_DIM, LRU_BLOCK_DIM), lambda b, cb, tc: (cb, 0, 0))
    return pl.pallas_call(
        _lru_kernel,
        grid=(BATCH, ncb, SEQ // ts),
        in_specs=[
            pl.BlockSpec((ts, LRU_CW), lambda b, cb, tc: (tc, b * pw + COL_XA // LRU_CW + cb)),
            pl.BlockSpec((ts, LRU_CW), lambda b, cb, tc: (tc, b * pw + COL_YA // LRU_CW + cb)),
            pl.BlockSpec((CONV_WIDTH, LRU_CW), lambda b, cb, tc: (0, cb)),
            vec, blk, vec, blk, vec, vec,
        ],
        out_specs=pl.BlockSpec((ts, LRU_CW), lambda b, cb, tc: (tc, b * ncb + cb)),
        out_shape=jax.ShapeDtypeStruct((SEQ, BATCH * D_RNN), BF16),
        scratch_shapes=[
            pltpu.VMEM((ts + SUBLANES, LRU_CW), F32),
            pltpu.VMEM((ts, LRU_CW), F32),
            pltpu.VMEM((ts, LRU_CW), F32),
            pltpu.VMEM((SUBLANES, LRU_CW), F32),
        ],
        compiler_params=pltpu.CompilerParams(
            dimension_semantics=("parallel", "parallel", "arbitrary"), vmem_limit_bytes=_vmem(40)),
        name="rglru",
    )(proj2d, proj2d, conv_w, conv_b, wr, br, wi, bi, lam)


def _cmp_kernel(x_ref, pe_ref, w1_ref, w2_ref, o_ref, xs_scr, cat_scr):
    xs_scr[0:SEQ, :] = x_ref[...].astype(F32)
    xs_scr[SEQ:SEQ + CMP_STRIDE, :] = jnp.zeros((CMP_STRIDE, HEAD_DIM), F32)
    pe = pe_ref[...]
    for p in range(CMP_BLOCK):
        tok = xs_scr[pl.ds(p, N_CMP_PAD, stride=CMP_STRIDE), :]
        cat_scr[:, p * HEAD_DIM:(p + 1) * HEAD_DIM] = (tok + pe[p:p + 1, :]).astype(BF16)
    pre = _bdot(cat_scr[...], w1_ref[...])
    o_ref[...] = _bdot(_gelu_tanh(pre).astype(BF16), w2_ref[...]).astype(BF16)


def _compress(proj2d, pe, w1, w2):
    pw_h = PROJ_W // HEAD_DIM
    base = COL_KV // HEAD_DIM
    return pl.pallas_call(
        _cmp_kernel,
        grid=(2, BATCH, N_KV_GROUPS),
        in_specs=[
            pl.BlockSpec((SEQ, HEAD_DIM), lambda t, b, g: (0, b * pw_h + base + t * N_KV_GROUPS + g)),
            pl.BlockSpec((None, CMP_BLOCK, HEAD_DIM), lambda t, b, g: (t, 0, 0)),
            pl.BlockSpec((None, CMP_BLOCK * HEAD_DIM, HEAD_DIM), lambda t, b, g: (t, 0, 0)),
            pl.BlockSpec((None, HEAD_DIM, HEAD_DIM), lambda t, b, g: (t, 0, 0)),
        ],
        out_specs=pl.BlockSpec((None, None, None, N_CMP_PAD, HEAD_DIM), lambda t, b, g: (t, b, g, 0, 0)),
        out_shape=jax.ShapeDtypeStruct((2, BATCH, N_KV_GROUPS, N_CMP_PAD, HEAD_DIM), BF16),
        scratch_shapes=[
            pltpu.VMEM((SEQ + CMP_STRIDE, HEAD_DIM), F32),
            pltpu.VMEM((N_CMP_PAD, CMP_BLOCK * HEAD_DIM), BF16),
        ],
        compiler_params=pltpu.CompilerParams(
            dimension_semantics=("parallel", "parallel", "parallel"), vmem_limit_bytes=_vmem(32)),
        name="kv_compress",
    )(proj2d, pe, w1, w2)


def _attn_kernel(q_ref, ks_ref, vs_ref, kw_ref, vw_ref, kc_ref, vc_ref, gt_ref, o_ref,
                 qx_scr, kx_scr, vx_scr, s0_scr, s1_scr, acc_scr, m_scr, ocmp_scr):
    t = ATT_T
    hpg = HEADS_PER_GROUP
    rows = hpg * t
    i = pl.program_id(2)
    q0 = i * t
    ext = slice(HEAD_DIM, 2 * HEAD_DIM)

    @pl.when(i == 0)
    def _():
        key_blk = jnp.right_shift(lax.broadcasted_iota(jnp.int32, (SEQ, LANES), 0), 6)
        onehot = jnp.where(key_blk == lax.broadcasted_iota(jnp.int32, (SEQ, LANES), 1), 1.0, 0.0)
        ones = jnp.ones((SEQ, HEAD_DIM), BF16)
        kx_scr[0, :, 0:HEAD_DIM] = ks_ref[...]
        kx_scr[0, :, ext] = onehot.astype(BF16)
        kx_scr[1, :, 0:HEAD_DIM] = kw_ref[...]
        kx_scr[1, :, ext] = jnp.zeros((SEQ, HEAD_DIM), BF16)
        vx_scr[0, :, 0:HEAD_DIM] = vs_ref[...]
        vx_scr[0, :, ext] = ones
        vx_scr[1, :, 0:HEAD_DIM] = vw_ref[...]
        vx_scr[1, :, ext] = ones
        qx_scr[1, :, 0:HEAD_DIM] = jnp.zeros((rows, HEAD_DIM), BF16)
        qx_scr[1, :, ext] = jnp.full((rows, HEAD_DIM), MASK_NEG, BF16)

    for hh in range(hpg):
        qx_scr[0, hh * t:(hh + 1) * t, 0:HEAD_DIM] = q_ref[:, hh * HEAD_DIM:(hh + 1) * HEAD_DIM]
    q4 = qx_scr[0, :, 0:HEAD_DIM]

    row = lax.broadcasted_iota(jnp.int32, (rows, LANES), 0)
    lane = lax.broadcasted_iota(jnp.int32, (rows, LANES), 1)
    cmp_ok = (lane * CMP_STRIDE + (CMP_BLOCK - 1)) <= (q0 + jnp.bitwise_and(row, t - 1))
    s = jnp.where(cmp_ok, _bdot_nt(q4, kc_ref[...]), MASK_NEG)
    e = jnp.where(cmp_ok, jnp.exp(s - jnp.max(s, axis=1, keepdims=True)), 0.0)
    den = jnp.sum(e, axis=1, keepdims=True)
    p = e / jnp.where(den > 0.0, den, 1.0)
    ocmp_scr[...] = _bdot(p.astype(BF16), vc_ref[...])
    p_sum = p[0:t]
    for hh in range(1, hpg):
        p_sum = p_sum + p[hh * t:(hh + 1) * t]

    jj = lax.broadcasted_iota(jnp.int32, (LANES, LANES), 0)
    nn = lax.broadcasted_iota(jnp.int32, (LANES, LANES), 1)
    overlap = jnp.logical_and(
        jnp.logical_and(nn * CMP_STRIDE < (jj + 1) * SLC_BLOCK, nn * CMP_STRIDE + CMP_BLOCK > jj * SLC_BLOCK),
        jj < N_SLC)
    overlap_t = jnp.where(overlap, 1.0, 0.0).astype(BF16)
    p_hi = p_sum.astype(BF16)
    p_lo = (p_sum - p_hi.astype(F32)).astype(BF16)
    imp_t = (_bdot_nt(overlap_t, p_hi) + _bdot_nt(overlap_t, p_lo))[0:N_SLC, :]

    blk = lax.broadcasted_iota(jnp.int32, (N_SLC, t), 0)
    pos = q0 + lax.broadcasted_iota(jnp.int32, (N_SLC, t), 1)
    cur = jnp.right_shift(pos, 6)
    forced = jnp.logical_or(blk == 0, jnp.logical_or(blk == cur, blk == cur - 1))
    val = jnp.where(forced, jnp.inf, jnp.where(blk * SLC_BLOCK <= pos, imp_t, -jnp.inf))
    rank = jnp.zeros((N_SLC, t), F32)
    for c in range(N_SLC):
        vc_row = val[c:c + 1, :]
        ahead = jnp.logical_or(vc_row > val, jnp.logical_and(vc_row == val, blk > c))
        rank = rank + jnp.where(ahead, 1.0, 0.0)
    notsel_t = jnp.where(rank < float(N_SELECT), 0.0, MASK_NEG)
    notsel_t = jnp.concatenate([notsel_t, jnp.zeros((LANES - N_SLC, t), F32)], axis=0)
    notsel = notsel_t.T.astype(BF16)
    for hh in range(hpg):
        qx_scr[0, hh * t:(hh + 1) * t, ext] = notsel

    m_scr[...] = jnp.full(m_scr.shape, M_INIT, F32)
    acc_scr[...] = jnp.zeros_like(acc_scr)

    def flash_update(slot, head, s, v_tile):
        sl = slice(head * t, (head + 1) * t)
        m_prev = m_scr[slot, sl, :]
        m_new = jnp.maximum(m_prev, jnp.max(s, axis=1, keepdims=True))
        alpha = jnp.exp(m_prev - m_new)
        p = jnp.exp(s - jnp.concatenate([m_new] * (t // LANES), axis=1))
        pv = _bdot(p.astype(BF16), v_tile)
        acc_scr[slot, sl, :] = jnp.concatenate([alpha, alpha], axis=1) * acc_scr[slot, sl, :] + pv
        m_scr[slot, sl, :] = m_new

    n_int = i + jnp.minimum(i, 1)

    def job(j):
        is_win = jnp.logical_and(j == i, j < n_int)
        src = is_win.astype(jnp.int32)
        variant = (j >= n_int).astype(jnp.int32)
        kt = jnp.where(is_win, i - 1, jnp.minimum(j, i))
        return variant, src, pl.multiple_of(kt * t, t)

    def scores(j, s_scr):
        variant, src, k0 = job(j)
        s_scr[...] = _bdot_nt(qx_scr[variant], kx_scr[src, pl.ds(k0, t), :])

    def update(j, s_scr):
        _, src, k0 = job(j)
        v_tile = vx_scr[src, pl.ds(k0, t), :]
        for hh in range(hpg):
            flash_update(src, hh, s_scr[hh * t:(hh + 1) * t, :], v_tile)

    def pair(pi, carry):
        j = 2 * pi
        scores(j + 1, s1_scr)
        update(j, s0_scr)
        scores(j + 2, s0_scr)
        update(j + 1, s1_scr)
        return carry

    scores(0, s0_scr)
    lax.fori_loop(0, jnp.right_shift(n_int + 1, 1), pair, 0)

    r2 = lax.broadcasted_iota(jnp.int32, (t, t), 0)
    c2 = lax.broadcasted_iota(jnp.int32, (t, t), 1)
    causal_bias = jnp.where(c2 <= r2, 0.0, MASK_NEG)
    band_bias = jnp.where(c2 > r2, 0.0, MASK_NEG)

    def masked_tile(variant, src, k0, bias):
        s = _bdot_nt(qx_scr[variant], kx_scr[src, pl.ds(k0, t), :])
        v_tile = vx_scr[src, pl.ds(k0, t), :]
        for hh in range(hpg):
            flash_update(src, hh, s[hh * t:(hh + 1) * t, :] + bias, v_tile)

    has_far = i >= WINDOW // t
    far_src = has_far.astype(jnp.int32)
    masked_tile(1 - far_src, far_src,
                pl.multiple_of(jnp.maximum(i - WINDOW // t, 0) * t, t), band_bias)
    masked_tile(0, 1, pl.multiple_of(q0, t), causal_bias)
    masked_tile(0, 0, pl.multiple_of(q0, t), causal_bias)

    gates = jax.nn.sigmoid(gt_ref[...].astype(F32))
    acc_s = acc_scr[0]
    acc_w = acc_scr[1]
    o_slc = acc_s[:, 0:HEAD_DIM] / acc_s[:, HEAD_DIM:2 * HEAD_DIM]
    o_win = acc_w[:, 0:HEAD_DIM] / acc_w[:, HEAD_DIM:2 * HEAD_DIM]
    o_cmp = ocmp_scr[...]
    for hh in range(hpg):
        sl = slice(hh * t, (hh + 1) * t)
        o = (gates[:, 3 * hh:3 * hh + 1] * o_cmp[sl]
             + gates[:, 3 * hh + 1:3 * hh + 2] * o_slc[sl]
             + gates[:, 3 * hh + 2:3 * hh + 3] * o_win[sl])
        o_ref[:, hh * HEAD_DIM:(hh + 1) * HEAD_DIM] = o.astype(BF16)


def _attention(proj2d, kcvc):
    t = ATT_T
    g_w = HEADS_PER_GROUP * HEAD_DIM
    rows = HEADS_PER_GROUP * t
    pw_g = PROJ_W // g_w
    pw_h = PROJ_W // HEAD_DIM

    def kv_spec(j):
        base = (COL_KV + j * KV_WIDTH) // HEAD_DIM
        return pl.BlockSpec((SEQ, HEAD_DIM), lambda b, g, i: (0, b * pw_h + base + g))

    def cmp_spec(tsel):
        return pl.BlockSpec((None, None, None, N_CMP_PAD, HEAD_DIM), lambda b, g, i: (tsel, b, g, 0, 0))

    return pl.pallas_call(
        _attn_kernel,
        grid=(BATCH, N_KV_GROUPS, SEQ // t),
        in_specs=[
            pl.BlockSpec((t, g_w), lambda b, g, i: (i, b * pw_g + COL_Q // g_w + g)),
            kv_spec(2), kv_spec(3), kv_spec(4), kv_spec(5),
            cmp_spec(0), cmp_spec(1),
            pl.BlockSpec((t, LANES), lambda b, g, i: (i, b * pw_h + COL_NG // LANES + g)),
        ],
        out_specs=pl.BlockSpec((t, g_w), lambda b, g, i: (i, b * N_KV_GROUPS + g)),
        out_shape=jax.ShapeDtypeStruct((SEQ, BATCH * Q_WIDTH), BF16),
        scratch_shapes=[
            pltpu.VMEM((2, rows, 2 * HEAD_DIM), BF16),
            pltpu.VMEM((2, SEQ, 2 * HEAD_DIM), BF16),
            pltpu.VMEM((2, SEQ, 2 * HEAD_DIM), BF16),
            pltpu.VMEM((rows, t), F32),
            pltpu.VMEM((rows, t), F32),
            pltpu.VMEM((2, rows, 2 * HEAD_DIM), F32),
            pltpu.VMEM((2, rows, LANES), F32),
            pltpu.VMEM((rows, HEAD_DIM), F32),
        ],
        compiler_params=pltpu.CompilerParams(
            dimension_semantics=("parallel", "parallel", "arbitrary"), vmem_limit_bytes=_vmem(40)),
        name="nsa_attention",
    )(proj2d, proj2d, proj2d, proj2d, proj2d, kcvc, kcvc, proj2d)


def _merge_kernel(ha_ref, ob_ref, wa_ref, wb_ref, ga_ref, gb_ref, o_ref):
    for mc in range(OUT_TS // OUT_MC):
        rs = slice(mc * OUT_MC, (mc + 1) * OUT_MC)
        ya = _bdot(ha_ref[rs, :], wa_ref[...])
        yb = _bdot(ob_ref[rs, :], wb_ref[...])
        ga = jax.nn.sigmoid(ga_ref[rs, :].astype(F32))
        gb = jax.nn.sigmoid(gb_ref[rs, :].astype(F32))
        o_ref[rs, :] = (ga * ya + gb * yb).astype(BF16)


def _merge(hg2d, ob2d, proj2d, wa, wb):
    ts, tn = OUT_TS, PROJ_TN
    nn = D_MODEL // tn
    pw = PROJ_W // tn
    return pl.pallas_call(
        _merge_kernel,
        grid=(SEQ // ts, BATCH, nn),
        in_specs=[
            pl.BlockSpec((ts, D_RNN), lambda i, b, n: (i, b)),
            pl.BlockSpec((ts, Q_WIDTH), lambda i, b, n: (i, b)),
            pl.BlockSpec((D_RNN, tn), lambda i, b, n: (0, n)),
            pl.BlockSpec((Q_WIDTH, tn), lambda i, b, n: (0, n)),
            pl.BlockSpec((ts, tn), lambda i, b, n: (i, b * pw + COL_MG // tn + n)),
            pl.BlockSpec((ts, tn), lambda i, b, n: (i, b * pw + (COL_MG + D_MODEL) // tn + n)),
        ],
        out_specs=pl.BlockSpec((ts, tn), lambda i, b, n: (i, b * nn + n)),
        out_shape=jax.ShapeDtypeStruct((SEQ, BATCH * D_MODEL), BF16),
        compiler_params=pltpu.CompilerParams(
            dimension_semantics=("parallel", "parallel", "arbitrary"), vmem_limit_bytes=_vmem(40)),
        name="branch_merge",
    )(hg2d, ob2d, wa, wb, proj2d, proj2d)


def _mixout_kernel(y_ref, w_ref, x_ref, gt_ref, post_ref, o_ref):
    for mc in range(OUT_TS // OUT_MC):
        rs = slice(mc * OUT_MC, (mc + 1) * OUT_MC)
        mixed = _bdot(y_ref[rs, :], w_ref[...])
        o_ref[rs, :] = x_ref[rs, :] + gt_ref[...] * _rms(mixed, post_ref[...])


def _mix_out(ymix, w_out, x1, mod3, post_g):
    ts = OUT_TS
    return pl.pallas_call(
        _mixout_kernel,
        grid=(SEQ // ts, BATCH),
        in_specs=[
            pl.BlockSpec((ts, D_MODEL), lambda i, b: (i, b)),
            pl.BlockSpec((D_MODEL, D_MODEL), lambda i, b: (0, 0)),
            pl.BlockSpec((ts, D_MODEL), lambda i, b: (i, b)),
            pl.BlockSpec((None, 1, D_MODEL), lambda i, b: (b * N_ADA + 5, 0, 0)),
            pl.BlockSpec((1, D_MODEL), lambda i, b: (0, 0)),
        ],
        out_specs=pl.BlockSpec((ts, D_MODEL), lambda i, b: (i, b)),
        out_shape=jax.ShapeDtypeStruct((SEQ, BATCH * D_MODEL), F32),
        compiler_params=pltpu.CompilerParams(
            dimension_semantics=("parallel", "parallel"), vmem_limit_bytes=_vmem(56)),
        name="mix_out",
    )(ymix, w_out, x1, mod3, post_g)


_HEAD_PERM_RUNS = ((0, ROPE_DIM // 2), (ROPE_DIM, LANES // 2 + ROPE_DIM // 2),
                   (ROPE_DIM // 2, ROPE_DIM), (LANES // 2 + ROPE_DIM // 2, HEAD_DIM))


def _permute_head(a, axis=-1):
    return jnp.concatenate([lax.slice_in_dim(a, lo, hi, axis=axis) for lo, hi in _HEAD_PERM_RUNS], axis=axis)


def _rope_tables():
    pos = jnp.arange(SEQ).astype(F32)
    inv_freq = ROPE_THETA ** (-jnp.arange(0, ROPE_DIM, 2, dtype=F32) / ROPE_DIM)
    ang = pos[:, None] * inv_freq[None, :]
    cos, sin = jnp.cos(ang), jnp.sin(ang)
    gap = LANES // 2 - ROPE_DIM // 2
    ones, zeros = jnp.ones((SEQ, gap), F32), jnp.zeros((SEQ, gap), F32)
    cos_t = jnp.concatenate([cos, ones, cos, ones], axis=1)
    sin_t = jnp.concatenate([-sin, zeros, sin, zeros], axis=1)
    scale = HEAD_DIM ** -0.5
    cos_all = jnp.stack([jnp.ones_like(cos_t), cos_t, cos_t * scale])
    sin_all = jnp.stack([jnp.zeros_like(sin_t), sin_t, sin_t * scale])
    return cos_all, sin_all


N_GATE_LOGITS = 3 * N_HEADS
PACK_TR = 1024
PACK_WIN = PROJ_TN + LANES


def _pack_matrices():
    m = np.zeros((4, PACK_WIN, PROJ_TN), np.float32)
    cols = np.arange(PROJ_TN)
    m[0, cols, cols] = 1.0
    perm = np.concatenate([np.arange(lo, hi) for lo, hi in _HEAD_PERM_RUNS])
    m[1, (cols // HEAD_DIM) * HEAD_DIM + perm[cols % HEAD_DIM], cols] = 1.0
    m[2, cols + N_GATE_LOGITS, cols] = 1.0
    per_group = 3 * HEADS_PER_GROUP
    for g in range(N_KV_GROUPS):
        m[3, g * per_group + np.arange(per_group), g * LANES + np.arange(per_group)] = 1.0
    return jnp.asarray(m, dtype=BF16)


def _pack_kernel(a_ref, b_ref, m_ref, o_ref):
    lane = lax.broadcasted_iota(jnp.int32, (PACK_TR, LANES), 1)
    tail = jnp.where(lane < N_GATE_LOGITS, b_ref[...], jnp.zeros((), BF16))
    win = jnp.concatenate([a_ref[...], tail], axis=1)
    o_ref[...] = _bdot(win, m_ref[...]).astype(BF16)


def _pack_w_in(w_in):
    n_tiles = PROJ_W // PROJ_TN
    mg_lo, mg_hi = COL_MG // PROJ_TN, COL_NG // PROJ_TN

    def kind(n):
        is_perm = jnp.logical_or(_is_query_tile(n), _is_key_tile(n))
        return jnp.where(n >= mg_hi, 3, jnp.where(n >= mg_lo, 2, jnp.where(is_perm, 1, 0)))

    def a_idx(n):
        return jnp.where(n >= mg_hi, mg_lo, n)

    def b_idx(n):
        ratio = PROJ_TN // LANES
        return jnp.where(jnp.logical_and(n >= mg_lo, n < mg_hi), (n + 1) * ratio, 0)

    return pl.pallas_call(
        _pack_kernel,
        grid=(n_tiles, D_MODEL // PACK_TR),
        in_specs=[
            pl.BlockSpec((None, PACK_TR, PROJ_TN), lambda n, r: (0, r, a_idx(n))),
            pl.BlockSpec((None, PACK_TR, LANES), lambda n, r: (0, r, b_idx(n))),
            pl.BlockSpec((None, PACK_WIN, PROJ_TN), lambda n, r: (kind(n), 0, 0)),
        ],
        out_specs=pl.BlockSpec((PACK_TR, PROJ_TN), lambda n, r: (r, n)),
        out_shape=jax.ShapeDtypeStruct((D_MODEL, PROJ_W), BF16),
        compiler_params=pltpu.CompilerParams(
            dimension_semantics=("parallel", "parallel"), vmem_limit_bytes=_vmem(32)),
        name="pack_w_in",
    )(w_in, w_in, _pack_matrices())


def _pack_compress_weights(pe_k, w1_k, w2_k, pe_v, w1_v, w2_v):
    pe_k = _permute_head(pe_k)
    w1_k = _permute_head(w1_k.reshape(CMP_BLOCK, HEAD_DIM, HEAD_DIM), axis=1).reshape(CMP_BLOCK * HEAD_DIM, HEAD_DIM)
    w2_k = _permute_head(w2_k)
    return (jnp.stack([pe_k, pe_v]), jnp.stack([w1_k, w1_v]).astype(BF16),
            jnp.stack([w2_k, w2_v]).astype(BF16))


def kernel(x, c, w_ada, b_ada, ffn1_pre_g, ffn1_post_g, ffn1_w_gate, ffn1_w_up, ffn1_w_down, mix_pre_g, mix_post_g, w_in, conv_w, conv_b, lru_wr, lru_br, lru_wi, lru_bi, lru_lambda, cmp_pe_k, cmp_w1_k, cmp_w2_k, cmp_pe_v, cmp_w1_v, cmp_w2_v, w_a_out, w_b_out, w_out, ffn2_pre_g, ffn2_post_g, ffn2_w_gate, ffn2_w_up, ffn2_w_down):
    assert x.shape == (BATCH, SEQ, D_MODEL) and w_ada.shape[0] == 1
    mod = _modulation(c, w_ada, b_ada)
    mod3 = mod.reshape(BATCH * N_ADA, 1, D_MODEL)

    x1 = _ffn(x, mod3, 0, ffn1_pre_g, ffn1_post_g, ffn1_w_gate[0].astype(BF16),
              ffn1_w_up[0].astype(BF16), ffn1_w_down[0].astype(BF16),
              x_time_major=False, out_time_major=True)

    cos_t, sin_t = _rope_tables()
    proj2d = _projection(x1, mod3, mix_pre_g, _pack_w_in(w_in.astype(BF16)), cos_t, sin_t)

    hg = _rglru(proj2d, conv_w[0], conv_b, lru_wr[0].astype(BF16), lru_br,
                lru_wi[0].astype(BF16), lru_bi, lru_lambda)

    pe, w1, w2 = _pack_compress_weights(cmp_pe_k[0], cmp_w1_k[0], cmp_w2_k[0],
                                        cmp_pe_v[0], cmp_w1_v[0], cmp_w2_v[0])
    kcvc = _compress(proj2d, pe, w1, w2)

    ob = _attention(proj2d, kcvc)
    ymix = _merge(hg, ob, proj2d, w_a_out[0].astype(BF16), w_b_out[0].astype(BF16))
    x2 = _mix_out(ymix, w_out[0].astype(BF16), x1, mod3, mix_post_g)

    return _ffn(x2, mod3, 6, ffn2_pre_g, ffn2_post_g, ffn2_w_gate[0].astype(BF16),
                ffn2_w_up[0].astype(BF16), ffn2_w_down[0].astype(BF16),
                x_time_major=True, out_time_major=False)
```

```python
import math

import jax
import jax.numpy as jnp
import numpy as np
from jax import lax
from jax.experimental import pallas as pl
from jax.experimental.pallas import tpu as pltpu

F32 = jnp.float32
BF16 = jnp.bfloat16

D_MODEL = 2048
BATCH = 8
SEQ = 2048
D_RNN = D_MODEL
LRU_BLOCKS = 16
LRU_BLOCK_DIM = D_RNN // LRU_BLOCKS
CONV_WIDTH = 4
LRU_C = 8.0
N_HEADS = 16
HEAD_DIM = 128
N_KV_GROUPS = 4
HEADS_PER_GROUP = N_HEADS // N_KV_GROUPS
Q_WIDTH = N_HEADS * HEAD_DIM
KV_WIDTH = N_KV_GROUPS * HEAD_DIM
CMP_STRIDE = 16
CMP_BLOCK = 2 * CMP_STRIDE
SLC_BLOCK = 64
N_SELECT = 16
WINDOW = 512
ROPE_THETA = 500000.0
ROPE_DIM = HEAD_DIM // 4
D_FF = 5632
NORM_EPS = 1e-6
N_ADA = 9
N_SLC = SEQ // SLC_BLOCK
N_CMP_PAD = SEQ // CMP_STRIDE

LANES = 128
SUBLANES = 8
BF16_ROWS = 16

COL_XA = 0
COL_YA = COL_XA + D_RNN
COL_Q = COL_YA + D_RNN
COL_KV = COL_Q + Q_WIDTH
COL_MG = COL_KV + 6 * KV_WIDTH
COL_NG = COL_MG + 2 * D_MODEL
PROJ_TN = 512
NG_PAD = PROJ_TN
PROJ_W = COL_NG + NG_PAD

MASK_NEG = -1e30
M_INIT = -1e29

FFN_TS = 512
FFN_TF = 512
FFN_PRE_ROWS = 48
PROJ_TS = 1024
PROJ_MC = 128
OUT_TS = 512
OUT_MC = 256
ATT_T = 256
LRU_TS = 512
LRU_CW = 512
MOD_TN = 1024


def _vmem(mb):
    return mb * 1024 * 1024


def _rms(x, g):
    return x * lax.rsqrt(jnp.mean(x * x, axis=-1, keepdims=True) + NORM_EPS) * g


def _gelu_tanh(x):
    c = math.sqrt(2.0 / math.pi)
    return x * (0.5 * (1.0 + jnp.tanh(c * (x + 0.044715 * (x * x * x)))))


def _bdot(a, b):
    return jnp.dot(a, b, preferred_element_type=F32)


def _bdot_nt(a, b):
    return lax.dot_general(a, b, (((1,), (1,)), ((), ())), preferred_element_type=F32)


def _mod_kernel(c_ref, w_ref, b_ref, o_ref):
    c = c_ref[...]
    ca = c * jax.nn.sigmoid(c)
    o_ref[...] = _bdot(ca.astype(BF16), w_ref[...].astype(BF16)) + b_ref[...]


def _modulation(c, w_ada, b_ada):
    n = N_ADA * D_MODEL
    return pl.pallas_call(
        _mod_kernel,
        grid=(n // MOD_TN,),
        in_specs=[
            pl.BlockSpec((BATCH, D_MODEL), lambda j: (0, 0)),
            pl.BlockSpec((None, D_MODEL, MOD_TN), lambda j: (0, 0, j)),
            pl.BlockSpec((1, MOD_TN), lambda j: (0, j)),
        ],
        out_specs=pl.BlockSpec((BATCH, MOD_TN), lambda j: (0, j)),
        out_shape=jax.ShapeDtypeStruct((BATCH, n), F32),
        compiler_params=pltpu.CompilerParams(
            dimension_semantics=("arbitrary",), vmem_limit_bytes=_vmem(40)),
        name="adaln_mod",
    )(c, w_ada, b_ada)


def _norm_next_rows(step, rows, tile_rows, xn_ref, pre_ref, scn_ref, shn_ref, u_next):
    r0 = pl.multiple_of(jnp.minimum(step * rows, tile_rows - rows), BF16_ROWS)
    un = _rms(xn_ref[pl.ds(r0, rows), :], pre_ref[...]) * (1.0 + scn_ref[...]) + shn_ref[...]
    u_next[pl.ds(r0, rows), :] = un.astype(BF16)


def _ffn_kernel(x_ref, xn_ref, sh_ref, sc_ref, gt_ref, shn_ref, scn_ref, pre_ref, post_ref,
                wg_ref, wu_ref, wd_ref, o_ref, u_scr, acc_scr):
    f = pl.program_id(2)
    tile = pl.program_id(0) * pl.num_programs(1) + pl.program_id(1)
    cur = jnp.bitwise_and(tile, 1)

    @pl.when(jnp.logical_and(tile == 0, f == 0))
    def _():
        u = _rms(x_ref[...], pre_ref[...]) * (1.0 + sc_ref[...]) + sh_ref[...]
        u_scr[0] = u.astype(BF16)

    @pl.when(f == 0)
    def _():
        acc_scr[...] = jnp.zeros_like(acc_scr)

    u = u_scr[cur]
    gate = _bdot(u, wg_ref[...])
    up = _bdot(u, wu_ref[...])
    h = (gate * jax.nn.sigmoid(gate)) * up
    acc_scr[...] += _bdot(h.astype(BF16), wd_ref[...])
    _norm_next_rows(f, FFN_PRE_ROWS, FFN_TS, xn_ref, pre_ref, scn_ref, shn_ref, u_scr.at[1 - cur])

    @pl.when(f == pl.num_programs(2) - 1)
    def _():
        y = _rms(acc_scr[...], post_ref[...])
        o_ref[...] = x_ref[...] + 0.5 * gt_ref[...] * y


def _next_tile(i, b, tile_rows):
    nxt = jnp.minimum(i * BATCH + b + 1, (SEQ // tile_rows) * BATCH - 1)
    return nxt // BATCH, lax.rem(nxt, BATCH)


def _ffn(x, mod3, k_mod, pre_g, post_g, wg, wu, wd, *, x_time_major, out_time_major):
    ts, tf = FFN_TS, FFN_TF
    assert (D_FF // tf) * FFN_PRE_ROWS >= ts

    def tm_spec(idx):
        return pl.BlockSpec((ts, D_MODEL), lambda i, b, f: idx(i, b))

    def bm_spec(idx):
        return pl.BlockSpec((None, ts, D_MODEL), lambda i, b, f: idx(i, b)[::-1] + (0,))

    here = lambda i, b: (i, b)
    nxt = lambda i, b: _next_tile(i, b, ts)
    x_spec, xn_spec = [(tm_spec if x_time_major else bm_spec)(idx) for idx in (here, nxt)]
    if out_time_major:
        o_spec, o_shape = tm_spec(here), jax.ShapeDtypeStruct((SEQ, BATCH * D_MODEL), F32)
    else:
        o_spec, o_shape = bm_spec(here), jax.ShapeDtypeStruct((BATCH, SEQ, D_MODEL), F32)

    def mod_spec(k, idx=here):
        return pl.BlockSpec((None, 1, D_MODEL), lambda i, b, f: (idx(i, b)[1] * N_ADA + k, 0, 0))

    vec_spec = pl.BlockSpec((1, D_MODEL), lambda i, b, f: (0, 0))
    return pl.pallas_call(
        _ffn_kernel,
        grid=(SEQ // ts, BATCH, D_FF // tf),
        in_specs=[
            x_spec, xn_spec, mod_spec(k_mod), mod_spec(k_mod + 1), mod_spec(k_mod + 2),
            mod_spec(k_mod, nxt), mod_spec(k_mod + 1, nxt), vec_spec, vec_spec,
            pl.BlockSpec((D_MODEL, tf), lambda i, b, f: (0, f)),
            pl.BlockSpec((D_MODEL, tf), lambda i, b, f: (0, f)),
            pl.BlockSpec((tf, D_MODEL), lambda i, b, f: (f, 0)),
        ],
        out_specs=o_spec,
        out_shape=o_shape,
        scratch_shapes=[pltpu.VMEM((2, ts, D_MODEL), BF16), pltpu.VMEM((ts, D_MODEL), F32)],
        compiler_params=pltpu.CompilerParams(
            dimension_semantics=("arbitrary", "arbitrary", "arbitrary"), vmem_limit_bytes=_vmem(58)),
        name="macaron_ffn",
    )(x, x, mod3, mod3, mod3, mod3, mod3, pre_g, post_g, wg, wu, wd)


def _rope_slice(x, c, s):
    return x * c + pltpu.roll(x, LANES // 2, axis=1) * s


def _proj_kernel(x_ref, sh_ref, sc_ref, pre_ref, w_ref, cos_ref, sin_ref, o_ref, u_scr):
    n = pl.program_id(2)

    @pl.when(n == 0)
    def _():
        u = _rms(x_ref[...], pre_ref[...]) * (1.0 + sc_ref[...]) + sh_ref[...]
        u_scr[...] = u.astype(BF16)

    w = w_ref[...]
    for mc in range(PROJ_TS // PROJ_MC):
        rs = slice(mc * PROJ_MC, (mc + 1) * PROJ_MC)
        r = _bdot(u_scr[rs, :], w)
        c = cos_ref[rs, :]
        s = sin_ref[rs, :]
        for hh in range(PROJ_TN // HEAD_DIM):
            sl = slice(hh * HEAD_DIM, (hh + 1) * HEAD_DIM)
            o_ref[rs, sl] = _rope_slice(r[:, sl], c, s).astype(BF16)


def _is_key_tile(n):
    is_k = n == COL_KV // PROJ_TN
    for j in range(1, 3):
        is_k = jnp.logical_or(is_k, n == (COL_KV + 2 * j * KV_WIDTH) // PROJ_TN)
    return is_k


def _is_query_tile(n):
    return jnp.logical_and(n >= COL_Q // PROJ_TN, n < COL_KV // PROJ_TN)


def _rope_kind(n):
    return jnp.where(_is_query_tile(n), 2, jnp.where(_is_key_tile(n), 1, 0))


def _projection(x1, mod3, pre_g, w_all, cos_t, sin_t):
    ts, tn = PROJ_TS, PROJ_TN
    nn = PROJ_W // tn

    def mod_spec(k):
        return pl.BlockSpec((None, 1, D_MODEL), lambda i, b, n: (b * N_ADA + k, 0, 0))

    tab_spec = pl.BlockSpec((None, ts, LANES), lambda i, b, n: (_rope_kind(n), i, 0))

    return pl.pallas_call(
        _proj_kernel,
        grid=(SEQ // ts, BATCH, nn),
        in_specs=[
            pl.BlockSpec((ts, D_MODEL), lambda i, b, n: (i, b)),
            mod_spec(3), mod_spec(4),
            pl.BlockSpec((1, D_MODEL), lambda i, b, n: (0, 0)),
            pl.BlockSpec((D_MODEL, tn), lambda i, b, n: (0, n)),
            tab_spec, tab_spec,
        ],
        out_specs=pl.BlockSpec((ts, tn), lambda i, b, n: (i, b * nn + n)),
        out_shape=jax.ShapeDtypeStruct((SEQ, BATCH * PROJ_W), BF16),
        scratch_shapes=[pltpu.VMEM((ts, D_MODEL), BF16)],
        compiler_params=pltpu.CompilerParams(
            dimension_semantics=("parallel", "parallel", "arbitrary"), vmem_limit_bytes=_vmem(48)),
        name="mix_in_proj",
    )(x1, mod3, mod3, pre_g, w_all, cos_t, sin_t)


def _lru_kernel(xa_ref, ya_ref, cw_ref, cb_ref, wr_ref, br_ref, wi_ref, bi_ref, lam_ref,
                o_ref, xe_scr, a_scr, b_scr, h_scr):
    tc = pl.program_id(2)
    ts = LRU_TS
    groups = ts // SUBLANES

    @pl.when(tc == 0)
    def _():
        xe_scr[0:SUBLANES, :] = jnp.zeros((SUBLANES, LRU_CW), F32)
        h_scr[...] = jnp.zeros_like(h_scr)

    xe_scr[SUBLANES:SUBLANES + ts, :] = xa_ref[...].astype(F32)
    cw = cw_ref[...]
    lead = SUBLANES - (CONV_WIDTH - 1)
    xc = cb_ref[...] + xe_scr[lead:lead + ts, :] * cw[0:1, :]
    for w in range(1, CONV_WIDTH):
        xc = xc + xe_scr[lead + w:lead + w + ts, :] * cw[w:w + 1, :]
    xe_scr[0:SUBLANES, :] = xe_scr[ts:ts + SUBLANES, :]

    nlam = -lam_ref[...]
    softplus = jnp.maximum(nlam, 0.0) + jnp.log1p(jnp.exp(-jnp.abs(nlam)))
    sub = lax.broadcasted_iota(jnp.int32, (groups, SUBLANES, LRU_BLOCK_DIM), 1)
    for k in range(LRU_CW // LRU_BLOCK_DIM):
        sl = slice(k * LRU_BLOCK_DIM, (k + 1) * LRU_BLOCK_DIM)
        xck = xc[:, sl]
        xb = xck.astype(BF16)
        r = jax.nn.sigmoid(_bdot(xb, wr_ref[k]) + br_ref[:, sl])
        ig = jax.nn.sigmoid(_bdot(xb, wi_ref[k]) + bi_ref[:, sl])
        log_a = (-LRU_C * r) * softplus[:, sl]
        em1 = jnp.tanh(log_a) * (jnp.exp(2.0 * log_a) + 1.0)
        a = jnp.exp(log_a).reshape(groups, SUBLANES, LRU_BLOCK_DIM)
        b = (jnp.sqrt(-em1) * (ig * xck)).reshape(groups, SUBLANES, LRU_BLOCK_DIM)
        d = 1
        while d < SUBLANES:
            a_prev = jnp.where(sub < d, 1.0, pltpu.roll(a, d, axis=1))
            b_prev = jnp.where(sub < d, 0.0, pltpu.roll(b, d, axis=1))
            b = a * b_prev + b
            a = a * a_prev
            d *= 2
        a_scr[:, sl] = a.reshape(ts, LRU_BLOCK_DIM)
        b_scr[:, sl] = b.reshape(ts, LRU_BLOCK_DIM)

    def step(g, h_prev):
        r0 = pl.multiple_of(g * SUBLANES, SUBLANES)
        h = b_scr[pl.ds(r0, SUBLANES), :] + a_scr[pl.ds(r0, SUBLANES), :] * h_prev
        b_scr[pl.ds(r0, SUBLANES), :] = h
        return jnp.broadcast_to(h[SUBLANES - 1:SUBLANES, :], (SUBLANES, LRU_CW))

    h_scr[...] = lax.fori_loop(0, groups, step, h_scr[...], unroll=4)
    o_ref[...] = (b_scr[...] * _gelu_tanh(ya_ref[...].astype(F32))).astype(BF16)


def _rglru(proj2d, conv_w, conv_b, wr, br, wi, bi, lam):
    ts = LRU_TS
    ncb = D_RNN // LRU_CW
    kb = LRU_CW // LRU_BLOCK_DIM
    pw = PROJ_W // LRU_CW
    vec = pl.BlockSpec((1, LRU_CW), lambda b, cb, tc: (0, cb))
    blk = pl.BlockSpec((kb, LRU_BLOCK_DIM, LRU_BLOCK_DIM), lambda b, cb, tc: (cb, 0, 0))
    return pl.pallas_call(
        _lru_kernel,
        grid=(BATCH, ncb, SEQ // ts),
        in_specs=[
            pl.BlockSpec((ts, LRU_CW), lambda b, cb, tc: (tc, b * pw + COL_XA // LRU_CW + cb)),
            pl.BlockSpec((ts, LRU_CW), lambda b, cb, tc: (tc, b * pw + COL_YA // LRU_CW + cb)),
            pl.BlockSpec((CONV_WIDTH, LRU_CW), lambda b, cb, tc: (0, cb)),
            vec, blk, vec, blk, vec, vec,
        ],
        out_specs=pl.BlockSpec((ts, LRU_CW), lambda b, cb, tc: (tc, b * ncb + cb)),
        out_shape=jax.ShapeDtypeStruct((SEQ, BATCH * D_RNN), BF16),
        scratch_shapes=[
            pltpu.VMEM((ts + SUBLANES, LRU_CW), F32),
            pltpu.VMEM((ts, LRU_CW), F32),
            pltpu.VMEM((ts, LRU_CW), F32),
            pltpu.VMEM((SUBLANES, LRU_CW), F32),
        ],
        compiler_params=pltpu.CompilerParams(
            dimension_semantics=("parallel", "parallel", "arbitrary"), vmem_limit_bytes=_vmem(40)),
        name="rglru",
    )(proj2d, proj2d, conv_w, conv_b, wr, br, wi, bi, lam)


def _cmp_kernel(x_ref, pe_ref, w1_ref, w2_ref, o_ref, xs_scr, cat_scr):
    xs_scr[0:SEQ, :] = x_ref[...].astype(F32)
    xs_scr[SEQ:SEQ + CMP_STRIDE, :] = jnp.zeros((CMP_STRIDE, HEAD_DIM), F32)
    pe = pe_ref[...]
    for p in range(CMP_BLOCK):
        tok = xs_scr[pl.ds(p, N_CMP_PAD, stride=CMP_STRIDE), :]
        cat_scr[:, p * HEAD_DIM:(p + 1) * HEAD_DIM] = (tok + pe[p:p + 1, :]).astype(BF16)
    pre = _bdot(cat_scr[...], w1_ref[...])
    o_ref[...] = _bdot(_gelu_tanh(pre).astype(BF16), w2_ref[...]).astype(BF16)


def _compress(proj2d, pe, w1, w2):
    pw_h = PROJ_W // HEAD_DIM
    base = COL_KV // HEAD_DIM
    return pl.pallas_call(
        _cmp_kernel,
        grid=(2, BATCH, N_KV_GROUPS),
        in_specs=[
            pl.BlockSpec((SEQ, HEAD_DIM), lambda t, b, g: (0, b * pw_h + base + t * N_KV_GROUPS + g)),
            pl.BlockSpec((None, CMP_BLOCK, HEAD_DIM), lambda t, b, g: (t, 0, 0)),
            pl.BlockSpec((None, CMP_BLOCK * HEAD_DIM, HEAD_DIM), lambda t, b, g: (t, 0, 0)),
            pl.BlockSpec((None, HEAD_DIM, HEAD_DIM), lambda t, b, g: (t, 0, 0)),
        ],
        out_specs=pl.BlockSpec((None, None, None, N_CMP_PAD, HEAD_DIM), lambda t, b, g: (t, b, g, 0, 0)),
        out_shape=jax.ShapeDtypeStruct((2, BATCH, N_KV_GROUPS, N_CMP_PAD, HEAD_DIM), BF16),
        scratch_shapes=[
            pltpu.VMEM((SEQ + CMP_STRIDE, HEAD_DIM), F32),
            pltpu.VMEM((N_CMP_PAD, CMP_BLOCK * HEAD_DIM), BF16),
        ],
        compiler_params=pltpu.CompilerParams(
            dimension_semantics=("parallel", "parallel", "parallel"), vmem_limit_bytes=_vmem(32)),
        name="kv_compress",
    )(proj2d, pe, w1, w2)


def _attn_kernel(q_ref, ks_ref, vs_ref, kw_ref, vw_ref, kc_ref, vc_ref, gt_ref, o_ref,
                 qx_scr, kx_scr, vx_scr, s0_scr, s1_scr, acc_scr, m_scr, ocmp_scr):
    t = ATT_T
    hpg = HEADS_PER_GROUP
    rows = hpg * t
    i = pl.program_id(2)
    q0 = i * t
    ext = slice(HEAD_DIM, 2 * HEAD_DIM)

    @pl.when(i == 0)
    def _():
        key_blk = jnp.right_shift(lax.broadcasted_iota(jnp.int32, (SEQ, LANES), 0), 6)
        onehot = jnp.where(key_blk == lax.broadcasted_iota(jnp.int32, (SEQ, LANES), 1), 1.0, 0.0)
        ones = jnp.ones((SEQ, HEAD_DIM), BF16)
        kx_scr[0, :, 0:HEAD_DIM] = ks_ref[...]
        kx_scr[0, :, ext] = onehot.astype(BF16)
        kx_scr[1, :, 0:HEAD_DIM] = kw_ref[...]
        kx_scr[1, :, ext] = jnp.zeros((SEQ, HEAD_DIM), BF16)
        vx_scr[0, :, 0:HEAD_DIM] = vs_ref[...]
        vx_scr[0, :, ext] = ones
        vx_scr[1, :, 0:HEAD_DIM] = vw_ref[...]
        vx_scr[1, :, ext] = ones
        qx_scr[1, :, 0:HEAD_DIM] = jnp.zeros((rows, HEAD_DIM), BF16)
        qx_scr[1, :, ext] = jnp.full((rows, HEAD_DIM), MASK_NEG, BF16)

    for hh in range(hpg):
        qx_scr[0, hh * t:(hh + 1) * t, 0:HEAD_DIM] = q_ref[:, hh * HEAD_DIM:(hh + 1) * HEAD_DIM]
    q4 = qx_scr[0, :, 0:HEAD_DIM]

    row = lax.broadcasted_iota(jnp.int32, (rows, LANES), 0)
    lane = lax.broadcasted_iota(jnp.int32, (rows, LANES), 1)
    cmp_ok = (lane * CMP_STRIDE + (CMP_BLOCK - 1)) <= (q0 + jnp.bitwise_and(row, t - 1))
    s = jnp.where(cmp_ok, _bdot_nt(q4, kc_ref[...]), MASK_NEG)
    e = jnp.where(cmp_ok, jnp.exp(s - jnp.max(s, axis=1, keepdims=True)), 0.0)
    den = jnp.sum(e, axis=1, keepdims=True)
    p = e / jnp.where(den > 0.0, den, 1.0)
    ocmp_scr[...] = _bdot(p.astype(BF16), vc_ref[...])
    p_sum = p[0:t]
    for hh in range(1, hpg):
        p_sum = p_sum + p[hh * t:(hh + 1) * t]

    jj = lax.broadcasted_iota(jnp.int32, (LANES, LANES), 0)
    nn = lax.broadcasted_iota(jnp.int32, (LANES, LANES), 1)
    overlap = jnp.logical_and(
        jnp.logical_and(nn * CMP_STRIDE < (jj + 1) * SLC_BLOCK, nn * CMP_STRIDE + CMP_BLOCK > jj * SLC_BLOCK),
        jj < N_SLC)
    overlap_t = jnp.where(overlap, 1.0, 0.0).astype(BF16)
    p_hi = p_sum.astype(BF16)
    p_lo = (p_sum - p_hi.astype(F32)).astype(BF16)
    imp_t = (_bdot_nt(overlap_t, p_hi) + _bdot_nt(overlap_t, p_lo))[0:N_SLC, :]

    blk = lax.broadcasted_iota(jnp.int32, (N_SLC, t), 0)
    pos = q0 + lax.broadcasted_iota(jnp.int32, (N_SLC, t), 1)
    cur = jnp.right_shift(pos, 6)
    forced = jnp.logical_or(blk == 0, jnp.logical_or(blk == cur, blk == cur - 1))
    val = jnp.where(forced, jnp.inf, jnp.where(blk * SLC_BLOCK <= pos, imp_t, -jnp.inf))
    rank = jnp.zeros((N_SLC, t), F32)
    for c in range(N_SLC):
        vc_row = val[c:c + 1, :]
        ahead = jnp.logical_or(vc_row > val, jnp.logical_and(vc_row == val, blk > c))
        rank = rank + jnp.where(ahead, 1.0, 0.0)
    notsel_t = jnp.where(rank < float(N_SELECT), 0.0, MASK_NEG)
    notsel_t = jnp.concatenate([notsel_t, jnp.zeros((LANES - N_SLC, t), F32)], axis=0)
    notsel = notsel_t.T.astype(BF16)
    for hh in range(hpg):
        qx_scr[0, hh * t:(hh + 1) * t, ext] = notsel

    m_scr[...] = jnp.full(m_scr.shape, M_INIT, F32)
    acc_scr[...] = jnp.zeros_like(acc_scr)

    def flash_update(slot, head, s, v_tile):
        sl = slice(head * t, (head + 1) * t)
        m_prev = m_scr[slot, sl, :]
        m_new = jnp.maximum(m_prev, jnp.max(s, axis=1, keepdims=True))
        alpha = jnp.exp(m_prev - m_new)
        p = jnp.exp(s - jnp.concatenate([m_new] * (t // LANES), axis=1))
        pv = _bdot(p.astype(BF16), v_tile)
        acc_scr[slot, sl, :] = jnp.concatenate([alpha, alpha], axis=1) * acc_scr[slot, sl, :] + pv
        m_scr[slot, sl, :] = m_new

    n_int = i + jnp.minimum(i, 1)

    def job(j):
        is_win = jnp.logical_and(j == i, j < n_int)
        src = is_win.astype(jnp.int32)
        variant = (j >= n_int).astype(jnp.int32)
        kt = jnp.where(is_win, i - 1, jnp.minimum(j, i))
        return variant, src, pl.multiple_of(kt * t, t)

    def scores(j, s_scr):
        variant, src, k0 = job(j)
        s_scr[...] = _bdot_nt(qx_scr[variant], kx_scr[src, pl.ds(k0, t), :])

    def update(j, s_scr):
        _, src, k0 = job(j)
        v_tile = vx_scr[src, pl.ds(k0, t), :]
        for hh in range(hpg):
            flash_update(src, hh, s_scr[hh * t:(hh + 1) * t, :], v_tile)

    def pair(pi, carry):
        j = 2 * pi
        scores(j + 1, s1_scr)
        update(j, s0_scr)
        scores(j + 2, s0_scr)
        update(j + 1, s1_scr)
        return carry

    scores(0, s0_scr)
    lax.fori_loop(0, jnp.right_shift(n_int + 1, 1), pair, 0)

    r2 = lax.broadcasted_iota(jnp.int32, (t, t), 0)
    c2 = lax.broadcasted_iota(jnp.int32, (t, t), 1)
    causal_bias = jnp.where(c2 <= r2, 0.0, MASK_NEG)
    band_bias = jnp.where(c2 > r2, 0.0, MASK_NEG)

    def masked_tile(variant, src, k0, bias):
        s = _bdot_nt(qx_scr[variant], kx_scr[src, pl.ds(k0, t), :])
        v_tile = vx_scr[src, pl.ds(k0, t), :]
        for hh in range(hpg):
            flash_update(src, hh, s[hh * t:(hh + 1) * t, :] + bias, v_tile)

    has_far = i >= WINDOW // t
    far_src = has_far.astype(jnp.int32)
    masked_tile(1 - far_src, far_src,
                pl.multiple_of(jnp.maximum(i - WINDOW // t, 0) * t, t), band_bias)
    masked_tile(0, 1, pl.multiple_of(q0, t), causal_bias)
    masked_tile(0, 0, pl.multiple_of(q0, t), causal_bias)

    gates = jax.nn.sigmoid(gt_ref[...].astype(F32))
    acc_s = acc_scr[0]
    acc_w = acc_scr[1]
    o_slc = acc_s[:, 0:HEAD_DIM] / acc_s[:, HEAD_DIM:2 * HEAD_DIM]
    o_win = acc_w[:, 0:HEAD_DIM] / acc_w[:, HEAD_DIM:2 * HEAD_DIM]
    o_cmp = ocmp_scr[...]
    for hh in range(hpg):
        sl = slice(hh * t, (hh + 1) * t)
        o = (gates[:, 3 * hh:3 * hh + 1] * o_cmp[sl]
             + gates[:, 3 * hh + 1:3 * hh + 2] * o_slc[sl]
             + gates[:, 3 * hh + 2:3 * hh + 3] * o_win[sl])
        o_ref[:, hh * HEAD_DIM:(hh + 1) * HEAD_DIM] = o.astype(BF16)


def _attention(proj2d, kcvc):
    t = ATT_T
    g_w = HEADS_PER_GROUP * HEAD_DIM
    rows = HEADS_PER_GROUP * t
    pw_g = PROJ_W // g_w
    pw_h = PROJ_W // HEAD_DIM

    def kv_spec(j):
        base = (COL_KV + j * KV_WIDTH) // HEAD_DIM
        return pl.BlockSpec((SEQ, HEAD_DIM), lambda b, g, i: (0, b * pw_h + base + g))

    def cmp_spec(tsel):
        return pl.BlockSpec((None, None, None, N_CMP_PAD, HEAD_DIM), lambda b, g, i: (tsel, b, g, 0, 0))

    return pl.pallas_call(
        _attn_kernel,
        grid=(BATCH, N_KV_GROUPS, SEQ // t),
        in_specs=[
            pl.BlockSpec((t, g_w), lambda b, g, i: (i, b * pw_g + COL_Q // g_w + g)),
            kv_spec(2), kv_spec(3), kv_spec(4), kv_spec(5),
            cmp_spec(0), cmp_spec(1),
            pl.BlockSpec((t, LANES), lambda b, g, i: (i, b * pw_h + COL_NG // LANES + g)),
        ],
        out_specs=pl.BlockSpec((t, g_w), lambda b, g, i: (i, b * N_KV_GROUPS + g)),
        out_shape=jax.ShapeDtypeStruct((SEQ, BATCH * Q_WIDTH), BF16),
        scratch_shapes=[
            pltpu.VMEM((2, rows, 2 * HEAD_DIM), BF16),
            pltpu.VMEM((2, SEQ, 2 * HEAD_DIM), BF16),
            pltpu.VMEM((2, SEQ, 2 * HEAD_DIM), BF16),
            pltpu.VMEM((rows, t), F32),
            pltpu.VMEM((rows, t), F32),
            pltpu.VMEM((2, rows, 2 * HEAD_DIM), F32),
            pltpu.VMEM((2, rows, LANES), F32),
            pltpu.VMEM((rows, HEAD_DIM), F32),
        ],
        compiler_params=pltpu.CompilerParams(
            dimension_semantics=("parallel", "parallel", "arbitrary"), vmem_limit_bytes=_vmem(40)),
        name="nsa_attention",
    )(proj2d, proj2d, proj2d, proj2d, proj2d, kcvc, kcvc, proj2d)


def _merge_kernel(ha_ref, ob_ref, wa_ref, wb_ref, ga_ref, gb_ref, o_ref):
    for mc in range(OUT_TS // OUT_MC):
        rs = slice(mc * OUT_MC, (mc + 1) * OUT_MC)
        ya = _bdot(ha_ref[rs, :], wa_ref[...])
        yb = _bdot(ob_ref[rs, :], wb_ref[...])
        ga = jax.nn.sigmoid(ga_ref[rs, :].astype(F32))
        gb = jax.nn.sigmoid(gb_ref[rs, :].astype(F32))
        o_ref[rs, :] = (ga * ya + gb * yb).astype(BF16)


def _merge(hg2d, ob2d, proj2d, wa, wb):
    ts, tn = OUT_TS, PROJ_TN
    nn = D_MODEL // tn
    pw = PROJ_W // tn
    return pl.pallas_call(
        _merge_kernel,
        grid=(SEQ // ts, BATCH, nn),
        in_specs=[
            pl.BlockSpec((ts, D_RNN), lambda i, b, n: (i, b)),
            pl.BlockSpec((ts, Q_WIDTH), lambda i, b, n: (i, b)),
            pl.BlockSpec((D_RNN, tn), lambda i, b, n: (0, n)),
            pl.BlockSpec((Q_WIDTH, tn), lambda i, b, n: (0, n)),
            pl.BlockSpec((ts, tn), lambda i, b, n: (i, b * pw + COL_MG // tn + n)),
            pl.BlockSpec((ts, tn), lambda i, b, n: (i, b * pw + (COL_MG + D_MODEL) // tn + n)),
        ],
        out_specs=pl.BlockSpec((ts, tn), lambda i, b, n: (i, b * nn + n)),
        out_shape=jax.ShapeDtypeStruct((SEQ, BATCH * D_MODEL), BF16),
        compiler_params=pltpu.CompilerParams(
            dimension_semantics=("parallel", "parallel", "arbitrary"), vmem_limit_bytes=_vmem(40)),
        name="branch_merge",
    )(hg2d, ob2d, wa, wb, proj2d, proj2d)


def _mixout_kernel(y_ref, w_ref, x_ref, gt_ref, post_ref, o_ref):
    for mc in range(OUT_TS // OUT_MC):
        rs = slice(mc * OUT_MC, (mc + 1) * OUT_MC)
        mixed = _bdot(y_ref[rs, :], w_ref[...])
        o_ref[rs, :] = x_ref[rs, :] + gt_ref[...] * _rms(mixed, post_ref[...])


def _mix_out(ymix, w_out, x1, mod3, post_g):
    ts = OUT_TS
    return pl.pallas_call(
        _mixout_kernel,
        grid=(SEQ // ts, BATCH),
        in_specs=[
            pl.BlockSpec((ts, D_MODEL), lambda i, b: (i, b)),
            pl.BlockSpec((D_MODEL, D_MODEL), lambda i, b: (0, 0)),
            pl.BlockSpec((ts, D_MODEL), lambda i, b: (i, b)),
            pl.BlockSpec((None, 1, D_MODEL), lambda i, b: (b * N_ADA + 5, 0, 0)),
            pl.BlockSpec((1, D_MODEL), lambda i, b: (0, 0)),
        ],
        out_specs=pl.BlockSpec((ts, D_MODEL), lambda i, b: (i, b)),
        out_shape=jax.ShapeDtypeStruct((SEQ, BATCH * D_MODEL), F32),
        compiler_params=pltpu.CompilerParams(
            dimension_semantics=("parallel", "parallel"), vmem_limit_bytes=_vmem(56)),
        name="mix_out",
    )(ymix, w_out, x1, mod3, post_g)


_HEAD_PERM_RUNS = ((0, ROPE_DIM // 2), (ROPE_DIM, LANES // 2 + ROPE_DIM // 2),
                   (ROPE_DIM // 2, ROPE_DIM), (LANES // 2 + ROPE_DIM // 2, HEAD_DIM))


def _permute_head(a, axis=-1):
    return jnp.concatenate([lax.slice_in_dim(a, lo, hi, axis=axis) for lo, hi in _HEAD_PERM_RUNS], axis=axis)


def _rope_tables():
    pos = jnp.arange(SEQ).astype(F32)
    inv_freq = ROPE_THETA ** (-jnp.arange(0, ROPE_DIM, 2, dtype=F32) / ROPE_DIM)
    ang = pos[:, None] * inv_freq[None, :]
    cos, sin = jnp.cos(ang), jnp.sin(ang)
    gap = LANES // 2 - ROPE_DIM // 2
    ones, zeros = jnp.ones((SEQ, gap), F32), jnp.zeros((SEQ, gap), F32)
    cos_t = jnp.concatenate([cos, ones, cos, ones], axis=1)
    sin_t = jnp.concatenate([-sin, zeros, sin, zeros], axis=1)
    scale = HEAD_DIM ** -0.5
    cos_all = jnp.stack([jnp.ones_like(cos_t), cos_t, cos_t * scale])
    sin_all = jnp.stack([jnp.zeros_like(sin_t), sin_t, sin_t * scale])
    return cos_all, sin_all


N_GATE_LOGITS = 3 * N_HEADS
PACK_WIN = PROJ_TN + LANES


def _pack_matrices():
    m = np.zeros((4, PROJ_TN, PACK_WIN), np.float32)
    cols = np.arange(PROJ_TN)
    m[0, cols, cols] = 1.0
    perm = np.concatenate([np.arange(lo, hi) for lo, hi in _HEAD_PERM_RUNS])
    m[1, cols, (cols // HEAD_DIM) * HEAD_DIM + perm[cols % HEAD_DIM]] = 1.0
    m[2, cols, cols + N_GATE_LOGITS] = 1.0
    per_group = 3 * HEADS_PER_GROUP
    for g in range(N_KV_GROUPS):
        m[3, g * LANES + np.arange(per_group), g * per_group + np.arange(per_group)] = 1.0
    return jnp.asarray(m, dtype=BF16)


def _pack_kernel(a_ref, b_ref, m_ref, o_ref):
    row = lax.broadcasted_iota(jnp.int32, (LANES, D_MODEL), 0)
    tail = jnp.where(row < N_GATE_LOGITS, b_ref[...], 0.0)
    win = jnp.concatenate([a_ref[...].astype(BF16), tail.astype(BF16)], axis=0)
    o_ref[...] = _bdot(m_ref[...], win).T.astype(BF16)


def _pack_w_in(w_in_t):
    n_tiles = PROJ_W // PROJ_TN
    mg_lo, mg_hi = COL_MG // PROJ_TN, COL_NG // PROJ_TN

    def kind(n):
        is_perm = jnp.logical_or(_is_query_tile(n), _is_key_tile(n))
        return jnp.where(n >= mg_hi, 3, jnp.where(n >= mg_lo, 2, jnp.where(is_perm, 1, 0)))

    def a_idx(n):
        return jnp.where(n >= mg_hi, mg_lo, n)

    def b_idx(n):
        ratio = PROJ_TN // LANES
        return jnp.where(jnp.logical_and(n >= mg_lo, n < mg_hi), (n + 1) * ratio, 0)

    return pl.pallas_call(
        _pack_kernel,
        grid=(n_tiles,),
        in_specs=[
            pl.BlockSpec((None, PROJ_TN, D_MODEL), lambda n: (0, a_idx(n), 0)),
            pl.BlockSpec((None, LANES, D_MODEL), lambda n: (0, b_idx(n), 0)),
            pl.BlockSpec((None, PROJ_TN, PACK_WIN), lambda n: (kind(n), 0, 0)),
        ],
        out_specs=pl.BlockSpec((D_MODEL, PROJ_TN), lambda n: (0, n)),
        out_shape=jax.ShapeDtypeStruct((D_MODEL, PROJ_W), BF16),
        compiler_params=pltpu.CompilerParams(
            dimension_semantics=("parallel",), vmem_limit_bytes=_vmem(40)),
        name="pack_w_in",
    )(w_in_t, w_in_t, _pack_matrices())


def _pack_compress_weights(pe_k, w1_k, w2_k, pe_v, w1_v, w2_v):
    pe_k = _permute_head(pe_k)
    w1_k = _permute_head(w1_k.reshape(CMP_BLOCK, HEAD_DIM, HEAD_DIM), axis=1).reshape(CMP_BLOCK * HEAD_DIM, HEAD_DIM)
    w2_k = _permute_head(w2_k)
    return (jnp.stack([pe_k, pe_v]), jnp.stack([w1_k, w1_v]).astype(BF16),
            jnp.stack([w2_k, w2_v]).astype(BF16))


def kernel(x, c, w_ada, b_ada, ffn1_pre_g, ffn1_post_g, ffn1_w_gate, ffn1_w_up, ffn1_w_down, mix_pre_g, mix_post_g, w_in, conv_w, conv_b, lru_wr, lru_br, lru_wi, lru_bi, lru_lambda, cmp_pe_k, cmp_w1_k, cmp_w2_k, cmp_pe_v, cmp_w1_v, cmp_w2_v, w_a_out, w_b_out, w_out, ffn2_pre_g, ffn2_post_g, ffn2_w_gate, ffn2_w_up, ffn2_w_down):
    assert x.shape == (BATCH, SEQ, D_MODEL) and w_ada.shape[0] == 1
    mod = _modulation(c, w_ada, b_ada)
    mod3 = mod.reshape(BATCH * N_ADA, 1, D_MODEL)

    x1 = _ffn(x, mod3, 0, ffn1_pre_g, ffn1_post_g, ffn1_w_gate[0].astype(BF16),
              ffn1_w_up[0].astype(BF16), ffn1_w_down[0].astype(BF16),
              x_time_major=False, out_time_major=True)

    cos_t, sin_t = _rope_tables()
    proj2d = _projection(x1, mod3, mix_pre_g, _pack_w_in(jnp.swapaxes(w_in, 1, 2)), cos_t, sin_t)

    hg = _rglru(proj2d, conv_w[0], conv_b, lru_wr[0].astype(BF16), lru_br,
                lru_wi[0].astype(BF16), lru_bi, lru_lambda)

    pe, w1, w2 = _pack_compress_weights(cmp_pe_k[0], cmp_w1_k[0], cmp_w2_k[0],
                                        cmp_pe_v[0], cmp_w1_v[0], cmp_w2_v[0])
    kcvc = _compress(proj2d, pe, w1, w2)

    ob = _attention(proj2d, kcvc)
    ymix = _merge(hg, ob, proj2d, w_a_out[0].astype(BF16), w_b_out[0].astype(BF16))
    x2 = _mix_out(ymix, w_out[0].astype(BF16), x1, mod3, mix_post_g)

    return _ffn(x2, mod3, 6, ffn2_pre_g, ffn2_post_g, ffn2_w_gate[0].astype(BF16),
                ffn2_w_up[0].astype(BF16), ffn2_w_down[0].astype(BF16),
                x_time_major=True, out_time_major=False)
```

```python
import math

import jax
import jax.numpy as jnp
import numpy as np
from jax import lax
from jax.experimental import pallas as pl
from jax.experimental.pallas import tpu as pltpu

F32 = jnp.float32
BF16 = jnp.bfloat16

D_MODEL = 2048
BATCH = 8
SEQ = 2048
D_RNN = D_MODEL
LRU_BLOCKS = 16
LRU_BLOCK_DIM = D_RNN // LRU_BLOCKS
CONV_WIDTH = 4
LRU_C = 8.0
N_HEADS = 16
HEAD_DIM = 128
N_KV_GROUPS = 4
HEADS_PER_GROUP = N_HEADS // N_KV_GROUPS
Q_WIDTH = N_HEADS * HEAD_DIM
KV_WIDTH = N_KV_GROUPS * HEAD_DIM
CMP_STRIDE = 16
CMP_BLOCK = 2 * CMP_STRIDE
SLC_BLOCK = 64
N_SELECT = 16
WINDOW = 512
ROPE_THETA = 500000.0
ROPE_DIM = HEAD_DIM // 4
D_FF = 5632
NORM_EPS = 1e-6
N_ADA = 9
N_SLC = SEQ // SLC_BLOCK
N_CMP_PAD = SEQ // CMP_STRIDE

LANES = 128
SUBLANES = 8
BF16_ROWS = 16

COL_XA = 0
COL_YA = COL_XA + D_RNN
COL_Q = COL_YA + D_RNN
COL_KV = COL_Q + Q_WIDTH
COL_MG = COL_KV + 6 * KV_WIDTH
COL_NG = COL_MG + 2 * D_MODEL
PROJ_TN = 512
NG_PAD = PROJ_TN
PROJ_W = COL_NG + NG_PAD

MASK_NEG = -1e30
M_INIT = -1e29

FFN_TS = 512
FFN_TF = 512
FFN_PRE_ROWS = 48
PROJ_TS = 1024
PROJ_NB = 3
PROJ_MC = 128
OUT_TS = 512
OUT_MC = 256
ATT_T = 256
LRU_TS = 512
LRU_CW = 512
MOD_TN = 1024


def _vmem(mb):
    return mb * 1024 * 1024


def _rms(x, g):
    return x * lax.rsqrt(jnp.mean(x * x, axis=-1, keepdims=True) + NORM_EPS) * g


def _gelu_tanh(x):
    c = math.sqrt(2.0 / math.pi)
    return x * (0.5 * (1.0 + jnp.tanh(c * (x + 0.044715 * (x * x * x)))))


def _bdot(a, b):
    return jnp.dot(a, b, preferred_element_type=F32)


def _bdot_nt(a, b):
    return lax.dot_general(a, b, (((1,), (1,)), ((), ())), preferred_element_type=F32)


def _mod_kernel(c_ref, w_ref, b_ref, o_ref):
    c = c_ref[...]
    ca = c * jax.nn.sigmoid(c)
    o_ref[...] = _bdot(ca.astype(BF16), w_ref[...].astype(BF16)) + b_ref[...]


def _modulation(c, w_ada, b_ada):
    n = N_ADA * D_MODEL
    return pl.pallas_call(
        _mod_kernel,
        grid=(n // MOD_TN,),
        in_specs=[
            pl.BlockSpec((BATCH, D_MODEL), lambda j: (0, 0)),
            pl.BlockSpec((None, D_MODEL, MOD_TN), lambda j: (0, 0, j)),
            pl.BlockSpec((1, MOD_TN), lambda j: (0, j)),
        ],
        out_specs=pl.BlockSpec((BATCH, MOD_TN), lambda j: (0, j)),
        out_shape=jax.ShapeDtypeStruct((BATCH, n), F32),
        compiler_params=pltpu.CompilerParams(
            dimension_semantics=("arbitrary",), vmem_limit_bytes=_vmem(40)),
        name="adaln_mod",
    )(c, w_ada, b_ada)


def _norm_next_rows(step, rows, tile_rows, xn_ref, pre_ref, scn_ref, shn_ref, u_next):
    r0 = pl.multiple_of(jnp.minimum(step * rows, tile_rows - rows), BF16_ROWS)
    un = _rms(xn_ref[pl.ds(r0, rows), :], pre_ref[...]) * (1.0 + scn_ref[...]) + shn_ref[...]
    u_next[pl.ds(r0, rows), :] = un.astype(BF16)


def _ffn_kernel(x_ref, xn_ref, sh_ref, sc_ref, gt_ref, shn_ref, scn_ref, pre_ref, post_ref,
                wg_ref, wu_ref, wd_ref, o_ref, u_scr, acc_scr):
    f = pl.program_id(2)
    tile = pl.program_id(0) * pl.num_programs(1) + pl.program_id(1)
    cur = jnp.bitwise_and(tile, 1)

    @pl.when(jnp.logical_and(tile == 0, f == 0))
    def _():
        u = _rms(x_ref[...], pre_ref[...]) * (1.0 + sc_ref[...]) + sh_ref[...]
        u_scr[0] = u.astype(BF16)

    @pl.when(f == 0)
    def _():
        acc_scr[...] = jnp.zeros_like(acc_scr)

    u = u_scr[cur]
    gate = _bdot(u, wg_ref[...])
    up = _bdot(u, wu_ref[...])
    h = (gate * jax.nn.sigmoid(gate)) * up
    acc_scr[...] += _bdot(h.astype(BF16), wd_ref[...])
    _norm_next_rows(f, FFN_PRE_ROWS, FFN_TS, xn_ref, pre_ref, scn_ref, shn_ref, u_scr.at[1 - cur])

    @pl.when(f == pl.num_programs(2) - 1)
    def _():
        y = _rms(acc_scr[...], post_ref[...])
        o_ref[...] = x_ref[...] + 0.5 * gt_ref[...] * y


def _next_tile(i, b, tile_rows):
    nxt = jnp.minimum(i * BATCH + b + 1, (SEQ // tile_rows) * BATCH - 1)
    return nxt // BATCH, lax.rem(nxt, BATCH)


def _ffn(x, mod3, k_mod, pre_g, post_g, wg, wu, wd, *, x_time_major, out_time_major):
    ts, tf = FFN_TS, FFN_TF
    assert (D_FF // tf) * FFN_PRE_ROWS >= ts

    def tm_spec(idx):
        return pl.BlockSpec((ts, D_MODEL), lambda i, b, f: idx(i, b))

    def bm_spec(idx):
        return pl.BlockSpec((None, ts, D_MODEL), lambda i, b, f: idx(i, b)[::-1] + (0,))

    here = lambda i, b: (i, b)
    nxt = lambda i, b: _next_tile(i, b, ts)
    x_spec, xn_spec = [(tm_spec if x_time_major else bm_spec)(idx) for idx in (here, nxt)]
    if out_time_major:
        o_spec, o_shape = tm_spec(here), jax.ShapeDtypeStruct((SEQ, BATCH * D_MODEL), F32)
    else:
        o_spec, o_shape = bm_spec(here), jax.ShapeDtypeStruct((BATCH, SEQ, D_MODEL), F32)

    def mod_spec(k, idx=here):
        return pl.BlockSpec((None, 1, D_MODEL), lambda i, b, f: (idx(i, b)[1] * N_ADA + k, 0, 0))

    vec_spec = pl.BlockSpec((1, D_MODEL), lambda i, b, f: (0, 0))
    return pl.pallas_call(
        _ffn_kernel,
        grid=(SEQ // ts, BATCH, D_FF // tf),
        in_specs=[
            x_spec, xn_spec, mod_spec(k_mod), mod_spec(k_mod + 1), mod_spec(k_mod + 2),
            mod_spec(k_mod, nxt), mod_spec(k_mod + 1, nxt), vec_spec, vec_spec,
            pl.BlockSpec((D_MODEL, tf), lambda i, b, f: (0, f)),
            pl.BlockSpec((D_MODEL, tf), lambda i, b, f: (0, f)),
            pl.BlockSpec((tf, D_MODEL), lambda i, b, f: (f, 0)),
        ],
        out_specs=o_spec,
        out_shape=o_shape,
        scratch_shapes=[pltpu.VMEM((2, ts, D_MODEL), BF16), pltpu.VMEM((ts, D_MODEL), F32)],
        compiler_params=pltpu.CompilerParams(
            dimension_semantics=("arbitrary", "arbitrary", "arbitrary"), vmem_limit_bytes=_vmem(58)),
        name="macaron_ffn",
    )(x, x, mod3, mod3, mod3, mod3, mod3, pre_g, post_g, wg, wu, wd)


def _rope_slice(x, c, s):
    return x * c + pltpu.roll(x, LANES // 2, axis=1) * s


def _proj_kernel(x_ref, sh_ref, sc_ref, pre_ref, w_ref, *rest):
    tab_refs, o_ref, u_scr = rest[:2 * PROJ_NB], rest[2 * PROJ_NB], rest[2 * PROJ_NB + 1]
    n = pl.program_id(2)

    @pl.when(n == 0)
    def _():
        u = _rms(x_ref[...], pre_ref[...]) * (1.0 + sc_ref[...]) + sh_ref[...]
        u_scr[...] = u.astype(BF16)

    w = w_ref[...]
    for mc in range(PROJ_TS // PROJ_MC):
        rs = slice(mc * PROJ_MC, (mc + 1) * PROJ_MC)
        r = _bdot(u_scr[rs, :], w)
        for k in range(PROJ_NB):
            c = tab_refs[2 * k][rs, :]
            s = tab_refs[2 * k + 1][rs, :]
            for hh in range(PROJ_TN // HEAD_DIM):
                sl = slice(k * PROJ_TN + hh * HEAD_DIM, k * PROJ_TN + (hh + 1) * HEAD_DIM)
                o_ref[rs, sl] = _rope_slice(r[:, sl], c, s).astype(BF16)


def _is_key_tile(n):
    is_k = n == COL_KV // PROJ_TN
    for j in range(1, 3):
        is_k = jnp.logical_or(is_k, n == (COL_KV + 2 * j * KV_WIDTH) // PROJ_TN)
    return is_k


def _is_query_tile(n):
    return jnp.logical_and(n >= COL_Q // PROJ_TN, n < COL_KV // PROJ_TN)


def _rope_kind(n):
    return jnp.where(_is_query_tile(n), 2, jnp.where(_is_key_tile(n), 1, 0))


def _projection(x1, mod3, pre_g, w_all, cos_t, sin_t):
    ts, tn = PROJ_TS, PROJ_NB * PROJ_TN
    nn = PROJ_W // tn

    def mod_spec(k):
        return pl.BlockSpec((None, 1, D_MODEL), lambda i, b, n: (b * N_ADA + k, 0, 0))

    def tab_spec(k):
        return pl.BlockSpec((None, ts, LANES), lambda i, b, n: (_rope_kind(PROJ_NB * n + k), i, 0))

    tab_specs = [tab_spec(k) for k in range(PROJ_NB) for _ in range(2)]
    tabs = [cos_t, sin_t] * PROJ_NB
    return pl.pallas_call(
        _proj_kernel,
        grid=(SEQ // ts, BATCH, nn),
        in_specs=[
            pl.BlockSpec((ts, D_MODEL), lambda i, b, n: (i, b)),
            mod_spec(3), mod_spec(4),
            pl.BlockSpec((1, D_MODEL), lambda i, b, n: (0, 0)),
            pl.BlockSpec((D_MODEL, tn), lambda i, b, n: (0, n)),
        ] + tab_specs,
        out_specs=pl.BlockSpec((ts, tn), lambda i, b, n: (i, b * nn + n)),
        out_shape=jax.ShapeDtypeStruct((SEQ, BATCH * PROJ_W), BF16),
        scratch_shapes=[pltpu.VMEM((ts, D_MODEL), BF16)],
        compiler_params=pltpu.CompilerParams(
            dimension_semantics=("parallel", "parallel", "arbitrary"), vmem_limit_bytes=_vmem(56)),
        name="mix_in_proj",
    )(x1, mod3, mod3, pre_g, w_all, *tabs)


def _lru_kernel(xa_ref, ya_ref, cw_ref, cb_ref, wr_ref, br_ref, wi_ref, bi_ref, lam_ref,
                o_ref, xe_scr, a_scr, b_scr, h_scr):
    tc = pl.program_id(2)
    ts = LRU_TS
    groups = ts // SUBLANES

    @pl.when(tc == 0)
    def _():
        xe_scr[0:SUBLANES, :] = jnp.zeros((SUBLANES, LRU_CW), F32)
        h_scr[...] = jnp.zeros_like(h_scr)

    xe_scr[SUBLANES:SUBLANES + ts, :] = xa_ref[...].astype(F32)
    cw = cw_ref[...]
    lead = SUBLANES - (CONV_WIDTH - 1)
    xc = cb_ref[...] + xe_scr[lead:lead + ts, :] * cw[0:1, :]
    for w in range(1, CONV_WIDTH):
        xc = xc + xe_scr[lead + w:lead + w + ts, :] * cw[w:w + 1, :]
    xe_scr[0:SUBLANES, :] = xe_scr[ts:ts + SUBLANES, :]

    nlam = -lam_ref[...]
    softplus = jnp.maximum(nlam, 0.0) + jnp.log1p(jnp.exp(-jnp.abs(nlam)))
    sub = lax.broadcasted_iota(jnp.int32, (groups, SUBLANES, LRU_BLOCK_DIM), 1)
    for k in range(LRU_CW // LRU_BLOCK_DIM):
        sl = slice(k * LRU_BLOCK_DIM, (k + 1) * LRU_BLOCK_DIM)
        xck = xc[:, sl]
        xb = xck.astype(BF16)
        r = jax.nn.sigmoid(_bdot(xb, wr_ref[k]) + br_ref[:, sl])
        ig = jax.nn.sigmoid(_bdot(xb, wi_ref[k]) + bi_ref[:, sl])
        log_a = (-LRU_C * r) * softplus[:, sl]
        em1 = jnp.tanh(log_a) * (jnp.exp(2.0 * log_a) + 1.0)
        a = jnp.exp(log_a).reshape(groups, SUBLANES, LRU_BLOCK_DIM)
        b = (jnp.sqrt(-em1) * (ig * xck)).reshape(groups, SUBLANES, LRU_BLOCK_DIM)
        d = 1
        while d < SUBLANES:
            a_prev = jnp.where(sub < d, 1.0, pltpu.roll(a, d, axis=1))
            b_prev = jnp.where(sub < d, 0.0, pltpu.roll(b, d, axis=1))
            b = a * b_prev + b
            a = a * a_prev
            d *= 2
        a_scr[:, sl] = a.reshape(ts, LRU_BLOCK_DIM)
        b_scr[:, sl] = b.reshape(ts, LRU_BLOCK_DIM)

    def step(g, h_prev):
        r0 = pl.multiple_of(g * SUBLANES, SUBLANES)
        h = b_scr[pl.ds(r0, SUBLANES), :] + a_scr[pl.ds(r0, SUBLANES), :] * h_prev
        b_scr[pl.ds(r0, SUBLANES), :] = h
        return jnp.broadcast_to(h[SUBLANES - 1:SUBLANES, :], (SUBLANES, LRU_CW))

    h_scr[...] = lax.fori_loop(0, groups, step, h_scr[...], unroll=4)
    o_ref[...] = (b_scr[...] * _gelu_tanh(ya_ref[...].astype(F32))).astype(BF16)


def _rglru(proj2d, conv_w, conv_b, wr, br, wi, bi, lam):
    ts = LRU_TS
    ncb = D_RNN // LRU_CW
    kb = LRU_CW // LRU_BLOCK_DIM
    pw = PROJ_W // LRU_CW
    vec = pl.BlockSpec((1, LRU_CW), lambda b, cb, tc: (0, cb))
    blk = pl.BlockSpec((kb, LRU_BLOCK_DIM, LRU_BLOCK_DIM), lambda b, cb, tc: (cb, 0, 0))
    return pl.pallas_call(
        _lru_kernel,
        grid=(BATCH, ncb, SEQ // ts),
        in_specs=[
            pl.BlockSpec((ts, LRU_CW), lambda b, cb, tc: (tc, b * pw + COL_XA // LRU_CW + cb)),
            pl.BlockSpec((ts, LRU_CW), lambda b, cb, tc: (tc, b * pw + COL_YA // LRU_CW + cb)),
            pl.BlockSpec((CONV_WIDTH, LRU_CW), lambda b, cb, tc: (0, cb)),
            vec, blk, vec, blk, vec, vec,
        ],
        out_specs=pl.BlockSpec((ts, LRU_CW), lambda b, cb, tc: (tc, b * ncb + cb)),
        out_shape=jax.ShapeDtypeStruct((SEQ, BATCH * D_RNN), BF16),
        scratch_shapes=[
            pltpu.VMEM((ts + SUBLANES, LRU_CW), F32),
            pltpu.VMEM((ts, LRU_CW), F32),
            pltpu.VMEM((ts, LRU_CW), F32),
            pltpu.VMEM((SUBLANES, LRU_CW), F32),
        ],
        compiler_params=pltpu.CompilerParams(
            dimension_semantics=("parallel", "parallel", "arbitrary"), vmem_limit_bytes=_vmem(40)),
        name="rglru",
    )(proj2d, proj2d, conv_w, conv_b, wr, br, wi, bi, lam)


def _cmp_kernel(x_ref, pe_ref, w1_ref, w2_ref, o_ref, xs_scr, cat_scr):
    xs_scr[0:SEQ, :] = x_ref[...].astype(F32)
    xs_scr[SEQ:SEQ + CMP_STRIDE, :] = jnp.zeros((CMP_STRIDE, HEAD_DIM), F32)
    pe = pe_ref[...]
    for p in range(CMP_BLOCK):
        tok = xs_scr[pl.ds(p, N_CMP_PAD, stride=CMP_STRIDE), :]
        cat_scr[:, p * HEAD_DIM:(p + 1) * HEAD_DIM] = (tok + pe[p:p + 1, :]).astype(BF16)
    pre = _bdot(cat_scr[...], w1_ref[...])
    o_ref[...] = _bdot(_gelu_tanh(pre).astype(BF16), w2_ref[...]).astype(BF16)


def _compress(proj2d, pe, w1, w2):
    pw_h = PROJ_W // HEAD_DIM
    base = COL_KV // HEAD_DIM
    return pl.pallas_call(
        _cmp_kernel,
        grid=(2, BATCH, N_KV_GROUPS),
        in_specs=[
            pl.BlockSpec((SEQ, HEAD_DIM), lambda t, b, g: (0, b * pw_h + base + t * N_KV_GROUPS + g)),
            pl.BlockSpec((None, CMP_BLOCK, HEAD_DIM), lambda t, b, g: (t, 0, 0)),
            pl.BlockSpec((None, CMP_BLOCK * HEAD_DIM, HEAD_DIM), lambda t, b, g: (t, 0, 0)),
            pl.BlockSpec((None, HEAD_DIM, HEAD_DIM), lambda t, b, g: (t, 0, 0)),
        ],
        out_specs=pl.BlockSpec((None, None, None, N_CMP_PAD, HEAD_DIM), lambda t, b, g: (t, b, g, 0, 0)),
        out_shape=jax.ShapeDtypeStruct((2, BATCH, N_KV_GROUPS, N_CMP_PAD, HEAD_DIM), BF16),
        scratch_shapes=[
            pltpu.VMEM((SEQ + CMP_STRIDE, HEAD_DIM), F32),
            pltpu.VMEM((N_CMP_PAD, CMP_BLOCK * HEAD_DIM), BF16),
        ],
        compiler_params=pltpu.CompilerParams(
            dimension_semantics=("parallel", "parallel", "parallel"), vmem_limit_bytes=_vmem(32)),
        name="kv_compress",
    )(proj2d, pe, w1, w2)


def _attn_kernel(q_ref, ks_ref, vs_ref, kw_ref, vw_ref, kc_ref, vc_ref, gt_ref, o_ref,
                 qx_scr, kx_scr, vx_scr, s0_scr, s1_scr, acc_scr, m_scr, ocmp_scr):
    t = ATT_T
    hpg = HEADS_PER_GROUP
    rows = hpg * t
    i = pl.program_id(2)
    q0 = i * t
    ext = slice(HEAD_DIM, 2 * HEAD_DIM)

    @pl.when(i == 0)
    def _():
        key_blk = jnp.right_shift(lax.broadcasted_iota(jnp.int32, (SEQ, LANES), 0), 6)
        onehot = jnp.where(key_blk == lax.broadcasted_iota(jnp.int32, (SEQ, LANES), 1), 1.0, 0.0)
        ones = jnp.ones((SEQ, HEAD_DIM), BF16)
        kx_scr[0, :, 0:HEAD_DIM] = ks_ref[...]
        kx_scr[0, :, ext] = onehot.astype(BF16)
        kx_scr[1, :, 0:HEAD_DIM] = kw_ref[...]
        kx_scr[1, :, ext] = jnp.zeros((SEQ, HEAD_DIM), BF16)
        vx_scr[0, :, 0:HEAD_DIM] = vs_ref[...]
        vx_scr[0, :, ext] = ones
        vx_scr[1, :, 0:HEAD_DIM] = vw_ref[...]
        vx_scr[1, :, ext] = ones
        qx_scr[1, :, 0:HEAD_DIM] = jnp.zeros((rows, HEAD_DIM), BF16)
        qx_scr[1, :, ext] = jnp.full((rows, HEAD_DIM), MASK_NEG, BF16)

    for hh in range(hpg):
        qx_scr[0, hh * t:(hh + 1) * t, 0:HEAD_DIM] = q_ref[:, hh * HEAD_DIM:(hh + 1) * HEAD_DIM]
    q4 = qx_scr[0, :, 0:HEAD_DIM]

    row = lax.broadcasted_iota(jnp.int32, (rows, LANES), 0)
    lane = lax.broadcasted_iota(jnp.int32, (rows, LANES), 1)
    cmp_ok = (lane * CMP_STRIDE + (CMP_BLOCK - 1)) <= (q0 + jnp.bitwise_and(row, t - 1))
    s = jnp.where(cmp_ok, _bdot_nt(q4, kc_ref[...]), MASK_NEG)
    e = jnp.where(cmp_ok, jnp.exp(s - jnp.max(s, axis=1, keepdims=True)), 0.0)
    den = jnp.sum(e, axis=1, keepdims=True)
    p = e / jnp.where(den > 0.0, den, 1.0)
    ocmp_scr[...] = _bdot(p.astype(BF16), vc_ref[...])
    p_sum = p[0:t]
    for hh in range(1, hpg):
        p_sum = p_sum + p[hh * t:(hh + 1) * t]

    jj = lax.broadcasted_iota(jnp.int32, (LANES, LANES), 0)
    nn = lax.broadcasted_iota(jnp.int32, (LANES, LANES), 1)
    overlap = jnp.logical_and(
        jnp.logical_and(nn * CMP_STRIDE < (jj + 1) * SLC_BLOCK, nn * CMP_STRIDE + CMP_BLOCK > jj * SLC_BLOCK),
        jj < N_SLC)
    overlap_t = jnp.where(overlap, 1.0, 0.0).astype(BF16)
    p_hi = p_sum.astype(BF16)
    p_lo = (p_sum - p_hi.astype(F32)).astype(BF16)
    imp_t = (_bdot_nt(overlap_t, p_hi) + _bdot_nt(overlap_t, p_lo))[0:N_SLC, :]

    blk = lax.broadcasted_iota(jnp.int32, (N_SLC, t), 0)
    pos = q0 + lax.broadcasted_iota(jnp.int32, (N_SLC, t), 1)
    cur = jnp.right_shift(pos, 6)
    forced = jnp.logical_or(blk == 0, jnp.logical_or(blk == cur, blk == cur - 1))
    val = jnp.where(forced, jnp.inf, jnp.where(blk * SLC_BLOCK <= pos, imp_t, -jnp.inf))
    rank = jnp.zeros((N_SLC, t), F32)
    for c in range(N_SLC):
        vc_row = val[c:c + 1, :]
        ahead = jnp.logical_or(vc_row > val, jnp.logical_and(vc_row == val, blk > c))
        rank = rank + jnp.where(ahead, 1.0, 0.0)
    notsel_t = jnp.where(rank < float(N_SELECT), 0.0, MASK_NEG)
    notsel_t = jnp.concatenate([notsel_t, jnp.zeros((LANES - N_SLC, t), F32)], axis=0)
    notsel = notsel_t.T.astype(BF16)
    for hh in range(hpg):
        qx_scr[0, hh * t:(hh + 1) * t, ext] = notsel

    m_scr[...] = jnp.full(m_scr.shape, M_INIT, F32)
    acc_scr[...] = jnp.zeros_like(acc_scr)

    def flash_update(slot, head, s, v_tile):
        sl = slice(head * t, (head + 1) * t)
        m_prev = m_scr[slot, sl, :]
        m_new = jnp.maximum(m_prev, jnp.max(s, axis=1, keepdims=True))
        alpha = jnp.exp(m_prev - m_new)
        p = jnp.exp(s - jnp.concatenate([m_new] * (t // LANES), axis=1))
        pv = _bdot(p.astype(BF16), v_tile)
        acc_scr[slot, sl, :] = jnp.concatenate([alpha, alpha], axis=1) * acc_scr[slot, sl, :] + pv
        m_scr[slot, sl, :] = m_new

    n_int = i + jnp.minimum(i, 1)

    def job(j):
        is_win = jnp.logical_and(j == i, j < n_int)
        src = is_win.astype(jnp.int32)
        variant = (j >= n_int).astype(jnp.int32)
        kt = jnp.where(is_win, i - 1, jnp.minimum(j, i))
        return variant, src, pl.multiple_of(kt * t, t)

    def scores(j, s_scr):
        variant, src, k0 = job(j)
        s_scr[...] = _bdot_nt(qx_scr[variant], kx_scr[src, pl.ds(k0, t), :])

    def update(j, s_scr):
        _, src, k0 = job(j)
        v_tile = vx_scr[src, pl.ds(k0, t), :]
        for hh in range(hpg):
            flash_update(src, hh, s_scr[hh * t:(hh + 1) * t, :], v_tile)

    def pair(pi, carry):
        j = 2 * pi
        scores(j + 1, s1_scr)
        update(j, s0_scr)
        scores(j + 2, s0_scr)
        update(j + 1, s1_scr)
        return carry

    scores(0, s0_scr)
    lax.fori_loop(0, jnp.right_shift(n_int + 1, 1), pair, 0)

    r2 = lax.broadcasted_iota(jnp.int32, (t, t), 0)
    c2 = lax.broadcasted_iota(jnp.int32, (t, t), 1)
    causal_bias = jnp.where(c2 <= r2, 0.0, MASK_NEG)
    band_bias = jnp.where(c2 > r2, 0.0, MASK_NEG)

    def masked_tile(variant, src, k0, bias):
        s = _bdot_nt(qx_scr[variant], kx_scr[src, pl.ds(k0, t), :])
        v_tile = vx_scr[src, pl.ds(k0, t), :]
        for hh in range(hpg):
            flash_update(src, hh, s[hh * t:(hh + 1) * t, :] + bias, v_tile)

    has_far = i >= WINDOW // t
    far_src = has_far.astype(jnp.int32)
    masked_tile(1 - far_src, far_src,
                pl.multiple_of(jnp.maximum(i - WINDOW // t, 0) * t, t), band_bias)
    masked_tile(0, 1, pl.multiple_of(q0, t), causal_bias)
    masked_tile(0, 0, pl.multiple_of(q0, t), causal_bias)

    gates = jax.nn.sigmoid(gt_ref[...].astype(F32))
    acc_s = acc_scr[0]
    acc_w = acc_scr[1]
    o_slc = acc_s[:, 0:HEAD_DIM] / acc_s[:, HEAD_DIM:2 * HEAD_DIM]
    o_win = acc_w[:, 0:HEAD_DIM] / acc_w[:, HEAD_DIM:2 * HEAD_DIM]
    o_cmp = ocmp_scr[...]
    for hh in range(hpg):
        sl = slice(hh * t, (hh + 1) * t)
        o = (gates[:, 3 * hh:3 * hh + 1] * o_cmp[sl]
             + gates[:, 3 * hh + 1:3 * hh + 2] * o_slc[sl]
             + gates[:, 3 * hh + 2:3 * hh + 3] * o_win[sl])
        o_ref[:, hh * HEAD_DIM:(hh + 1) * HEAD_DIM] = o.astype(BF16)


def _attention(proj2d, kcvc):
    t = ATT_T
    g_w = HEADS_PER_GROUP * HEAD_DIM
    rows = HEADS_PER_GROUP * t
    pw_g = PROJ_W // g_w
    pw_h = PROJ_W // HEAD_DIM

    def kv_spec(j):
        base = (COL_KV + j * KV_WIDTH) // HEAD_DIM
        return pl.BlockSpec((SEQ, HEAD_DIM), lambda b, g, i: (0, b * pw_h + base + g))

    def cmp_spec(tsel):
        return pl.BlockSpec((None, None, None, N_CMP_PAD, HEAD_DIM), lambda b, g, i: (tsel, b, g, 0, 0))

    return pl.pallas_call(
        _attn_kernel,
        grid=(BATCH, N_KV_GROUPS, SEQ // t),
        in_specs=[
            pl.BlockSpec((t, g_w), lambda b, g, i: (i, b * pw_g + COL_Q // g_w + g)),
            kv_spec(2), kv_spec(3), kv_spec(4), kv_spec(5),
            cmp_spec(0), cmp_spec(1),
            pl.BlockSpec((t, LANES), lambda b, g, i: (i, b * pw_h + COL_NG // LANES + g)),
        ],
        out_specs=pl.BlockSpec((t, g_w), lambda b, g, i: (i, b * N_KV_GROUPS + g)),
        out_shape=jax.ShapeDtypeStruct((SEQ, BATCH * Q_WIDTH), BF16),
        scratch_shapes=[
            pltpu.VMEM((2, rows, 2 * HEAD_DIM), BF16),
            pltpu.VMEM((2, SEQ, 2 * HEAD_DIM), BF16),
            pltpu.VMEM((2, SEQ, 2 * HEAD_DIM), BF16),
            pltpu.VMEM((rows, t), F32),
            pltpu.VMEM((rows, t), F32),
            pltpu.VMEM((2, rows, 2 * HEAD_DIM), F32),
            pltpu.VMEM((2, rows, LANES), F32),
            pltpu.VMEM((rows, HEAD_DIM), F32),
        ],
        compiler_params=pltpu.CompilerParams(
            dimension_semantics=("parallel", "parallel", "arbitrary"), vmem_limit_bytes=_vmem(40)),
        name="nsa_attention",
    )(proj2d, proj2d, proj2d, proj2d, proj2d, kcvc, kcvc, proj2d)


def _merge_kernel(ha_ref, ob_ref, wa_ref, wb_ref, ga_ref, gb_ref, o_ref):
    for mc in range(OUT_TS // OUT_MC):
        rs = slice(mc * OUT_MC, (mc + 1) * OUT_MC)
        ya = _bdot(ha_ref[rs, :], wa_ref[...])
        yb = _bdot(ob_ref[rs, :], wb_ref[...])
        ga = jax.nn.sigmoid(ga_ref[rs, :].astype(F32))
        gb = jax.nn.sigmoid(gb_ref[rs, :].astype(F32))
        o_ref[rs, :] = (ga * ya + gb * yb).astype(BF16)


def _merge(hg2d, ob2d, proj2d, wa, wb):
    ts, tn = OUT_TS, PROJ_TN
    nn = D_MODEL // tn
    pw = PROJ_W // tn
    return pl.pallas_call(
        _merge_kernel,
        grid=(SEQ // ts, BATCH, nn),
        in_specs=[
            pl.BlockSpec((ts, D_RNN), lambda i, b, n: (i, b)),
            pl.BlockSpec((ts, Q_WIDTH), lambda i, b, n: (i, b)),
            pl.BlockSpec((D_RNN, tn), lambda i, b, n: (0, n)),
            pl.BlockSpec((Q_WIDTH, tn), lambda i, b, n: (0, n)),
            pl.BlockSpec((ts, tn), lambda i, b, n: (i, b * pw + COL_MG // tn + n)),
            pl.BlockSpec((ts, tn), lambda i, b, n: (i, b * pw + (COL_MG + D_MODEL) // tn + n)),
        ],
        out_specs=pl.BlockSpec((ts, tn), lambda i, b, n: (i, b * nn + n)),
        out_shape=jax.ShapeDtypeStruct((SEQ, BATCH * D_MODEL), BF16),
        compiler_params=pltpu.CompilerParams(
            dimension_semantics=("parallel", "parallel", "arbitrary"), vmem_limit_bytes=_vmem(40)),
        name="branch_merge",
    )(hg2d, ob2d, wa, wb, proj2d, proj2d)


def _mixout_kernel(y_ref, w_ref, x_ref, gt_ref, post_ref, o_ref):
    for mc in range(OUT_TS // OUT_MC):
        rs = slice(mc * OUT_MC, (mc + 1) * OUT_MC)
        mixed = _bdot(y_ref[rs, :], w_ref[...])
        o_ref[rs, :] = x_ref[rs, :] + gt_ref[...] * _rms(mixed, post_ref[...])


def _mix_out(ymix, w_out, x1, mod3, post_g):
    ts = OUT_TS
    return pl.pallas_call(
        _mixout_kernel,
        grid=(SEQ // ts, BATCH),
        in_specs=[
            pl.BlockSpec((ts, D_MODEL), lambda i, b: (i, b)),
            pl.BlockSpec((D_MODEL, D_MODEL), lambda i, b: (0, 0)),
            pl.BlockSpec((ts, D_MODEL), lambda i, b: (i, b)),
            pl.BlockSpec((None, 1, D_MODEL), lambda i, b: (b * N_ADA + 5, 0, 0)),
            pl.BlockSpec((1, D_MODEL), lambda i, b: (0, 0)),
        ],
        out_specs=pl.BlockSpec((ts, D_MODEL), lambda i, b: (i, b)),
        out_shape=jax.ShapeDtypeStruct((SEQ, BATCH * D_MODEL), F32),
        compiler_params=pltpu.CompilerParams(
            dimension_semantics=("parallel", "parallel"), vmem_limit_bytes=_vmem(56)),
        name="mix_out",
    )(ymix, w_out, x1, mod3, post_g)


_HEAD_PERM_RUNS = ((0, ROPE_DIM // 2), (ROPE_DIM, LANES // 2 + ROPE_DIM // 2),
                   (ROPE_DIM // 2, ROPE_DIM), (LANES // 2 + ROPE_DIM // 2, HEAD_DIM))


def _permute_head(a, axis=-1):
    return jnp.concatenate([lax.slice_in_dim(a, lo, hi, axis=axis) for lo, hi in _HEAD_PERM_RUNS], axis=axis)


def _rope_tables():
    pos = jnp.arange(SEQ).astype(F32)
    inv_freq = ROPE_THETA ** (-jnp.arange(0, ROPE_DIM, 2, dtype=F32) / ROPE_DIM)
    ang = pos[:, None] * inv_freq[None, :]
    cos, sin = jnp.cos(ang), jnp.sin(ang)
    gap = LANES // 2 - ROPE_DIM // 2
    ones, zeros = jnp.ones((SEQ, gap), F32), jnp.zeros((SEQ, gap), F32)
    cos_t = jnp.concatenate([cos, ones, cos, ones], axis=1)
    sin_t = jnp.concatenate([-sin, zeros, sin, zeros], axis=1)
    scale = HEAD_DIM ** -0.5
    cos_all = jnp.stack([jnp.ones_like(cos_t), cos_t, cos_t * scale])
    sin_all = jnp.stack([jnp.zeros_like(sin_t), sin_t, sin_t * scale])
    return cos_all, sin_all


N_GATE_LOGITS = 3 * N_HEADS
PACK_WIN = PROJ_TN + LANES


def _pack_matrices():
    m = np.zeros((4, PROJ_TN, PACK_WIN), np.float32)
    cols = np.arange(PROJ_TN)
    m[0, cols, cols] = 1.0
    perm = np.concatenate([np.arange(lo, hi) for lo, hi in _HEAD_PERM_RUNS])
    m[1, cols, (cols // HEAD_DIM) * HEAD_DIM + perm[cols % HEAD_DIM]] = 1.0
    m[2, cols, cols + N_GATE_LOGITS] = 1.0
    per_group = 3 * HEADS_PER_GROUP
    for g in range(N_KV_GROUPS):
        m[3, g * LANES + np.arange(per_group), g * per_group + np.arange(per_group)] = 1.0
    return jnp.asarray(m, dtype=BF16)


def _pack_kernel(a_ref, b_ref, m_ref, o_ref):
    row = lax.broadcasted_iota(jnp.int32, (LANES, D_MODEL), 0)
    tail = jnp.where(row < N_GATE_LOGITS, b_ref[...], 0.0)
    win = jnp.concatenate([a_ref[...].astype(BF16), tail.astype(BF16)], axis=0)
    o_ref[...] = _bdot(m_ref[...], win).T.astype(BF16)


def _pack_w_in(w_in_t):
    n_tiles = PROJ_W // PROJ_TN
    mg_lo, mg_hi = COL_MG // PROJ_TN, COL_NG // PROJ_TN

    def kind(n):
        is_perm = jnp.logical_or(_is_query_tile(n), _is_key_tile(n))
        return jnp.where(n >= mg_hi, 3, jnp.where(n >= mg_lo, 2, jnp.where(is_perm, 1, 0)))

    def a_idx(n):
        return jnp.where(n >= mg_hi, mg_lo, n)

    def b_idx(n):
        ratio = PROJ_TN // LANES
        return jnp.where(jnp.logical_and(n >= mg_lo, n < mg_hi), (n + 1) * ratio, 0)

    return pl.pallas_call(
        _pack_kernel,
        grid=(n_tiles,),
        in_specs=[
            pl.BlockSpec((None, PROJ_TN, D_MODEL), lambda n: (0, a_idx(n), 0)),
            pl.BlockSpec((None, LANES, D_MODEL), lambda n: (0, b_idx(n), 0)),
            pl.BlockSpec((None, PROJ_TN, PACK_WIN), lambda n: (kind(n), 0, 0)),
        ],
        out_specs=pl.BlockSpec((D_MODEL, PROJ_TN), lambda n: (0, n)),
        out_shape=jax.ShapeDtypeStruct((D_MODEL, PROJ_W), BF16),
        compiler_params=pltpu.CompilerParams(
            dimension_semantics=("parallel",), vmem_limit_bytes=_vmem(40)),
        name="pack_w_in",
    )(w_in_t, w_in_t, _pack_matrices())


def _pack_compress_weights(pe_k, w1_k, w2_k, pe_v, w1_v, w2_v):
    pe_k = _permute_head(pe_k)
    w1_k = _permute_head(w1_k.reshape(CMP_BLOCK, HEAD_DIM, HEAD_DIM), axis=1).reshape(CMP_BLOCK * HEAD_DIM, HEAD_DIM)
    w2_k = _permute_head(w2_k)
    return (jnp.stack([pe_k, pe_v]), jnp.stack([w1_k, w1_v]).astype(BF16),
            jnp.stack([w2_k, w2_v]).astype(BF16))


def kernel(x, c, w_ada, b_ada, ffn1_pre_g, ffn1_post_g, ffn1_w_gate, ffn1_w_up, ffn1_w_down, mix_pre_g, mix_post_g, w_in, conv_w, conv_b, lru_wr, lru_br, lru_wi, lru_bi, lru_lambda, cmp_pe_k, cmp_w1_k, cmp_w2_k, cmp_pe_v, cmp_w1_v, cmp_w2_v, w_a_out, w_b_out, w_out, ffn2_pre_g, ffn2_post_g, ffn2_w_gate, ffn2_w_up, ffn2_w_down):
    assert x.shape == (BATCH, SEQ, D_MODEL) and w_ada.shape[0] == 1
    mod = _modulation(c, w_ada, b_ada)
    mod3 = mod.reshape(BATCH * N_ADA, 1, D_MODEL)

    x1 = _ffn(x, mod3, 0, ffn1_pre_g, ffn1_post_g, ffn1_w_gate[0].astype(BF16),
              ffn1_w_up[0].astype(BF16), ffn1_w_down[0].astype(BF16),
              x_time_major=False, out_time_major=True)

    cos_t, sin_t = _rope_tables()
    proj2d = _projection(x1, mod3, mix_pre_g, _pack_w_in(jnp.swapaxes(w_in, 1, 2)), cos_t, sin_t)

    hg = _rglru(proj2d, conv_w[0], conv_b, lru_wr[0].astype(BF16), lru_br,
                lru_wi[0].astype(BF16), lru_bi, lru_lambda)

    pe, w1, w2 = _pack_compress_weights(cmp_pe_k[0], cmp_w1_k[0], cmp_w2_k[0],
                                        cmp_pe_v[0], cmp_w1_v[0], cmp_w2_v[0])
    kcvc = _compress(proj2d, pe, w1, w2)

    ob = _attention(proj2d, kcvc)
    ymix = _merge(hg, ob, proj2d, w_a_out[0].astype(BF16), w_b_out[0].astype(BF16))
    x2 = _mix_out(ymix, w_out[0].astype(BF16), x1, mod3, mix_post_g)

    return _ffn(x2, mod3, 6, ffn2_pre_g, ffn2_post_g, ffn2_w_gate[0].astype(BF16),
                ffn2_w_up[0].astype(BF16), ffn2_w_down[0].astype(BF16),
                x_time_major=True, out_time_major=False)
```

```python
import math

import jax
import jax.numpy as jnp
import numpy as np
from jax import lax
from jax.experimental import pallas as pl
from jax.experimental.pallas import tpu as pltpu

F32 = jnp.float32
BF16 = jnp.bfloat16

D_MODEL = 2048
BATCH = 8
SEQ = 2048
D_RNN = D_MODEL
LRU_BLOCKS = 16
LRU_BLOCK_DIM = D_RNN // LRU_BLOCKS
CONV_WIDTH = 4
LRU_C = 8.0
N_HEADS = 16
HEAD_DIM = 128
N_KV_GROUPS = 4
HEADS_PER_GROUP = N_HEADS // N_KV_GROUPS
Q_WIDTH = N_HEADS * HEAD_DIM
KV_WIDTH = N_KV_GROUPS * HEAD_DIM
CMP_STRIDE = 16
CMP_BLOCK = 2 * CMP_STRIDE
SLC_BLOCK = 64
N_SELECT = 16
WINDOW = 512
ROPE_THETA = 500000.0
ROPE_DIM = HEAD_DIM // 4
D_FF = 5632
NORM_EPS = 1e-6
N_ADA = 9
N_SLC = SEQ // SLC_BLOCK
N_CMP_PAD = SEQ // CMP_STRIDE

LANES = 128
SUBLANES = 8
BF16_ROWS = 16

COL_XA = 0
COL_YA = COL_XA + D_RNN
COL_Q = COL_YA + D_RNN
COL_KV = COL_Q + Q_WIDTH
COL_MG = COL_KV + 6 * KV_WIDTH
COL_NG = COL_MG + 2 * D_MODEL
PROJ_TN = 512
NG_PAD = PROJ_TN
PROJ_W = COL_NG + NG_PAD

MASK_NEG = -1e30
M_INIT = -1e29

FFN_TS = 512
FFN_TF = 512
FFN_PRE_ROWS = 48
PROJ_TS = 1024
PROJ_NB = 3
PROJ_MC = 128
OUT_TS = 512
MERGE_NB = 2
OUT_MC = 256
ATT_T = 256
LRU_TS = 512
LRU_CW = 512
MOD_TN = 1024


def _vmem(mb):
    return mb * 1024 * 1024


def _rms(x, g):
    return x * lax.rsqrt(jnp.mean(x * x, axis=-1, keepdims=True) + NORM_EPS) * g


def _gelu_tanh(x):
    c = math.sqrt(2.0 / math.pi)
    return x * (0.5 * (1.0 + jnp.tanh(c * (x + 0.044715 * (x * x * x)))))


def _bdot(a, b):
    return jnp.dot(a, b, preferred_element_type=F32)


def _bdot_nt(a, b):
    return lax.dot_general(a, b, (((1,), (1,)), ((), ())), preferred_element_type=F32)


def _mod_kernel(c_ref, w_ref, b_ref, o_ref):
    c = c_ref[...]
    ca = c * jax.nn.sigmoid(c)
    o_ref[...] = _bdot(ca.astype(BF16), w_ref[...].astype(BF16)) + b_ref[...]


def _modulation(c, w_ada, b_ada):
    n = N_ADA * D_MODEL
    return pl.pallas_call(
        _mod_kernel,
        grid=(n // MOD_TN,),
        in_specs=[
            pl.BlockSpec((BATCH, D_MODEL), lambda j: (0, 0)),
            pl.BlockSpec((None, D_MODEL, MOD_TN), lambda j: (0, 0, j)),
            pl.BlockSpec((1, MOD_TN), lambda j: (0, j)),
        ],
        out_specs=pl.BlockSpec((BATCH, MOD_TN), lambda j: (0, j)),
        out_shape=jax.ShapeDtypeStruct((BATCH, n), F32),
        compiler_params=pltpu.CompilerParams(
            dimension_semantics=("arbitrary",), vmem_limit_bytes=_vmem(40)),
        name="adaln_mod",
    )(c, w_ada, b_ada)


def _norm_next_rows(step, rows, tile_rows, xn_ref, pre_ref, scn_ref, shn_ref, u_next):
    r0 = pl.multiple_of(jnp.minimum(step * rows, tile_rows - rows), BF16_ROWS)
    un = _rms(xn_ref[pl.ds(r0, rows), :], pre_ref[...]) * (1.0 + scn_ref[...]) + shn_ref[...]
    u_next[pl.ds(r0, rows), :] = un.astype(BF16)


def _ffn_kernel(x_ref, xn_ref, sh_ref, sc_ref, gt_ref, shn_ref, scn_ref, pre_ref, post_ref,
                wg_ref, wu_ref, wd_ref, o_ref, u_scr, acc_scr):
    f = pl.program_id(2)
    tile = pl.program_id(0) * pl.num_programs(1) + pl.program_id(1)
    cur = jnp.bitwise_and(tile, 1)

    @pl.when(jnp.logical_and(tile == 0, f == 0))
    def _():
        u = _rms(x_ref[...], pre_ref[...]) * (1.0 + sc_ref[...]) + sh_ref[...]
        u_scr[0] = u.astype(BF16)

    @pl.when(f == 0)
    def _():
        acc_scr[...] = jnp.zeros_like(acc_scr)

    u = u_scr[cur]
    gate = _bdot(u, wg_ref[...])
    up = _bdot(u, wu_ref[...])
    h = (gate * jax.nn.sigmoid(gate)) * up
    acc_scr[...] += _bdot(h.astype(BF16), wd_ref[...])
    _norm_next_rows(f, FFN_PRE_ROWS, FFN_TS, xn_ref, pre_ref, scn_ref, shn_ref, u_scr.at[1 - cur])

    @pl.when(f == pl.num_programs(2) - 1)
    def _():
        y = _rms(acc_scr[...], post_ref[...])
        o_ref[...] = x_ref[...] + 0.5 * gt_ref[...] * y


def _next_tile(i, b, tile_rows):
    nxt = jnp.minimum(i * BATCH + b + 1, (SEQ // tile_rows) * BATCH - 1)
    return nxt // BATCH, lax.rem(nxt, BATCH)


def _ffn(x, mod3, k_mod, pre_g, post_g, wg, wu, wd, *, x_time_major, out_time_major):
    ts, tf = FFN_TS, FFN_TF
    assert (D_FF // tf) * FFN_PRE_ROWS >= ts

    def tm_spec(idx):
        return pl.BlockSpec((ts, D_MODEL), lambda i, b, f: idx(i, b))

    def bm_spec(idx):
        return pl.BlockSpec((None, ts, D_MODEL), lambda i, b, f: idx(i, b)[::-1] + (0,))

    here = lambda i, b: (i, b)
    nxt = lambda i, b: _next_tile(i, b, ts)
    x_spec, xn_spec = [(tm_spec if x_time_major else bm_spec)(idx) for idx in (here, nxt)]
    if out_time_major:
        o_spec, o_shape = tm_spec(here), jax.ShapeDtypeStruct((SEQ, BATCH * D_MODEL), F32)
    else:
        o_spec, o_shape = bm_spec(here), jax.ShapeDtypeStruct((BATCH, SEQ, D_MODEL), F32)

    def mod_spec(k, idx=here):
        return pl.BlockSpec((None, 1, D_MODEL), lambda i, b, f: (idx(i, b)[1] * N_ADA + k, 0, 0))

    vec_spec = pl.BlockSpec((1, D_MODEL), lambda i, b, f: (0, 0))
    return pl.pallas_call(
        _ffn_kernel,
        grid=(SEQ // ts, BATCH, D_FF // tf),
        in_specs=[
            x_spec, xn_spec, mod_spec(k_mod), mod_spec(k_mod + 1), mod_spec(k_mod + 2),
            mod_spec(k_mod, nxt), mod_spec(k_mod + 1, nxt), vec_spec, vec_spec,
            pl.BlockSpec((D_MODEL, tf), lambda i, b, f: (0, f)),
            pl.BlockSpec((D_MODEL, tf), lambda i, b, f: (0, f)),
            pl.BlockSpec((tf, D_MODEL), lambda i, b, f: (f, 0)),
        ],
        out_specs=o_spec,
        out_shape=o_shape,
        scratch_shapes=[pltpu.VMEM((2, ts, D_MODEL), BF16), pltpu.VMEM((ts, D_MODEL), F32)],
        compiler_params=pltpu.CompilerParams(
            dimension_semantics=("arbitrary", "arbitrary", "arbitrary"), vmem_limit_bytes=_vmem(58)),
        name="macaron_ffn",
    )(x, x, mod3, mod3, mod3, mod3, mod3, pre_g, post_g, wg, wu, wd)


def _rope_slice(x, c, s):
    return x * c + pltpu.roll(x, LANES // 2, axis=1) * s


def _proj_kernel(x_ref, sh_ref, sc_ref, pre_ref, w_ref, *rest):
    tab_refs, o_ref, u_scr = rest[:2 * PROJ_NB], rest[2 * PROJ_NB], rest[2 * PROJ_NB + 1]
    n = pl.program_id(2)

    @pl.when(n == 0)
    def _():
        u = _rms(x_ref[...], pre_ref[...]) * (1.0 + sc_ref[...]) + sh_ref[...]
        u_scr[...] = u.astype(BF16)

    w = w_ref[...]
    for mc in range(PROJ_TS // PROJ_MC):
        rs = slice(mc * PROJ_MC, (mc + 1) * PROJ_MC)
        r = _bdot(u_scr[rs, :], w)
        for k in range(PROJ_NB):
            c = tab_refs[2 * k][rs, :]
            s = tab_refs[2 * k + 1][rs, :]
            for hh in range(PROJ_TN // HEAD_DIM):
                sl = slice(k * PROJ_TN + hh * HEAD_DIM, k * PROJ_TN + (hh + 1) * HEAD_DIM)
                o_ref[rs, sl] = _rope_slice(r[:, sl], c, s).astype(BF16)


def _is_key_tile(n):
    is_k = n == COL_KV // PROJ_TN
    for j in range(1, 3):
        is_k = jnp.logical_or(is_k, n == (COL_KV + 2 * j * KV_WIDTH) // PROJ_TN)
    return is_k


def _is_query_tile(n):
    return jnp.logical_and(n >= COL_Q // PROJ_TN, n < COL_KV // PROJ_TN)


def _rope_kind(n):
    return jnp.where(_is_query_tile(n), 2, jnp.where(_is_key_tile(n), 1, 0))


def _projection(x1, mod3, pre_g, w_all, cos_t, sin_t):
    ts, tn = PROJ_TS, PROJ_NB * PROJ_TN
    nn = PROJ_W // tn

    def mod_spec(k):
        return pl.BlockSpec((None, 1, D_MODEL), lambda i, b, n: (b * N_ADA + k, 0, 0))

    def tab_spec(k):
        return pl.BlockSpec((None, ts, LANES), lambda i, b, n: (_rope_kind(PROJ_NB * n + k), i, 0))

    tab_specs = [tab_spec(k) for k in range(PROJ_NB) for _ in range(2)]
    tabs = [cos_t, sin_t] * PROJ_NB
    return pl.pallas_call(
        _proj_kernel,
        grid=(SEQ // ts, BATCH, nn),
        in_specs=[
            pl.BlockSpec((ts, D_MODEL), lambda i, b, n: (i, b)),
            mod_spec(3), mod_spec(4),
            pl.BlockSpec((1, D_MODEL), lambda i, b, n: (0, 0)),
            pl.BlockSpec((D_MODEL, tn), lambda i, b, n: (0, n)),
        ] + tab_specs,
        out_specs=pl.BlockSpec((ts, tn), lambda i, b, n: (i, b * nn + n)),
        out_shape=jax.ShapeDtypeStruct((SEQ, BATCH * PROJ_W), BF16),
        scratch_shapes=[pltpu.VMEM((ts, D_MODEL), BF16)],
        compiler_params=pltpu.CompilerParams(
            dimension_semantics=("parallel", "parallel", "arbitrary"), vmem_limit_bytes=_vmem(56)),
        name="mix_in_proj",
    )(x1, mod3, mod3, pre_g, w_all, *tabs)


def _lru_kernel(xa_ref, ya_ref, cw_ref, cb_ref, wr_ref, br_ref, wi_ref, bi_ref, lam_ref,
                o_ref, xe_scr, a_scr, b_scr, h_scr):
    tc = pl.program_id(2)
    ts = LRU_TS
    groups = ts // SUBLANES

    @pl.when(tc == 0)
    def _():
        xe_scr[0:SUBLANES, :] = jnp.zeros((SUBLANES, LRU_CW), F32)
        h_scr[...] = jnp.zeros_like(h_scr)

    xe_scr[SUBLANES:SUBLANES + ts, :] = xa_ref[...].astype(F32)
    cw = cw_ref[...]
    lead = SUBLANES - (CONV_WIDTH - 1)
    xc = cb_ref[...] + xe_scr[lead:lead + ts, :] * cw[0:1, :]
    for w in range(1, CONV_WIDTH):
        xc = xc + xe_scr[lead + w:lead + w + ts, :] * cw[w:w + 1, :]
    xe_scr[0:SUBLANES, :] = xe_scr[ts:ts + SUBLANES, :]

    nlam = -lam_ref[...]
    softplus = jnp.maximum(nlam, 0.0) + jnp.log1p(jnp.exp(-jnp.abs(nlam)))
    sub = lax.broadcasted_iota(jnp.int32, (groups, SUBLANES, LRU_BLOCK_DIM), 1)
    for k in range(LRU_CW // LRU_BLOCK_DIM):
        sl = slice(k * LRU_BLOCK_DIM, (k + 1) * LRU_BLOCK_DIM)
        xck = xc[:, sl]
        xb = xck.astype(BF16)
        r = jax.nn.sigmoid(_bdot(xb, wr_ref[k]) + br_ref[:, sl])
        ig = jax.nn.sigmoid(_bdot(xb, wi_ref[k]) + bi_ref[:, sl])
        log_a = (-LRU_C * r) * softplus[:, sl]
        em1 = jnp.tanh(log_a) * (jnp.exp(2.0 * log_a) + 1.0)
        a = jnp.exp(log_a).reshape(groups, SUBLANES, LRU_BLOCK_DIM)
        b = (jnp.sqrt(-em1) * (ig * xck)).reshape(groups, SUBLANES, LRU_BLOCK_DIM)
        d = 1
        while d < SUBLANES:
            a_prev = jnp.where(sub < d, 1.0, pltpu.roll(a, d, axis=1))
            b_prev = jnp.where(sub < d, 0.0, pltpu.roll(b, d, axis=1))
            b = a * b_prev + b
            a = a * a_prev
            d *= 2
        a_scr[:, sl] = a.reshape(ts, LRU_BLOCK_DIM)
        b_scr[:, sl] = b.reshape(ts, LRU_BLOCK_DIM)

    def step(g, h_prev):
        r0 = pl.multiple_of(g * SUBLANES, SUBLANES)
        h = b_scr[pl.ds(r0, SUBLANES), :] + a_scr[pl.ds(r0, SUBLANES), :] * h_prev
        b_scr[pl.ds(r0, SUBLANES), :] = h
        return jnp.broadcast_to(h[SUBLANES - 1:SUBLANES, :], (SUBLANES, LRU_CW))

    h_scr[...] = lax.fori_loop(0, groups, step, h_scr[...], unroll=4)
    o_ref[...] = (b_scr[...] * _gelu_tanh(ya_ref[...].astype(F32))).astype(BF16)


def _rglru(proj2d, conv_w, conv_b, wr, br, wi, bi, lam):
    ts = LRU_TS
    ncb = D_RNN // LRU_CW
    kb = LRU_CW // LRU_BLOCK_DIM
    pw = PROJ_W // LRU_CW
    vec = pl.BlockSpec((1, LRU_CW), lambda b, cb, tc: (0, cb))
    blk = pl.BlockSpec((kb, LRU_BLOCK_DIM, LRU_BLOCK_DIM), lambda b, cb, tc: (cb, 0, 0))
    return pl.pallas_call(
        _lru_kernel,
        grid=(BATCH, ncb, SEQ // ts),
        in_specs=[
            pl.BlockSpec((ts, LRU_CW), lambda b, cb, tc: (tc, b * pw + COL_XA // LRU_CW + cb)),
            pl.BlockSpec((ts, LRU_CW), lambda b, cb, tc: (tc, b * pw + COL_YA // LRU_CW + cb)),
            pl.BlockSpec((CONV_WIDTH, LRU_CW), lambda b, cb, tc: (0, cb)),
            vec, blk, vec, blk, vec, vec,
        ],
        out_specs=pl.BlockSpec((ts, LRU_CW), lambda b, cb, tc: (tc, b * ncb + cb)),
        out_shape=jax.ShapeDtypeStruct((SEQ, BATCH * D_RNN), BF16),
        scratch_shapes=[
            pltpu.VMEM((ts + SUBLANES, LRU_CW), F32),
            pltpu.VMEM((ts, LRU_CW), F32),
            pltpu.VMEM((ts, LRU_CW), F32),
            pltpu.VMEM((SUBLANES, LRU_CW), F32),
        ],
        compiler_params=pltpu.CompilerParams(
            dimension_semantics=("parallel", "parallel", "arbitrary"), vmem_limit_bytes=_vmem(40)),
        name="rglru",
    )(proj2d, proj2d, conv_w, conv_b, wr, br, wi, bi, lam)


def _cmp_kernel(x_ref, pe_ref, w1_ref, w2_ref, o_ref, xs_scr, cat_scr):
    xs_scr[0:SEQ, :] = x_ref[...].astype(F32)
    xs_scr[SEQ:SEQ + CMP_STRIDE, :] = jnp.zeros((CMP_STRIDE, HEAD_DIM), F32)
    pe = pe_ref[...]
    for p in range(CMP_BLOCK):
        tok = xs_scr[pl.ds(p, N_CMP_PAD, stride=CMP_STRIDE), :]
        cat_scr[:, p * HEAD_DIM:(p + 1) * HEAD_DIM] = (tok + pe[p:p + 1, :]).astype(BF16)
    pre = _bdot(cat_scr[...], w1_ref[...])
    o_ref[...] = _bdot(_gelu_tanh(pre).astype(BF16), w2_ref[...]).astype(BF16)


def _compress(proj2d, pe, w1, w2):
    pw_h = PROJ_W // HEAD_DIM
    base = COL_KV // HEAD_DIM
    return pl.pallas_call(
        _cmp_kernel,
        grid=(2, BATCH, N_KV_GROUPS),
        in_specs=[
            pl.BlockSpec((SEQ, HEAD_DIM), lambda t, b, g: (0, b * pw_h + base + t * N_KV_GROUPS + g)),
            pl.BlockSpec((None, CMP_BLOCK, HEAD_DIM), lambda t, b, g: (t, 0, 0)),
            pl.BlockSpec((None, CMP_BLOCK * HEAD_DIM, HEAD_DIM), lambda t, b, g: (t, 0, 0)),
            pl.BlockSpec((None, HEAD_DIM, HEAD_DIM), lambda t, b, g: (t, 0, 0)),
        ],
        out_specs=pl.BlockSpec((None, None, None, N_CMP_PAD, HEAD_DIM), lambda t, b, g: (t, b, g, 0, 0)),
        out_shape=jax.ShapeDtypeStruct((2, BATCH, N_KV_GROUPS, N_CMP_PAD, HEAD_DIM), BF16),
        scratch_shapes=[
            pltpu.VMEM((SEQ + CMP_STRIDE, HEAD_DIM), F32),
            pltpu.VMEM((N_CMP_PAD, CMP_BLOCK * HEAD_DIM), BF16),
        ],
        compiler_params=pltpu.CompilerParams(
            dimension_semantics=("parallel", "parallel", "parallel"), vmem_limit_bytes=_vmem(32)),
        name="kv_compress",
    )(proj2d, pe, w1, w2)


def _attn_kernel(q_ref, ks_ref, vs_ref, kw_ref, vw_ref, kc_ref, vc_ref, gt_ref, o_ref,
                 qx_scr, kx_scr, vx_scr, s0_scr, s1_scr, acc_scr, m_scr, ocmp_scr):
    t = ATT_T
    hpg = HEADS_PER_GROUP
    rows = hpg * t
    i = pl.program_id(2)
    q0 = i * t
    ext = slice(HEAD_DIM, 2 * HEAD_DIM)

    @pl.when(i == 0)
    def _():
        key_blk = jnp.right_shift(lax.broadcasted_iota(jnp.int32, (SEQ, LANES), 0), 6)
        onehot = jnp.where(key_blk == lax.broadcasted_iota(jnp.int32, (SEQ, LANES), 1), 1.0, 0.0)
        ones = jnp.ones((SEQ, HEAD_DIM), BF16)
        kx_scr[0, :, 0:HEAD_DIM] = ks_ref[...]
        kx_scr[0, :, ext] = onehot.astype(BF16)
        kx_scr[1, :, 0:HEAD_DIM] = kw_ref[...]
        kx_scr[1, :, ext] = jnp.zeros((SEQ, HEAD_DIM), BF16)
        vx_scr[0, :, 0:HEAD_DIM] = vs_ref[...]
        vx_scr[0, :, ext] = ones
        vx_scr[1, :, 0:HEAD_DIM] = vw_ref[...]
        vx_scr[1, :, ext] = ones
        qx_scr[1, :, 0:HEAD_DIM] = jnp.zeros((rows, HEAD_DIM), BF16)
        qx_scr[1, :, ext] = jnp.full((rows, HEAD_DIM), MASK_NEG, BF16)

    for hh in range(hpg):
        qx_scr[0, hh * t:(hh + 1) * t, 0:HEAD_DIM] = q_ref[:, hh * HEAD_DIM:(hh + 1) * HEAD_DIM]
    q4 = qx_scr[0, :, 0:HEAD_DIM]

    row = lax.broadcasted_iota(jnp.int32, (rows, LANES), 0)
    lane = lax.broadcasted_iota(jnp.int32, (rows, LANES), 1)
    cmp_ok = (lane * CMP_STRIDE + (CMP_BLOCK - 1)) <= (q0 + jnp.bitwise_and(row, t - 1))
    s = jnp.where(cmp_ok, _bdot_nt(q4, kc_ref[...]), MASK_NEG)
    e = jnp.where(cmp_ok, jnp.exp(s - jnp.max(s, axis=1, keepdims=True)), 0.0)
    den = jnp.sum(e, axis=1, keepdims=True)
    p = e / jnp.where(den > 0.0, den, 1.0)
    ocmp_scr[...] = _bdot(p.astype(BF16), vc_ref[...])
    p_sum = p[0:t]
    for hh in range(1, hpg):
        p_sum = p_sum + p[hh * t:(hh + 1) * t]

    jj = lax.broadcasted_iota(jnp.int32, (LANES, LANES), 0)
    nn = lax.broadcasted_iota(jnp.int32, (LANES, LANES), 1)
    overlap = jnp.logical_and(
        jnp.logical_and(nn * CMP_STRIDE < (jj + 1) * SLC_BLOCK, nn * CMP_STRIDE + CMP_BLOCK > jj * SLC_BLOCK),
        jj < N_SLC)
    overlap_t = jnp.where(overlap, 1.0, 0.0).astype(BF16)
    p_hi = p_sum.astype(BF16)
    p_lo = (p_sum - p_hi.astype(F32)).astype(BF16)
    imp_t = (_bdot_nt(overlap_t, p_hi) + _bdot_nt(overlap_t, p_lo))[0:N_SLC, :]

    blk = lax.broadcasted_iota(jnp.int32, (N_SLC, t), 0)
    pos = q0 + lax.broadcasted_iota(jnp.int32, (N_SLC, t), 1)
    cur = jnp.right_shift(pos, 6)
    forced = jnp.logical_or(blk == 0, jnp.logical_or(blk == cur, blk == cur - 1))
    val = jnp.where(forced, jnp.inf, jnp.where(blk * SLC_BLOCK <= pos, imp_t, -jnp.inf))
    rank = jnp.zeros((N_SLC, t), F32)
    for c in range(N_SLC):
        vc_row = val[c:c + 1, :]
        ahead = jnp.logical_or(vc_row > val, jnp.logical_and(vc_row == val, blk > c))
        rank = rank + jnp.where(ahead, 1.0, 0.0)
    notsel_t = jnp.where(rank < float(N_SELECT), 0.0, MASK_NEG)
    notsel_t = jnp.concatenate([notsel_t, jnp.zeros((LANES - N_SLC, t), F32)], axis=0)
    notsel = notsel_t.T.astype(BF16)
    for hh in range(hpg):
        qx_scr[0, hh * t:(hh + 1) * t, ext] = notsel

    m_scr[...] = jnp.full(m_scr.shape, M_INIT, F32)
    acc_scr[...] = jnp.zeros_like(acc_scr)

    def flash_update(slot, head, s, v_tile):
        sl = slice(head * t, (head + 1) * t)
        m_prev = m_scr[slot, sl, :]
        m_new = jnp.maximum(m_prev, jnp.max(s, axis=1, keepdims=True))
        alpha = jnp.exp(m_prev - m_new)
        p = jnp.exp(s - jnp.concatenate([m_new] * (t // LANES), axis=1))
        pv = _bdot(p.astype(BF16), v_tile)
        acc_scr[slot, sl, :] = jnp.concatenate([alpha, alpha], axis=1) * acc_scr[slot, sl, :] + pv
        m_scr[slot, sl, :] = m_new

    n_int = i + jnp.minimum(i, 1)

    def job(j):
        is_win = jnp.logical_and(j == i, j < n_int)
        src = is_win.astype(jnp.int32)
        variant = (j >= n_int).astype(jnp.int32)
        kt = jnp.where(is_win, i - 1, jnp.minimum(j, i))
        return variant, src, pl.multiple_of(kt * t, t)

    def scores(j, s_scr):
        variant, src, k0 = job(j)
        s_scr[...] = _bdot_nt(qx_scr[variant], kx_scr[src, pl.ds(k0, t), :])

    def update(j, s_scr):
        _, src, k0 = job(j)
        v_tile = vx_scr[src, pl.ds(k0, t), :]
        for hh in range(hpg):
            flash_update(src, hh, s_scr[hh * t:(hh + 1) * t, :], v_tile)

    def pair(pi, carry):
        j = 2 * pi
        scores(j + 1, s1_scr)
        update(j, s0_scr)
        scores(j + 2, s0_scr)
        update(j + 1, s1_scr)
        return carry

    scores(0, s0_scr)
    lax.fori_loop(0, jnp.right_shift(n_int + 1, 1), pair, 0)

    r2 = lax.broadcasted_iota(jnp.int32, (t, t), 0)
    c2 = lax.broadcasted_iota(jnp.int32, (t, t), 1)
    causal_bias = jnp.where(c2 <= r2, 0.0, MASK_NEG)
    band_bias = jnp.where(c2 > r2, 0.0, MASK_NEG)

    def masked_tile(variant, src, k0, bias):
        s = _bdot_nt(qx_scr[variant], kx_scr[src, pl.ds(k0, t), :])
        v_tile = vx_scr[src, pl.ds(k0, t), :]
        for hh in range(hpg):
            flash_update(src, hh, s[hh * t:(hh + 1) * t, :] + bias, v_tile)

    has_far = i >= WINDOW // t
    far_src = has_far.astype(jnp.int32)
    masked_tile(1 - far_src, far_src,
                pl.multiple_of(jnp.maximum(i - WINDOW // t, 0) * t, t), band_bias)
    masked_tile(0, 1, pl.multiple_of(q0, t), causal_bias)
    masked_tile(0, 0, pl.multiple_of(q0, t), causal_bias)

    gates = jax.nn.sigmoid(gt_ref[...].astype(F32))
    acc_s = acc_scr[0]
    acc_w = acc_scr[1]
    o_slc = acc_s[:, 0:HEAD_DIM] / acc_s[:, HEAD_DIM:2 * HEAD_DIM]
    o_win = acc_w[:, 0:HEAD_DIM] / acc_w[:, HEAD_DIM:2 * HEAD_DIM]
    o_cmp = ocmp_scr[...]
    for hh in range(hpg):
        sl = slice(hh * t, (hh + 1) * t)
        o = (gates[:, 3 * hh:3 * hh + 1] * o_cmp[sl]
             + gates[:, 3 * hh + 1:3 * hh + 2] * o_slc[sl]
             + gates[:, 3 * hh + 2:3 * hh + 3] * o_win[sl])
        o_ref[:, hh * HEAD_DIM:(hh + 1) * HEAD_DIM] = o.astype(BF16)


def _attention(proj2d, kcvc):
    t = ATT_T
    g_w = HEADS_PER_GROUP * HEAD_DIM
    rows = HEADS_PER_GROUP * t
    pw_g = PROJ_W // g_w
    pw_h = PROJ_W // HEAD_DIM

    def kv_spec(j):
        base = (COL_KV + j * KV_WIDTH) // HEAD_DIM
        return pl.BlockSpec((SEQ, HEAD_DIM), lambda b, g, i: (0, b * pw_h + base + g))

    def cmp_spec(tsel):
        return pl.BlockSpec((None, None, None, N_CMP_PAD, HEAD_DIM), lambda b, g, i: (tsel, b, g, 0, 0))

    return pl.pallas_call(
        _attn_kernel,
        grid=(BATCH, N_KV_GROUPS, SEQ // t),
        in_specs=[
            pl.BlockSpec((t, g_w), lambda b, g, i: (i, b * pw_g + COL_Q // g_w + g)),
            kv_spec(2), kv_spec(3), kv_spec(4), kv_spec(5),
            cmp_spec(0), cmp_spec(1),
            pl.BlockSpec((t, LANES), lambda b, g, i: (i, b * pw_h + COL_NG // LANES + g)),
        ],
        out_specs=pl.BlockSpec((t, g_w), lambda b, g, i: (i, b * N_KV_GROUPS + g)),
        out_shape=jax.ShapeDtypeStruct((SEQ, BATCH * Q_WIDTH), BF16),
        scratch_shapes=[
            pltpu.VMEM((2, rows, 2 * HEAD_DIM), BF16),
            pltpu.VMEM((2, SEQ, 2 * HEAD_DIM), BF16),
            pltpu.VMEM((2, SEQ, 2 * HEAD_DIM), BF16),
            pltpu.VMEM((rows, t), F32),
            pltpu.VMEM((rows, t), F32),
            pltpu.VMEM((2, rows, 2 * HEAD_DIM), F32),
            pltpu.VMEM((2, rows, LANES), F32),
            pltpu.VMEM((rows, HEAD_DIM), F32),
        ],
        compiler_params=pltpu.CompilerParams(
            dimension_semantics=("parallel", "parallel", "arbitrary"), vmem_limit_bytes=_vmem(40)),
        name="nsa_attention",
    )(proj2d, proj2d, proj2d, proj2d, proj2d, kcvc, kcvc, proj2d)


def _merge_kernel(ha_ref, ob_ref, wa_ref, wb_ref, *rest):
    gate_refs, o_ref = rest[:2 * MERGE_NB], rest[2 * MERGE_NB]
    for mc in range(OUT_TS // OUT_MC):
        rs = slice(mc * OUT_MC, (mc + 1) * OUT_MC)
        ya = _bdot(ha_ref[rs, :], wa_ref[...])
        yb = _bdot(ob_ref[rs, :], wb_ref[...])
        for k in range(MERGE_NB):
            cs = slice(k * PROJ_TN, (k + 1) * PROJ_TN)
            ga = jax.nn.sigmoid(gate_refs[2 * k][rs, :].astype(F32))
            gb = jax.nn.sigmoid(gate_refs[2 * k + 1][rs, :].astype(F32))
            o_ref[rs, cs] = (ga * ya[:, cs] + gb * yb[:, cs]).astype(BF16)


def _merge(hg2d, ob2d, proj2d, wa, wb):
    ts, tn = OUT_TS, MERGE_NB * PROJ_TN
    nn = D_MODEL // tn
    pw = PROJ_W // PROJ_TN

    def gate_spec(col0, k):
        return pl.BlockSpec(
            (ts, PROJ_TN), lambda i, b, n: (i, b * pw + col0 // PROJ_TN + MERGE_NB * n + k))

    gate_specs = [gate_spec(col0, k) for k in range(MERGE_NB) for col0 in (COL_MG, COL_MG + D_MODEL)]
    return pl.pallas_call(
        _merge_kernel,
        grid=(SEQ // ts, BATCH, nn),
        in_specs=[
            pl.BlockSpec((ts, D_RNN), lambda i, b, n: (i, b)),
            pl.BlockSpec((ts, Q_WIDTH), lambda i, b, n: (i, b)),
            pl.BlockSpec((D_RNN, tn), lambda i, b, n: (0, n)),
            pl.BlockSpec((Q_WIDTH, tn), lambda i, b, n: (0, n)),
        ] + gate_specs,
        out_specs=pl.BlockSpec((ts, tn), lambda i, b, n: (i, b * nn + n)),
        out_shape=jax.ShapeDtypeStruct((SEQ, BATCH * D_MODEL), BF16),
        compiler_params=pltpu.CompilerParams(
            dimension_semantics=("parallel", "parallel", "arbitrary"), vmem_limit_bytes=_vmem(48)),
        name="branch_merge",
    )(hg2d, ob2d, wa, wb, *([proj2d] * (2 * MERGE_NB)))


def _mixout_kernel(y_ref, w_ref, x_ref, gt_ref, post_ref, o_ref):
    for mc in range(OUT_TS // OUT_MC):
        rs = slice(mc * OUT_MC, (mc + 1) * OUT_MC)
        mixed = _bdot(y_ref[rs, :], w_ref[...])
        o_ref[rs, :] = x_ref[rs, :] + gt_ref[...] * _rms(mixed, post_ref[...])


def _mix_out(ymix, w_out, x1, mod3, post_g):
    ts = OUT_TS
    return pl.pallas_call(
        _mixout_kernel,
        grid=(SEQ // ts, BATCH),
        in_specs=[
            pl.BlockSpec((ts, D_MODEL), lambda i, b: (i, b)),
            pl.BlockSpec((D_MODEL, D_MODEL), lambda i, b: (0, 0)),
            pl.BlockSpec((ts, D_MODEL), lambda i, b: (i, b)),
            pl.BlockSpec((None, 1, D_MODEL), lambda i, b: (b * N_ADA + 5, 0, 0)),
            pl.BlockSpec((1, D_MODEL), lambda i, b: (0, 0)),
        ],
        out_specs=pl.BlockSpec((ts, D_MODEL), lambda i, b: (i, b)),
        out_shape=jax.ShapeDtypeStruct((SEQ, BATCH * D_MODEL), F32),
        compiler_params=pltpu.CompilerParams(
            dimension_semantics=("parallel", "parallel"), vmem_limit_bytes=_vmem(56)),
        name="mix_out",
    )(ymix, w_out, x1, mod3, post_g)


_HEAD_PERM_RUNS = ((0, ROPE_DIM // 2), (ROPE_DIM, LANES // 2 + ROPE_DIM // 2),
                   (ROPE_DIM // 2, ROPE_DIM), (LANES // 2 + ROPE_DIM // 2, HEAD_DIM))


def _permute_head(a, axis=-1):
    return jnp.concatenate([lax.slice_in_dim(a, lo, hi, axis=axis) for lo, hi in _HEAD_PERM_RUNS], axis=axis)


def _rope_tables():
    pos = jnp.arange(SEQ).astype(F32)
    inv_freq = ROPE_THETA ** (-jnp.arange(0, ROPE_DIM, 2, dtype=F32) / ROPE_DIM)
    ang = pos[:, None] * inv_freq[None, :]
    cos, sin = jnp.cos(ang), jnp.sin(ang)
    gap = LANES // 2 - ROPE_DIM // 2
    ones, zeros = jnp.ones((SEQ, gap), F32), jnp.zeros((SEQ, gap), F32)
    cos_t = jnp.concatenate([cos, ones, cos, ones], axis=1)
    sin_t = jnp.concatenate([-sin, zeros, sin, zeros], axis=1)
    scale = HEAD_DIM ** -0.5
    cos_all = jnp.stack([jnp.ones_like(cos_t), cos_t, cos_t * scale])
    sin_all = jnp.stack([jnp.zeros_like(sin_t), sin_t, sin_t * scale])
    return cos_all, sin_all


N_GATE_LOGITS = 3 * N_HEADS
PACK_WIN = PROJ_TN + LANES


def _pack_matrices():
    m = np.zeros((4, PROJ_TN, PACK_WIN), np.float32)
    cols = np.arange(PROJ_TN)
    m[0, cols, cols] = 1.0
    perm = np.concatenate([np.arange(lo, hi) for lo, hi in _HEAD_PERM_RUNS])
    m[1, cols, (cols // HEAD_DIM) * HEAD_DIM + perm[cols % HEAD_DIM]] = 1.0
    m[2, cols, cols + N_GATE_LOGITS] = 1.0
    per_group = 3 * HEADS_PER_GROUP
    for g in range(N_KV_GROUPS):
        m[3, g * LANES + np.arange(per_group), g * per_group + np.arange(per_group)] = 1.0
    return jnp.asarray(m, dtype=BF16)


def _pack_kernel(a_ref, b_ref, m_ref, o_ref):
    row = lax.broadcasted_iota(jnp.int32, (LANES, D_MODEL), 0)
    tail = jnp.where(row < N_GATE_LOGITS, b_ref[...], 0.0)
    win = jnp.concatenate([a_ref[...].astype(BF16), tail.astype(BF16)], axis=0)
    o_ref[...] = _bdot(m_ref[...], win).T.astype(BF16)


def _pack_w_in(w_in_t):
    n_tiles = PROJ_W // PROJ_TN
    mg_lo, mg_hi = COL_MG // PROJ_TN, COL_NG // PROJ_TN

    def kind(n):
        is_perm = jnp.logical_or(_is_query_tile(n), _is_key_tile(n))
        return jnp.where(n >= mg_hi, 3, jnp.where(n >= mg_lo, 2, jnp.where(is_perm, 1, 0)))

    def a_idx(n):
        return jnp.where(n >= mg_hi, mg_lo, n)

    def b_idx(n):
        ratio = PROJ_TN // LANES
        return jnp.where(jnp.logical_and(n >= mg_lo, n < mg_hi), (n + 1) * ratio, 0)

    return pl.pallas_call(
        _pack_kernel,
        grid=(n_tiles,),
        in_specs=[
            pl.BlockSpec((None, PROJ_TN, D_MODEL), lambda n: (0, a_idx(n), 0)),
            pl.BlockSpec((None, LANES, D_MODEL), lambda n: (0, b_idx(n), 0)),
            pl.BlockSpec((None, PROJ_TN, PACK_WIN), lambda n: (kind(n), 0, 0)),
        ],
        out_specs=pl.BlockSpec((D_MODEL, PROJ_TN), lambda n: (0, n)),
        out_shape=jax.ShapeDtypeStruct((D_MODEL, PROJ_W), BF16),
        compiler_params=pltpu.CompilerParams(
            dimension_semantics=("parallel",), vmem_limit_bytes=_vmem(40)),
        name="pack_w_in",
    )(w_in_t, w_in_t, _pack_matrices())


def _pack_compress_weights(pe_k, w1_k, w2_k, pe_v, w1_v, w2_v):
    pe_k = _permute_head(pe_k)
    w1_k = _permute_head(w1_k.reshape(CMP_BLOCK, HEAD_DIM, HEAD_DIM), axis=1).reshape(CMP_BLOCK * HEAD_DIM, HEAD_DIM)
    w2_k = _permute_head(w2_k)
    return (jnp.stack([pe_k, pe_v]), jnp.stack([w1_k, w1_v]).astype(BF16),
            jnp.stack([w2_k, w2_v]).astype(BF16))


def kernel(x, c, w_ada, b_ada, ffn1_pre_g, ffn1_post_g, ffn1_w_gate, ffn1_w_up, ffn1_w_down, mix_pre_g, mix_post_g, w_in, conv_w, conv_b, lru_wr, lru_br, lru_wi, lru_bi, lru_lambda, cmp_pe_k, cmp_w1_k, cmp_w2_k, cmp_pe_v, cmp_w1_v, cmp_w2_v, w_a_out, w_b_out, w_out, ffn2_pre_g, ffn2_post_g, ffn2_w_gate, ffn2_w_up, ffn2_w_down):
    assert x.shape == (BATCH, SEQ, D_MODEL) and w_ada.shape[0] == 1
    mod = _modulation(c, w_ada, b_ada)
    mod3 = mod.reshape(BATCH * N_ADA, 1, D_MODEL)

    x1 = _ffn(x, mod3, 0, ffn1_pre_g, ffn1_post_g, ffn1_w_gate[0].astype(BF16),
              ffn1_w_up[0].astype(BF16), ffn1_w_down[0].astype(BF16),
              x_time_major=False, out_time_major=True)

    cos_t, sin_t = _rope_tables()
    proj2d = _projection(x1, mod3, mix_pre_g, _pack_w_in(jnp.swapaxes(w_in, 1, 2)), cos_t, sin_t)

    hg = _rglru(proj2d, conv_w[0], conv_b, lru_wr[0].astype(BF16), lru_br,
                lru_wi[0].astype(BF16), lru_bi, lru_lambda)

    pe, w1, w2 = _pack_compress_weights(cmp_pe_k[0], cmp_w1_k[0], cmp_w2_k[0],
                                        cmp_pe_v[0], cmp_w1_v[0], cmp_w2_v[0])
    kcvc = _compress(proj2d, pe, w1, w2)

    ob = _attention(proj2d, kcvc)
    ymix = _merge(hg, ob, proj2d, w_a_out[0].astype(BF16), w_b_out[0].astype(BF16))
    x2 = _mix_out(ymix, w_out[0].astype(BF16), x1, mod3, mix_post_g)

    return _ffn(x2, mod3, 6, ffn2_pre_g, ffn2_post_g, ffn2_w_gate[0].astype(BF16),
                ffn2_w_up[0].astype(BF16), ffn2_w_down[0].astype(BF16),
                x_time_major=True, out_time_major=False)
```

```python
import math

import jax
import jax.numpy as jnp
import numpy as np
from jax import lax
from jax.experimental import pallas as pl
from jax.experimental.pallas import tpu as pltpu

F32 = jnp.float32
BF16 = jnp.bfloat16

D_MODEL = 2048
BATCH = 8
SEQ = 2048
D_RNN = D_MODEL
LRU_BLOCKS = 16
LRU_BLOCK_DIM = D_RNN // LRU_BLOCKS
CONV_WIDTH = 4
LRU_C = 8.0
N_HEADS = 16
HEAD_DIM = 128
N_KV_GROUPS = 4
HEADS_PER_GROUP = N_HEADS // N_KV_GROUPS
Q_WIDTH = N_HEADS * HEAD_DIM
KV_WIDTH = N_KV_GROUPS * HEAD_DIM
CMP_STRIDE = 16
CMP_BLOCK = 2 * CMP_STRIDE
SLC_BLOCK = 64
N_SELECT = 16
WINDOW = 512
ROPE_THETA = 500000.0
ROPE_DIM = HEAD_DIM // 4
D_FF = 5632
NORM_EPS = 1e-6
N_ADA = 9
N_SLC = SEQ // SLC_BLOCK
N_CMP_PAD = SEQ // CMP_STRIDE

LANES = 128
SUBLANES = 8
BF16_ROWS = 16

COL_XA = 0
COL_YA = COL_XA + D_RNN
COL_Q = COL_YA + D_RNN
COL_KV = COL_Q + Q_WIDTH
COL_MG = COL_KV + 6 * KV_WIDTH
COL_NG = COL_MG + 2 * D_MODEL
PROJ_TN = 512
NG_PAD = PROJ_TN
PROJ_W = COL_NG + NG_PAD

MASK_NEG = -1e30
M_INIT = -1e29

FFN_TS = 512
FFN_TF = 512
FFN_PRE_ROWS = 48
PROJ_TS = 1024
PROJ_NB = 3
PROJ_MC = 128
OUT_TS = 512
MERGE_NB = 2
OUT_MC = 256
ATT_T = 256
LRU_TS = 512
LRU_CW = 512
MIX_TS = 1024
MOD_TN = 1024


def _vmem(mb):
    return mb * 1024 * 1024


def _rms(x, g):
    return x * lax.rsqrt(jnp.mean(x * x, axis=-1, keepdims=True) + NORM_EPS) * g


def _gelu_tanh(x):
    c = math.sqrt(2.0 / math.pi)
    return x * (0.5 * (1.0 + jnp.tanh(c * (x + 0.044715 * (x * x * x)))))


def _bdot(a, b):
    return jnp.dot(a, b, preferred_element_type=F32)


def _bdot_nt(a, b):
    return lax.dot_general(a, b, (((1,), (1,)), ((), ())), preferred_element_type=F32)


def _mod_kernel(c_ref, w_ref, b_ref, o_ref):
    c = c_ref[...]
    ca = c * jax.nn.sigmoid(c)
    o_ref[...] = _bdot(ca.astype(BF16), w_ref[...].astype(BF16)) + b_ref[...]


def _modulation(c, w_ada, b_ada):
    n = N_ADA * D_MODEL
    return pl.pallas_call(
        _mod_kernel,
        grid=(n // MOD_TN,),
        in_specs=[
            pl.BlockSpec((BATCH, D_MODEL), lambda j: (0, 0)),
            pl.BlockSpec((None, D_MODEL, MOD_TN), lambda j: (0, 0, j)),
            pl.BlockSpec((1, MOD_TN), lambda j: (0, j)),
        ],
        out_specs=pl.BlockSpec((BATCH, MOD_TN), lambda j: (0, j)),
        out_shape=jax.ShapeDtypeStruct((BATCH, n), F32),
        compiler_params=pltpu.CompilerParams(
            dimension_semantics=("arbitrary",), vmem_limit_bytes=_vmem(40)),
        name="adaln_mod",
    )(c, w_ada, b_ada)


def _norm_next_rows(step, rows, tile_rows, xn_ref, pre_ref, scn_ref, shn_ref, u_next):
    r0 = pl.multiple_of(jnp.minimum(step * rows, tile_rows - rows), BF16_ROWS)
    un = _rms(xn_ref[pl.ds(r0, rows), :], pre_ref[...]) * (1.0 + scn_ref[...]) + shn_ref[...]
    u_next[pl.ds(r0, rows), :] = un.astype(BF16)


def _ffn_kernel(x_ref, xn_ref, sh_ref, sc_ref, gt_ref, shn_ref, scn_ref, pre_ref, post_ref,
                wg_ref, wu_ref, wd_ref, o_ref, u_scr, acc_scr):
    f = pl.program_id(2)
    tile = pl.program_id(0) * pl.num_programs(1) + pl.program_id(1)
    cur = jnp.bitwise_and(tile, 1)

    @pl.when(jnp.logical_and(tile == 0, f == 0))
    def _():
        u = _rms(x_ref[...], pre_ref[...]) * (1.0 + sc_ref[...]) + sh_ref[...]
        u_scr[0] = u.astype(BF16)

    @pl.when(f == 0)
    def _():
        acc_scr[...] = jnp.zeros_like(acc_scr)

    u = u_scr[cur]
    gate = _bdot(u, wg_ref[...])
    up = _bdot(u, wu_ref[...])
    h = (gate * jax.nn.sigmoid(gate)) * up
    acc_scr[...] += _bdot(h.astype(BF16), wd_ref[...])
    _norm_next_rows(f, FFN_PRE_ROWS, FFN_TS, xn_ref, pre_ref, scn_ref, shn_ref, u_scr.at[1 - cur])

    @pl.when(f == pl.num_programs(2) - 1)
    def _():
        y = _rms(acc_scr[...], post_ref[...])
        o_ref[...] = x_ref[...] + 0.5 * gt_ref[...] * y


def _next_tile(i, b, tile_rows):
    nxt = jnp.minimum(i * BATCH + b + 1, (SEQ // tile_rows) * BATCH - 1)
    return nxt // BATCH, lax.rem(nxt, BATCH)


def _ffn(x, mod3, k_mod, pre_g, post_g, wg, wu, wd, *, x_time_major, out_time_major):
    ts, tf = FFN_TS, FFN_TF
    assert (D_FF // tf) * FFN_PRE_ROWS >= ts

    def tm_spec(idx):
        return pl.BlockSpec((ts, D_MODEL), lambda i, b, f: idx(i, b))

    def bm_spec(idx):
        return pl.BlockSpec((None, ts, D_MODEL), lambda i, b, f: idx(i, b)[::-1] + (0,))

    here = lambda i, b: (i, b)
    nxt = lambda i, b: _next_tile(i, b, ts)
    x_spec, xn_spec = [(tm_spec if x_time_major else bm_spec)(idx) for idx in (here, nxt)]
    if out_time_major:
        o_spec, o_shape = tm_spec(here), jax.ShapeDtypeStruct((SEQ, BATCH * D_MODEL), F32)
    else:
        o_spec, o_shape = bm_spec(here), jax.ShapeDtypeStruct((BATCH, SEQ, D_MODEL), F32)

    def mod_spec(k, idx=here):
        return pl.BlockSpec((None, 1, D_MODEL), lambda i, b, f: (idx(i, b)[1] * N_ADA + k, 0, 0))

    vec_spec = pl.BlockSpec((1, D_MODEL), lambda i, b, f: (0, 0))
    return pl.pallas_call(
        _ffn_kernel,
        grid=(SEQ // ts, BATCH, D_FF // tf),
        in_specs=[
            x_spec, xn_spec, mod_spec(k_mod), mod_spec(k_mod + 1), mod_spec(k_mod + 2),
            mod_spec(k_mod, nxt), mod_spec(k_mod + 1, nxt), vec_spec, vec_spec,
            pl.BlockSpec((D_MODEL, tf), lambda i, b, f: (0, f)),
            pl.BlockSpec((D_MODEL, tf), lambda i, b, f: (0, f)),
            pl.BlockSpec((tf, D_MODEL), lambda i, b, f: (f, 0)),
        ],
        out_specs=o_spec,
        out_shape=o_shape,
        scratch_shapes=[pltpu.VMEM((2, ts, D_MODEL), BF16), pltpu.VMEM((ts, D_MODEL), F32)],
        compiler_params=pltpu.CompilerParams(
            dimension_semantics=("arbitrary", "arbitrary", "arbitrary"), vmem_limit_bytes=_vmem(58)),
        name="macaron_ffn",
    )(x, x, mod3, mod3, mod3, mod3, mod3, pre_g, post_g, wg, wu, wd)


def _rope_slice(x, c, s):
    return x * c + pltpu.roll(x, LANES // 2, axis=1) * s


def _proj_kernel(x_ref, sh_ref, sc_ref, pre_ref, w_ref, *rest):
    tab_refs, o_ref, u_scr = rest[:2 * PROJ_NB], rest[2 * PROJ_NB], rest[2 * PROJ_NB + 1]
    n = pl.program_id(2)

    @pl.when(n == 0)
    def _():
        u = _rms(x_ref[...], pre_ref[...]) * (1.0 + sc_ref[...]) + sh_ref[...]
        u_scr[...] = u.astype(BF16)

    w = w_ref[...]
    for mc in range(PROJ_TS // PROJ_MC):
        rs = slice(mc * PROJ_MC, (mc + 1) * PROJ_MC)
        r = _bdot(u_scr[rs, :], w)
        for k in range(PROJ_NB):
            c = tab_refs[2 * k][rs, :]
            s = tab_refs[2 * k + 1][rs, :]
            for hh in range(PROJ_TN // HEAD_DIM):
                sl = slice(k * PROJ_TN + hh * HEAD_DIM, k * PROJ_TN + (hh + 1) * HEAD_DIM)
                o_ref[rs, sl] = _rope_slice(r[:, sl], c, s).astype(BF16)


def _is_key_tile(n):
    is_k = n == COL_KV // PROJ_TN
    for j in range(1, 3):
        is_k = jnp.logical_or(is_k, n == (COL_KV + 2 * j * KV_WIDTH) // PROJ_TN)
    return is_k


def _is_query_tile(n):
    return jnp.logical_and(n >= COL_Q // PROJ_TN, n < COL_KV // PROJ_TN)


def _rope_kind(n):
    return jnp.where(_is_query_tile(n), 2, jnp.where(_is_key_tile(n), 1, 0))


def _projection(x1, mod3, pre_g, w_all, cos_t, sin_t):
    ts, tn = PROJ_TS, PROJ_NB * PROJ_TN
    nn = PROJ_W // tn

    def mod_spec(k):
        return pl.BlockSpec((None, 1, D_MODEL), lambda i, b, n: (b * N_ADA + k, 0, 0))

    def tab_spec(k):
        return pl.BlockSpec((None, ts, LANES), lambda i, b, n: (_rope_kind(PROJ_NB * n + k), i, 0))

    tab_specs = [tab_spec(k) for k in range(PROJ_NB) for _ in range(2)]
    tabs = [cos_t, sin_t] * PROJ_NB
    return pl.pallas_call(
        _proj_kernel,
        grid=(SEQ // ts, BATCH, nn),
        in_specs=[
            pl.BlockSpec((ts, D_MODEL), lambda i, b, n: (i, b)),
            mod_spec(3), mod_spec(4),
            pl.BlockSpec((1, D_MODEL), lambda i, b, n: (0, 0)),
            pl.BlockSpec((D_MODEL, tn), lambda i, b, n: (0, n)),
        ] + tab_specs,
        out_specs=pl.BlockSpec((ts, tn), lambda i, b, n: (i, b * nn + n)),
        out_shape=jax.ShapeDtypeStruct((SEQ, BATCH * PROJ_W), BF16),
        scratch_shapes=[pltpu.VMEM((ts, D_MODEL), BF16)],
        compiler_params=pltpu.CompilerParams(
            dimension_semantics=("parallel", "parallel", "arbitrary"), vmem_limit_bytes=_vmem(56)),
        name="mix_in_proj",
    )(x1, mod3, mod3, pre_g, w_all, *tabs)


def _lru_kernel(xa_ref, ya_ref, cw_ref, cb_ref, wr_ref, br_ref, wi_ref, bi_ref, lam_ref,
                o_ref, xe_scr, a_scr, b_scr, h_scr):
    tc = pl.program_id(2)
    ts = LRU_TS
    groups = ts // SUBLANES

    @pl.when(tc == 0)
    def _():
        xe_scr[0:SUBLANES, :] = jnp.zeros((SUBLANES, LRU_CW), F32)
        h_scr[...] = jnp.zeros_like(h_scr)

    xe_scr[SUBLANES:SUBLANES + ts, :] = xa_ref[...].astype(F32)
    cw = cw_ref[...]
    lead = SUBLANES - (CONV_WIDTH - 1)
    xc = cb_ref[...] + xe_scr[lead:lead + ts, :] * cw[0:1, :]
    for w in range(1, CONV_WIDTH):
        xc = xc + xe_scr[lead + w:lead + w + ts, :] * cw[w:w + 1, :]
    xe_scr[0:SUBLANES, :] = xe_scr[ts:ts + SUBLANES, :]

    nlam = -lam_ref[...]
    softplus = jnp.maximum(nlam, 0.0) + jnp.log1p(jnp.exp(-jnp.abs(nlam)))
    sub = lax.broadcasted_iota(jnp.int32, (groups, SUBLANES, LRU_BLOCK_DIM), 1)
    for k in range(LRU_CW // LRU_BLOCK_DIM):
        sl = slice(k * LRU_BLOCK_DIM, (k + 1) * LRU_BLOCK_DIM)
        xck = xc[:, sl]
        xb = xck.astype(BF16)
        r = jax.nn.sigmoid(_bdot(xb, wr_ref[k]) + br_ref[:, sl])
        ig = jax.nn.sigmoid(_bdot(xb, wi_ref[k]) + bi_ref[:, sl])
        log_a = (-LRU_C * r) * softplus[:, sl]
        em1 = jnp.tanh(log_a) * (jnp.exp(2.0 * log_a) + 1.0)
        a = jnp.exp(log_a).reshape(groups, SUBLANES, LRU_BLOCK_DIM)
        b = (jnp.sqrt(-em1) * (ig * xck)).reshape(groups, SUBLANES, LRU_BLOCK_DIM)
        d = 1
        while d < SUBLANES:
            a_prev = jnp.where(sub < d, 1.0, pltpu.roll(a, d, axis=1))
            b_prev = jnp.where(sub < d, 0.0, pltpu.roll(b, d, axis=1))
            b = a * b_prev + b
            a = a * a_prev
            d *= 2
        a_scr[:, sl] = a.reshape(ts, LRU_BLOCK_DIM)
        b_scr[:, sl] = b.reshape(ts, LRU_BLOCK_DIM)

    def step(g, h_prev):
        r0 = pl.multiple_of(g * SUBLANES, SUBLANES)
        h = b_scr[pl.ds(r0, SUBLANES), :] + a_scr[pl.ds(r0, SUBLANES), :] * h_prev
        b_scr[pl.ds(r0, SUBLANES), :] = h
        return jnp.broadcast_to(h[SUBLANES - 1:SUBLANES, :], (SUBLANES, LRU_CW))

    h_scr[...] = lax.fori_loop(0, groups, step, h_scr[...], unroll=4)
    o_ref[...] = (b_scr[...] * _gelu_tanh(ya_ref[...].astype(F32))).astype(BF16)


def _rglru(proj2d, conv_w, conv_b, wr, br, wi, bi, lam):
    ts = LRU_TS
    ncb = D_RNN // LRU_CW
    kb = LRU_CW // LRU_BLOCK_DIM
    pw = PROJ_W // LRU_CW
    vec = pl.BlockSpec((1, LRU_CW), lambda b, cb, tc: (0, cb))
    blk = pl.BlockSpec((kb, LRU_BLOCK_DIM, LRU_BLOCK_DIM), lambda b, cb, tc: (cb, 0, 0))
    return pl.pallas_call(
        _lru_kernel,
        grid=(BATCH, ncb, SEQ // ts),
        in_specs=[
            pl.BlockSpec((ts, LRU_CW), lambda b, cb, tc: (tc, b * pw + COL_XA // LRU_CW + cb)),
            pl.BlockSpec((ts, LRU_CW), lambda b, cb, tc: (tc, b * pw + COL_YA // LRU_CW + cb)),
            pl.BlockSpec((CONV_WIDTH, LRU_CW), lambda b, cb, tc: (0, cb)),
            vec, blk, vec, blk, vec, vec,
        ],
        out_specs=pl.BlockSpec((ts, LRU_CW), lambda b, cb, tc: (tc, b * ncb + cb)),
        out_shape=jax.ShapeDtypeStruct((SEQ, BATCH * D_RNN), BF16),
        scratch_shapes=[
            pltpu.VMEM((ts + SUBLANES, LRU_CW), F32),
            pltpu.VMEM((ts, LRU_CW), F32),
            pltpu.VMEM((ts, LRU_CW), F32),
            pltpu.VMEM((SUBLANES, LRU_CW), F32),
        ],
        compiler_params=pltpu.CompilerParams(
            dimension_semantics=("parallel", "parallel", "arbitrary"), vmem_limit_bytes=_vmem(40)),
        name="rglru",
    )(proj2d, proj2d, conv_w, conv_b, wr, br, wi, bi, lam)


def _cmp_kernel(x_ref, pe_ref, w1_ref, w2_ref, o_ref, xs_scr, cat_scr):
    xs_scr[0:SEQ, :] = x_ref[...].astype(F32)
    xs_scr[SEQ:SEQ + CMP_STRIDE, :] = jnp.zeros((CMP_STRIDE, HEAD_DIM), F32)
    pe = pe_ref[...]
    for p in range(CMP_BLOCK):
        tok = xs_scr[pl.ds(p, N_CMP_PAD, stride=CMP_STRIDE), :]
        cat_scr[:, p * HEAD_DIM:(p + 1) * HEAD_DIM] = (tok + pe[p:p + 1, :]).astype(BF16)
    pre = _bdot(cat_scr[...], w1_ref[...])
    o_ref[...] = _bdot(_gelu_tanh(pre).astype(BF16), w2_ref[...]).astype(BF16)


def _compress(proj2d, pe, w1, w2):
    pw_h = PROJ_W // HEAD_DIM
    base = COL_KV // HEAD_DIM
    return pl.pallas_call(
        _cmp_kernel,
        grid=(2, BATCH, N_KV_GROUPS),
        in_specs=[
            pl.BlockSpec((SEQ, HEAD_DIM), lambda t, b, g: (0, b * pw_h + base + t * N_KV_GROUPS + g)),
            pl.BlockSpec((None, CMP_BLOCK, HEAD_DIM), lambda t, b, g: (t, 0, 0)),
            pl.BlockSpec((None, CMP_BLOCK * HEAD_DIM, HEAD_DIM), lambda t, b, g: (t, 0, 0)),
            pl.BlockSpec((None, HEAD_DIM, HEAD_DIM), lambda t, b, g: (t, 0, 0)),
        ],
        out_specs=pl.BlockSpec((None, None, None, N_CMP_PAD, HEAD_DIM), lambda t, b, g: (t, b, g, 0, 0)),
        out_shape=jax.ShapeDtypeStruct((2, BATCH, N_KV_GROUPS, N_CMP_PAD, HEAD_DIM), BF16),
        scratch_shapes=[
            pltpu.VMEM((SEQ + CMP_STRIDE, HEAD_DIM), F32),
            pltpu.VMEM((N_CMP_PAD, CMP_BLOCK * HEAD_DIM), BF16),
        ],
        compiler_params=pltpu.CompilerParams(
            dimension_semantics=("parallel", "parallel", "parallel"), vmem_limit_bytes=_vmem(32)),
        name="kv_compress",
    )(proj2d, pe, w1, w2)


def _attn_kernel(q_ref, ks_ref, vs_ref, kw_ref, vw_ref, kc_ref, vc_ref, gt_ref, o_ref,
                 qx_scr, kx_scr, vx_scr, s0_scr, s1_scr, acc_scr, m_scr, ocmp_scr):
    t = ATT_T
    hpg = HEADS_PER_GROUP
    rows = hpg * t
    i = pl.program_id(2)
    q0 = i * t
    ext = slice(HEAD_DIM, 2 * HEAD_DIM)

    @pl.when(i == 0)
    def _():
        key_blk = jnp.right_shift(lax.broadcasted_iota(jnp.int32, (SEQ, LANES), 0), 6)
        onehot = jnp.where(key_blk == lax.broadcasted_iota(jnp.int32, (SEQ, LANES), 1), 1.0, 0.0)
        ones = jnp.ones((SEQ, HEAD_DIM), BF16)
        kx_scr[0, :, 0:HEAD_DIM] = ks_ref[...]
        kx_scr[0, :, ext] = onehot.astype(BF16)
        kx_scr[1, :, 0:HEAD_DIM] = kw_ref[...]
        kx_scr[1, :, ext] = jnp.zeros((SEQ, HEAD_DIM), BF16)
        vx_scr[0, :, 0:HEAD_DIM] = vs_ref[...]
        vx_scr[0, :, ext] = ones
        vx_scr[1, :, 0:HEAD_DIM] = vw_ref[...]
        vx_scr[1, :, ext] = ones
        qx_scr[1, :, 0:HEAD_DIM] = jnp.zeros((rows, HEAD_DIM), BF16)
        qx_scr[1, :, ext] = jnp.full((rows, HEAD_DIM), MASK_NEG, BF16)

    for hh in range(hpg):
        qx_scr[0, hh * t:(hh + 1) * t, 0:HEAD_DIM] = q_ref[:, hh * HEAD_DIM:(hh + 1) * HEAD_DIM]
    q4 = qx_scr[0, :, 0:HEAD_DIM]

    row = lax.broadcasted_iota(jnp.int32, (rows, LANES), 0)
    lane = lax.broadcasted_iota(jnp.int32, (rows, LANES), 1)
    cmp_ok = (lane * CMP_STRIDE + (CMP_BLOCK - 1)) <= (q0 + jnp.bitwise_and(row, t - 1))
    s = jnp.where(cmp_ok, _bdot_nt(q4, kc_ref[...]), MASK_NEG)
    e = jnp.where(cmp_ok, jnp.exp(s - jnp.max(s, axis=1, keepdims=True)), 0.0)
    den = jnp.sum(e, axis=1, keepdims=True)
    p = e / jnp.where(den > 0.0, den, 1.0)
    ocmp_scr[...] = _bdot(p.astype(BF16), vc_ref[...])
    p_sum = p[0:t]
    for hh in range(1, hpg):
        p_sum = p_sum + p[hh * t:(hh + 1) * t]

    jj = lax.broadcasted_iota(jnp.int32, (LANES, LANES), 0)
    nn = lax.broadcasted_iota(jnp.int32, (LANES, LANES), 1)
    overlap = jnp.logical_and(
        jnp.logical_and(nn * CMP_STRIDE < (jj + 1) * SLC_BLOCK, nn * CMP_STRIDE + CMP_BLOCK > jj * SLC_BLOCK),
        jj < N_SLC)
    overlap_t = jnp.where(overlap, 1.0, 0.0).astype(BF16)
    p_hi = p_sum.astype(BF16)
    p_lo = (p_sum - p_hi.astype(F32)).astype(BF16)
    imp_t = (_bdot_nt(overlap_t, p_hi) + _bdot_nt(overlap_t, p_lo))[0:N_SLC, :]

    blk = lax.broadcasted_iota(jnp.int32, (N_SLC, t), 0)
    pos = q0 + lax.broadcasted_iota(jnp.int32, (N_SLC, t), 1)
    cur = jnp.right_shift(pos, 6)
    forced = jnp.logical_or(blk == 0, jnp.logical_or(blk == cur, blk == cur - 1))
    val = jnp.where(forced, jnp.inf, jnp.where(blk * SLC_BLOCK <= pos, imp_t, -jnp.inf))
    rank = jnp.zeros((N_SLC, t), F32)
    for c in range(N_SLC):
        vc_row = val[c:c + 1, :]
        ahead = jnp.logical_or(vc_row > val, jnp.logical_and(vc_row == val, blk > c))
        rank = rank + jnp.where(ahead, 1.0, 0.0)
    notsel_t = jnp.where(rank < float(N_SELECT), 0.0, MASK_NEG)
    notsel_t = jnp.concatenate([notsel_t, jnp.zeros((LANES - N_SLC, t), F32)], axis=0)
    notsel = notsel_t.T.astype(BF16)
    for hh in range(hpg):
        qx_scr[0, hh * t:(hh + 1) * t, ext] = notsel

    m_scr[...] = jnp.full(m_scr.shape, M_INIT, F32)
    acc_scr[...] = jnp.zeros_like(acc_scr)

    def flash_update(slot, head, s, v_tile):
        sl = slice(head * t, (head + 1) * t)
        m_prev = m_scr[slot, sl, :]
        m_new = jnp.maximum(m_prev, jnp.max(s, axis=1, keepdims=True))
        alpha = jnp.exp(m_prev - m_new)
        p = jnp.exp(s - jnp.concatenate([m_new] * (t // LANES), axis=1))
        pv = _bdot(p.astype(BF16), v_tile)
        acc_scr[slot, sl, :] = jnp.concatenate([alpha, alpha], axis=1) * acc_scr[slot, sl, :] + pv
        m_scr[slot, sl, :] = m_new

    n_int = i + jnp.minimum(i, 1)

    def job(j):
        is_win = jnp.logical_and(j == i, j < n_int)
        src = is_win.astype(jnp.int32)
        variant = (j >= n_int).astype(jnp.int32)
        kt = jnp.where(is_win, i - 1, jnp.minimum(j, i))
        return variant, src, pl.multiple_of(kt * t, t)

    def scores(j, s_scr):
        variant, src, k0 = job(j)
        s_scr[...] = _bdot_nt(qx_scr[variant], kx_scr[src, pl.ds(k0, t), :])

    def update(j, s_scr):
        _, src, k0 = job(j)
        v_tile = vx_scr[src, pl.ds(k0, t), :]
        for hh in range(hpg):
            flash_update(src, hh, s_scr[hh * t:(hh + 1) * t, :], v_tile)

    def pair(pi, carry):
        j = 2 * pi
        scores(j + 1, s1_scr)
        update(j, s0_scr)
        scores(j + 2, s0_scr)
        update(j + 1, s1_scr)
        return carry

    scores(0, s0_scr)
    lax.fori_loop(0, jnp.right_shift(n_int + 1, 1), pair, 0)

    r2 = lax.broadcasted_iota(jnp.int32, (t, t), 0)
    c2 = lax.broadcasted_iota(jnp.int32, (t, t), 1)
    causal_bias = jnp.where(c2 <= r2, 0.0, MASK_NEG)
    band_bias = jnp.where(c2 > r2, 0.0, MASK_NEG)

    def masked_tile(variant, src, k0, bias):
        s = _bdot_nt(qx_scr[variant], kx_scr[src, pl.ds(k0, t), :])
        v_tile = vx_scr[src, pl.ds(k0, t), :]
        for hh in range(hpg):
            flash_update(src, hh, s[hh * t:(hh + 1) * t, :] + bias, v_tile)

    has_far = i >= WINDOW // t
    far_src = has_far.astype(jnp.int32)
    masked_tile(1 - far_src, far_src,
                pl.multiple_of(jnp.maximum(i - WINDOW // t, 0) * t, t), band_bias)
    masked_tile(0, 1, pl.multiple_of(q0, t), causal_bias)
    masked_tile(0, 0, pl.multiple_of(q0, t), causal_bias)

    gates = jax.nn.sigmoid(gt_ref[...].astype(F32))
    acc_s = acc_scr[0]
    acc_w = acc_scr[1]
    o_slc = acc_s[:, 0:HEAD_DIM] / acc_s[:, HEAD_DIM:2 * HEAD_DIM]
    o_win = acc_w[:, 0:HEAD_DIM] / acc_w[:, HEAD_DIM:2 * HEAD_DIM]
    o_cmp = ocmp_scr[...]
    for hh in range(hpg):
        sl = slice(hh * t, (hh + 1) * t)
        o = (gates[:, 3 * hh:3 * hh + 1] * o_cmp[sl]
             + gates[:, 3 * hh + 1:3 * hh + 2] * o_slc[sl]
             + gates[:, 3 * hh + 2:3 * hh + 3] * o_win[sl])
        o_ref[:, hh * HEAD_DIM:(hh + 1) * HEAD_DIM] = o.astype(BF16)


def _attention(proj2d, kcvc):
    t = ATT_T
    g_w = HEADS_PER_GROUP * HEAD_DIM
    rows = HEADS_PER_GROUP * t
    pw_g = PROJ_W // g_w
    pw_h = PROJ_W // HEAD_DIM

    def kv_spec(j):
        base = (COL_KV + j * KV_WIDTH) // HEAD_DIM
        return pl.BlockSpec((SEQ, HEAD_DIM), lambda b, g, i: (0, b * pw_h + base + g))

    def cmp_spec(tsel):
        return pl.BlockSpec((None, None, None, N_CMP_PAD, HEAD_DIM), lambda b, g, i: (tsel, b, g, 0, 0))

    return pl.pallas_call(
        _attn_kernel,
        grid=(BATCH, N_KV_GROUPS, SEQ // t),
        in_specs=[
            pl.BlockSpec((t, g_w), lambda b, g, i: (i, b * pw_g + COL_Q // g_w + g)),
            kv_spec(2), kv_spec(3), kv_spec(4), kv_spec(5),
            cmp_spec(0), cmp_spec(1),
            pl.BlockSpec((t, LANES), lambda b, g, i: (i, b * pw_h + COL_NG // LANES + g)),
        ],
        out_specs=pl.BlockSpec((t, g_w), lambda b, g, i: (i, b * N_KV_GROUPS + g)),
        out_shape=jax.ShapeDtypeStruct((SEQ, BATCH * Q_WIDTH), BF16),
        scratch_shapes=[
            pltpu.VMEM((2, rows, 2 * HEAD_DIM), BF16),
            pltpu.VMEM((2, SEQ, 2 * HEAD_DIM), BF16),
            pltpu.VMEM((2, SEQ, 2 * HEAD_DIM), BF16),
            pltpu.VMEM((rows, t), F32),
            pltpu.VMEM((rows, t), F32),
            pltpu.VMEM((2, rows, 2 * HEAD_DIM), F32),
            pltpu.VMEM((2, rows, LANES), F32),
            pltpu.VMEM((rows, HEAD_DIM), F32),
        ],
        compiler_params=pltpu.CompilerParams(
            dimension_semantics=("parallel", "parallel", "arbitrary"), vmem_limit_bytes=_vmem(40)),
        name="nsa_attention",
    )(proj2d, proj2d, proj2d, proj2d, proj2d, kcvc, kcvc, proj2d)


def _merge_kernel(ha_ref, ob_ref, wa_ref, wb_ref, *rest):
    gate_refs, o_ref = rest[:2 * MERGE_NB], rest[2 * MERGE_NB]
    for mc in range(OUT_TS // OUT_MC):
        rs = slice(mc * OUT_MC, (mc + 1) * OUT_MC)
        ya = _bdot(ha_ref[rs, :], wa_ref[...])
        yb = _bdot(ob_ref[rs, :], wb_ref[...])
        for k in range(MERGE_NB):
            cs = slice(k * PROJ_TN, (k + 1) * PROJ_TN)
            ga = jax.nn.sigmoid(gate_refs[2 * k][rs, :].astype(F32))
            gb = jax.nn.sigmoid(gate_refs[2 * k + 1][rs, :].astype(F32))
            o_ref[rs, cs] = (ga * ya[:, cs] + gb * yb[:, cs]).astype(BF16)


def _merge(hg2d, ob2d, proj2d, wa, wb):
    ts, tn = OUT_TS, MERGE_NB * PROJ_TN
    nn = D_MODEL // tn
    pw = PROJ_W // PROJ_TN

    def gate_spec(col0, k):
        return pl.BlockSpec(
            (ts, PROJ_TN), lambda i, b, n: (i, b * pw + col0 // PROJ_TN + MERGE_NB * n + k))

    gate_specs = [gate_spec(col0, k) for k in range(MERGE_NB) for col0 in (COL_MG, COL_MG + D_MODEL)]
    return pl.pallas_call(
        _merge_kernel,
        grid=(SEQ // ts, BATCH, nn),
        in_specs=[
            pl.BlockSpec((ts, D_RNN), lambda i, b, n: (i, b)),
            pl.BlockSpec((ts, Q_WIDTH), lambda i, b, n: (i, b)),
            pl.BlockSpec((D_RNN, tn), lambda i, b, n: (0, n)),
            pl.BlockSpec((Q_WIDTH, tn), lambda i, b, n: (0, n)),
        ] + gate_specs,
        out_specs=pl.BlockSpec((ts, tn), lambda i, b, n: (i, b * nn + n)),
        out_shape=jax.ShapeDtypeStruct((SEQ, BATCH * D_MODEL), BF16),
        compiler_params=pltpu.CompilerParams(
            dimension_semantics=("parallel", "parallel", "arbitrary"), vmem_limit_bytes=_vmem(48)),
        name="branch_merge",
    )(hg2d, ob2d, wa, wb, *([proj2d] * (2 * MERGE_NB)))


def _mixout_kernel(y_ref, w_ref, x_ref, gt_ref, post_ref, o_ref):
    for mc in range(MIX_TS // OUT_MC):
        rs = slice(mc * OUT_MC, (mc + 1) * OUT_MC)
        mixed = _bdot(y_ref[rs, :], w_ref[...])
        o_ref[rs, :] = x_ref[rs, :] + gt_ref[...] * _rms(mixed, post_ref[...])


def _mix_out(ymix, w_out, x1, mod3, post_g):
    ts = MIX_TS
    return pl.pallas_call(
        _mixout_kernel,
        grid=(SEQ // ts, BATCH),
        in_specs=[
            pl.BlockSpec((ts, D_MODEL), lambda i, b: (i, b)),
            pl.BlockSpec((D_MODEL, D_MODEL), lambda i, b: (0, 0), pipeline_mode=pl.Buffered(1)),
            pl.BlockSpec((ts, D_MODEL), lambda i, b: (i, b)),
            pl.BlockSpec((None, 1, D_MODEL), lambda i, b: (b * N_ADA + 5, 0, 0)),
            pl.BlockSpec((1, D_MODEL), lambda i, b: (0, 0)),
        ],
        out_specs=pl.BlockSpec((ts, D_MODEL), lambda i, b: (i, b)),
        out_shape=jax.ShapeDtypeStruct((SEQ, BATCH * D_MODEL), F32),
        compiler_params=pltpu.CompilerParams(
            dimension_semantics=("parallel", "parallel"), vmem_limit_bytes=_vmem(56)),
        name="mix_out",
    )(ymix, w_out, x1, mod3, post_g)


_HEAD_PERM_RUNS = ((0, ROPE_DIM // 2), (ROPE_DIM, LANES // 2 + ROPE_DIM // 2),
                   (ROPE_DIM // 2, ROPE_DIM), (LANES // 2 + ROPE_DIM // 2, HEAD_DIM))


def _permute_head(a, axis=-1):
    return jnp.concatenate([lax.slice_in_dim(a, lo, hi, axis=axis) for lo, hi in _HEAD_PERM_RUNS], axis=axis)


def _rope_tables():
    pos = jnp.arange(SEQ).astype(F32)
    inv_freq = ROPE_THETA ** (-jnp.arange(0, ROPE_DIM, 2, dtype=F32) / ROPE_DIM)
    ang = pos[:, None] * inv_freq[None, :]
    cos, sin = jnp.cos(ang), jnp.sin(ang)
    gap = LANES // 2 - ROPE_DIM // 2
    ones, zeros = jnp.ones((SEQ, gap), F32), jnp.zeros((SEQ, gap), F32)
    cos_t = jnp.concatenate([cos, ones, cos, ones], axis=1)
    sin_t = jnp.concatenate([-sin, zeros, sin, zeros], axis=1)
    scale = HEAD_DIM ** -0.5
    cos_all = jnp.stack([jnp.ones_like(cos_t), cos_t, cos_t * scale])
    sin_all = jnp.stack([jnp.zeros_like(sin_t), sin_t, sin_t * scale])
    return cos_all, sin_all


N_GATE_LOGITS = 3 * N_HEADS
PACK_WIN = PROJ_TN + LANES


def _pack_matrices():
    m = np.zeros((4, PROJ_TN, PACK_WIN), np.float32)
    cols = np.arange(PROJ_TN)
    m[0, cols, cols] = 1.0
    perm = np.concatenate([np.arange(lo, hi) for lo, hi in _HEAD_PERM_RUNS])
    m[1, cols, (cols // HEAD_DIM) * HEAD_DIM + perm[cols % HEAD_DIM]] = 1.0
    m[2, cols, cols + N_GATE_LOGITS] = 1.0
    per_group = 3 * HEADS_PER_GROUP
    for g in range(N_KV_GROUPS):
        m[3, g * LANES + np.arange(per_group), g * per_group + np.arange(per_group)] = 1.0
    return jnp.asarray(m, dtype=BF16)


def _pack_kernel(a_ref, b_ref, m_ref, o_ref):
    row = lax.broadcasted_iota(jnp.int32, (LANES, D_MODEL), 0)
    tail = jnp.where(row < N_GATE_LOGITS, b_ref[...], 0.0)
    win = jnp.concatenate([a_ref[...].astype(BF16), tail.astype(BF16)], axis=0)
    o_ref[...] = _bdot(m_ref[...], win).T.astype(BF16)


def _pack_w_in(w_in_t):
    n_tiles = PROJ_W // PROJ_TN
    mg_lo, mg_hi = COL_MG // PROJ_TN, COL_NG // PROJ_TN

    def kind(n):
        is_perm = jnp.logical_or(_is_query_tile(n), _is_key_tile(n))
        return jnp.where(n >= mg_hi, 3, jnp.where(n >= mg_lo, 2, jnp.where(is_perm, 1, 0)))

    def a_idx(n):
        return jnp.where(n >= mg_hi, mg_lo, n)

    def b_idx(n):
        ratio = PROJ_TN // LANES
        return jnp.where(jnp.logical_and(n >= mg_lo, n < mg_hi), (n + 1) * ratio, 0)

    return pl.pallas_call(
        _pack_kernel,
        grid=(n_tiles,),
        in_specs=[
            pl.BlockSpec((None, PROJ_TN, D_MODEL), lambda n: (0, a_idx(n), 0)),
            pl.BlockSpec((None, LANES, D_MODEL), lambda n: (0, b_idx(n), 0)),
            pl.BlockSpec((None, PROJ_TN, PACK_WIN), lambda n: (kind(n), 0, 0)),
        ],
        out_specs=pl.BlockSpec((D_MODEL, PROJ_TN), lambda n: (0, n)),
        out_shape=jax.ShapeDtypeStruct((D_MODEL, PROJ_W), BF16),
        compiler_params=pltpu.CompilerParams(
            dimension_semantics=("parallel",), vmem_limit_bytes=_vmem(40)),
        name="pack_w_in",
    )(w_in_t, w_in_t, _pack_matrices())


def _pack_compress_weights(pe_k, w1_k, w2_k, pe_v, w1_v, w2_v):
    pe_k = _permute_head(pe_k)
    w1_k = _permute_head(w1_k.reshape(CMP_BLOCK, HEAD_DIM, HEAD_DIM), axis=1).reshape(CMP_BLOCK * HEAD_DIM, HEAD_DIM)
    w2_k = _permute_head(w2_k)
    return (jnp.stack([pe_k, pe_v]), jnp.stack([w1_k, w1_v]).astype(BF16),
            jnp.stack([w2_k, w2_v]).astype(BF16))


def kernel(x, c, w_ada, b_ada, ffn1_pre_g, ffn1_post_g, ffn1_w_gate, ffn1_w_up, ffn1_w_down, mix_pre_g, mix_post_g, w_in, conv_w, conv_b, lru_wr, lru_br, lru_wi, lru_bi, lru_lambda, cmp_pe_k, cmp_w1_k, cmp_w2_k, cmp_pe_v, cmp_w1_v, cmp_w2_v, w_a_out, w_b_out, w_out, ffn2_pre_g, ffn2_post_g, ffn2_w_gate, ffn2_w_up, ffn2_w_down):
    assert x.shape == (BATCH, SEQ, D_MODEL) and w_ada.shape[0] == 1
    mod = _modulation(c, w_ada, b_ada)
    mod3 = mod.reshape(BATCH * N_ADA, 1, D_MODEL)

    x1 = _ffn(x, mod3, 0, ffn1_pre_g, ffn1_post_g, ffn1_w_gate[0].astype(BF16),
              ffn1_w_up[0].astype(BF16), ffn1_w_down[0].astype(BF16),
              x_time_major=False, out_time_major=True)

    cos_t, sin_t = _rope_tables()
    proj2d = _projection(x1, mod3, mix_pre_g, _pack_w_in(jnp.swapaxes(w_in, 1, 2)), cos_t, sin_t)

    hg = _rglru(proj2d, conv_w[0], conv_b, lru_wr[0].astype(BF16), lru_br,
                lru_wi[0].astype(BF16), lru_bi, lru_lambda)

    pe, w1, w2 = _pack_compress_weights(cmp_pe_k[0], cmp_w1_k[0], cmp_w2_k[0],
                                        cmp_pe_v[0], cmp_w1_v[0], cmp_w2_v[0])
    kcvc = _compress(proj2d, pe, w1, w2)

    ob = _attention(proj2d, kcvc)
    ymix = _merge(hg, ob, proj2d, w_a_out[0].astype(BF16), w_b_out[0].astype(BF16))
    x2 = _mix_out(ymix, w_out[0].astype(BF16), x1, mod3, mix_post_g)

    return _ffn(x2, mod3, 6, ffn2_pre_g, ffn2_post_g, ffn2_w_gate[0].astype(BF16),
                ffn2_w_up[0].astype(BF16), ffn2_w_down[0].astype(BF16),
                x_time_major=True, out_time_major=False)
```
